```python
import math
import jax
import jax.numpy as jnp
from jax import lax
import numpy as np

D_MODEL = 1024
BATCH = 32
SEQ = 256
DEPTH = 2
DEC_BATCH = 8
DEC_SEQ = 4096
PAST_LEN = 512

GRID_W = 64
POS_BASE = 10000.0
MIX_W = D_MODEL
CONV_W = MIX_W // 4
CONV_GROUPS = 4
CONV_K = 31
DN_HEADS = 4
DN_DK = 128
DN_DV = 128
QK_W = DN_HEADS * DN_DK
DN_W = DN_HEADS * DN_DV
DN_SHORT_K = 5
DN_CHUNK = 64
GMLP_W = MIX_W // 4
GMLP_GROUPS = 4
GMLP_CHUNK = 128
IN_SIZES = (2 * CONV_W, 2 * QK_W + DN_W, DN_W, 2 * DN_HEADS, 2 * DN_HEADS, 2 * GMLP_W)
IN_DIM = 2 * CONV_W + 2 * QK_W + 2 * DN_W + 4 * DN_HEADS + 2 * GMLP_W
N_EXPERTS = 32
TOP_K = 4
D_FF = D_MODEL
SWIGLU_ALPHA = 1.702
SWIGLU_LIMIT = 7.0
MOE_BLOCK = 256
DEEPNORM_ALPHA = (2 * DEPTH) ** 0.25
DEEPNORM_BETA = (8 * DEPTH) ** -0.25
EPS = 1e-6

kernel_name = "hybrid_conv_deltanet_gmlp_moe_flow_step"


def _layer_norm(x, gain=None, bias=None):
    xf = x.astype(jnp.float32)
    xc = xf - jnp.mean(xf, axis=-1, keepdims=True)
    y = xc * lax.rsqrt(jnp.mean(xc * xc, axis=-1, keepdims=True) + EPS)
    if gain is not None:
        y = y * gain.astype(jnp.float32) + bias.astype(jnp.float32)
    return y.astype(x.dtype)


def _group_layer_norm(x, groups, gain, bias):
    shp = x.shape
    y = _layer_norm(x.reshape(shp[:-1] + (groups, shp[-1] // groups))).reshape(shp)
    return (y.astype(jnp.float32) * gain.astype(jnp.float32) + bias.astype(jnp.float32)).astype(x.dtype)


def _l2norm(x):
    return x * lax.rsqrt(jnp.sum(x * x, axis=-1, keepdims=True) + EPS)


def _depthwise_conv(x, w):
    pad = w.shape[0] // 2
    return lax.conv_general_dilated(
        x, w[:, None, :].astype(x.dtype), window_strides=(1,), padding=[(pad, pad)],
        dimension_numbers=("NWC", "WIO", "NWC"), feature_group_count=x.shape[-1])


def _grid_pos_embed(t, d, dtype):
    rows = t // GRID_W
    r, col = jnp.meshgrid(jnp.arange(rows), jnp.arange(GRID_W), indexing="ij")
    r = r.reshape(-1).astype(jnp.float32)[:, None]
    col = col.reshape(-1).astype(jnp.float32)[:, None]
    n_freq = d // 4
    omega = 1.0 / (POS_BASE ** (jnp.arange(n_freq, dtype=jnp.float32) / n_freq))
    emb = jnp.concatenate([jnp.sin(r * omega), jnp.cos(r * omega),
                           jnp.sin(col * omega), jnp.cos(col * omega)], axis=-1)
    return emb.astype(dtype)


def _modulation(cond, w_ada, b_ada):
    return jax.nn.silu(cond) @ w_ada + b_ada


def _gated_delta_rule(q, k, v, g, beta, s0):
    bsz, t, h, _ = q.shape
    dv = v.shape[-1]
    c = DN_CHUNK
    n = t // c

    def chunks(a):
        a = a.astype(jnp.float32).reshape((bsz, n, c, h) + a.shape[3:])
        return jnp.moveaxis(a, 3, 1)

    q, k, v, g, beta = chunks(q), chunks(k), chunks(v), chunks(g), chunks(beta)
    gc = jnp.cumsum(g, axis=-1)
    incl = jnp.tril(jnp.ones((c, c), dtype=bool))
    strict = jnp.tril(jnp.ones((c, c), dtype=bool), -1)
    decay = jnp.exp(jnp.where(incl, gc[..., :, None] - gc[..., None, :], -jnp.inf))
    kk = jnp.einsum("bhnid,bhnjd->bhnij", k, k)
    a_mat = jnp.where(strict, beta[..., :, None] * kk * decay, 0.0)
    eye = jnp.eye(c, dtype=jnp.float32)
    t_mat = lax.linalg.triangular_solve(a_mat + eye, jnp.broadcast_to(eye, a_mat.shape),
                                        left_side=True, lower=True, unit_diagonal=True)
    w = jnp.einsum("bhnij,bhnjd->bhnid", t_mat, k * (beta * jnp.exp(gc))[..., None])
    u = jnp.einsum("bhnij,bhnjd->bhnid", t_mat, v * beta[..., None])
    qk = jnp.einsum("bhnid,bhnjd->bhnij", q, k) * decay
    q_dec = q * jnp.exp(gc)[..., None]
    k_dec = k * jnp.exp(gc[..., -1:] - gc)[..., None]
    g_tot = jnp.exp(gc[..., -1])
    xs = tuple(jnp.moveaxis(a, 2, 0) for a in (w, u, qk, q_dec, k_dec, g_tot))

    def step(s, inp):
        w_n, u_n, qk_n, qd_n, kd_n, gt_n = inp
        v_new = u_n - jnp.einsum("bhcd,bhde->bhce", w_n, s)
        o_n = jnp.einsum("bhcd,bhde->bhce", qd_n, s) + jnp.einsum("bhij,bhje->bhie", qk_n, v_new)
        s = s * gt_n[..., None, None] + jnp.einsum("bhcd,bhce->bhde", kd_n, v_new)
        return s, o_n

    s_fin, o = lax.scan(step, s0.astype(jnp.float32), xs)
    o = jnp.transpose(o, (1, 0, 3, 2, 4)).reshape(bsz, t, h, dv)
    return o, s_fin


def _token_mixer(h, s0, p):
    bsz, t, _ = h.shape
    proj = h @ p["w_in"]
    p_conv, p_qkv, p_z, p_beta, p_a, p_gm = jnp.split(
        proj, np.cumsum(IN_SIZES)[:-1].tolist(), axis=-1)

    a_val, a_gate = jnp.split(p_conv, 2, axis=-1)
    ya = _depthwise_conv(a_val * jax.nn.sigmoid(a_gate), p["conv_dw"]) + p["conv_b"]
    ya = jax.nn.silu(_group_layer_norm(ya, CONV_GROUPS, p["conv_ln_g"], p["conv_ln_b"]))

    qkv = jax.nn.silu(_depthwise_conv(p_qkv, p["dn_conv"])).astype(jnp.float32)
    q, k, v = jnp.split(qkv, [QK_W, 2 * QK_W], axis=-1)
    q = _l2norm(q.reshape(bsz, t, DN_HEADS, DN_DK)) * DN_DK ** -0.5
    k = _l2norm(k.reshape(bsz, t, DN_HEADS, DN_DK))
    v = v.reshape(bsz, t, DN_HEADS, DN_DV)
    beta = jax.nn.sigmoid(p_beta.astype(jnp.float32)).reshape(bsz, t, 2, DN_HEADS)
    g = -jnp.exp(p["dn_a_log"].astype(jnp.float32)) * jax.nn.softplus(
        p_a.astype(jnp.float32).reshape(bsz, t, 2, DN_HEADS) + p["dn_dt_bias"].astype(jnp.float32))
    o_f, s_f = _gated_delta_rule(q, k, v, g[:, :, 0], beta[:, :, 0], s0[:, 0])
    rev = lambda a: jnp.flip(a, axis=1)
    o_b, s_b = _gated_delta_rule(rev(q), rev(k), rev(v), rev(g[:, :, 1]), rev(beta[:, :, 1]), s0[:, 1])
    o = o_f + rev(o_b)
    o = o * lax.rsqrt(jnp.mean(o * o, axis=-1, keepdims=True) + EPS) * p["dn_norm_g"].astype(jnp.float32)
    z = jax.nn.silu(p_z.astype(jnp.float32).reshape(bsz, t, DN_HEADS, DN_DV))
    yb = (o * z).reshape(bsz, t, DN_W).astype(h.dtype)

    u, vg = jnp.split(jax.nn.gelu(p_gm), 2, axis=-1)
    n = t // GMLP_CHUNK
    vg = _group_layer_norm(vg, GMLP_GROUPS, p["gm_ln_g"], p["gm_ln_b"]).reshape(
        bsz, n, GMLP_CHUNK, GMLP_GROUPS, GMLP_W // GMLP_GROUPS)
    sg = jnp.einsum("gpq,bnqgc->bnpgc", p["gm_ws"], vg) + jnp.transpose(p["gm_bs"])[:, :, None]
    yc = u * sg.reshape(bsz, t, GMLP_W)

    y = jnp.concatenate([ya, yb, yc], axis=-1) @ p["w_out"]
    return y, jnp.stack([s_f, s_b], axis=1)


def _moe(h, p):
    bsz, t, d = h.shape
    x = h.reshape(-1, d)
    nk = x.shape[0] * TOP_K
    logits = x.astype(jnp.float32) @ p["w_router"].astype(jnp.float32) + p["b_router"].astype(jnp.float32)
    top_val, top_idx = lax.top_k(logits, TOP_K)
    gates = jax.nn.softmax(top_val, axis=-1)
    flat_e = top_idx.reshape(-1)
    order = jnp.argsort(flat_e)
    sorted_e = flat_e[order]
    counts = jnp.bincount(flat_e, length=N_EXPERTS)
    padded = (counts + MOE_BLOCK - 1) // MOE_BLOCK * MOE_BLOCK
    pad_end = jnp.cumsum(padded)
    pad_start = pad_end - padded
    grp_start = jnp.cumsum(counts) - counts
    dest = pad_start[sorted_e] + jnp.arange(nk) - grp_start[sorted_e]
    n_blocks = nk // MOE_BLOCK + N_EXPERTS
    m_pad = n_blocks * MOE_BLOCK
    slot_tok = jnp.zeros((m_pad,), jnp.int32).at[dest].set((order // TOP_K).astype(jnp.int32))
    slot_gate = jnp.zeros((m_pad,), jnp.float32).at[dest].set(gates.reshape(-1)[order])
    block_e = jnp.minimum(jnp.searchsorted(pad_end, jnp.arange(n_blocks) * MOE_BLOCK, side="right"),
                          N_EXPERTS - 1)
    xb = x[slot_tok].reshape(n_blocks, MOE_BLOCK, d)
    w_gu, b_gu, w_down, b_down = p["w_gu"], p["b_gu"], p["w_down"], p["b_down"]

    def expert_block(args):
        xblk, e = args
        gu = xblk @ w_gu[e] + b_gu[e]
        gate, up = jnp.split(gu, 2, axis=-1)
        gate = jnp.minimum(gate, SWIGLU_LIMIT)
        up = jnp.clip(up, -SWIGLU_LIMIT, SWIGLU_LIMIT)
        act = gate * jax.nn.sigmoid(SWIGLU_ALPHA * gate)
        return ((up + 1.0) * act) @ w_down[e] + b_down[e]

    yb = lax.map(expert_block, (xb, block_e)).reshape(m_pad, d)
    y = jnp.zeros_like(x).at[slot_tok].add(yb * slot_gate[:, None].astype(yb.dtype))
    return y.reshape(bsz, t, d)


def _layer(x, mod, s0, p):
    sh1, sc1, gt1, sh2, sc2, gt2 = jnp.split(mod, 6, axis=-1)
    hmix = _layer_norm(x) * (1.0 + sc1) + sh1
    y, s_fin = _token_mixer(hmix, s0, p)
    x = _layer_norm(DEEPNORM_ALPHA * x + gt1 * y, p["ln1_g"], p["ln1_b"])
    hffn = _layer_norm(x) * (1.0 + sc2) + sh2
    x = _layer_norm(DEEPNORM_ALPHA * x + gt2 * _moe(hffn, p), p["ln2_g"], p["ln2_b"])
    return x, s_fin


def setup_inputs(seed: int = 0) -> dict:
    key = jax.random.key(seed)
    ks = iter(jax.random.split(key, 48))
    nrm = lambda shape, scale: jax.random.normal(next(ks), shape, jnp.float32) * scale
    L = DEPTH
    dt = jnp.exp(jax.random.uniform(next(ks), (L, 2, DN_HEADS), jnp.float32,
                                    minval=math.log(1e-3), maxval=math.log(1e-1)))
    a_log = jnp.log(jax.random.uniform(next(ks), (L, 2, DN_HEADS), jnp.float32, minval=1.0, maxval=16.0))
    return {
        "x_prompt": nrm((BATCH, SEQ, D_MODEL), 1.0),
        "x_sample": nrm((DEC_BATCH, DEC_SEQ, D_MODEL), 1.0),
        "state_delta": nrm((DEC_BATCH, DEPTH, 2, DN_HEADS, DN_DK, DN_DV), 0.1),
        "c": nrm((DEC_BATCH, D_MODEL), 1.0),
        "c_ctx": nrm((D_MODEL,), 1.0),
        "w_ada": nrm((L, D_MODEL, 6 * D_MODEL), D_MODEL ** -0.5),
        "b_ada": nrm((L, 6 * D_MODEL), 0.01),
        "w_in": nrm((L, D_MODEL, IN_DIM), D_MODEL ** -0.5),
        "conv_dw": nrm((L, CONV_K, CONV_W), CONV_K ** -0.5),
        "conv_b": nrm((L, CONV_W), 0.01),
        "conv_ln_g": 1.0 + nrm((L, CONV_W), 0.01),
        "conv_ln_b": nrm((L, CONV_W), 0.01),
        "dn_conv": nrm((L, DN_SHORT_K, 2 * QK_W + DN_W), DN_SHORT_K ** -0.5),
        "dn_a_log": a_log,
        "dn_dt_bias": dt + jnp.log(-jnp.expm1(-dt)),
        "dn_norm_g": 1.0 + nrm((L, DN_DV), 0.01),
        "gm_ln_g": 1.0 + nrm((L, GMLP_W), 0.01),
        "gm_ln_b": nrm((L, GMLP_W), 0.01),
        "gm_ws": nrm((L, GMLP_GROUPS, GMLP_CHUNK, GMLP_CHUNK), GMLP_CHUNK ** -0.5),
        "gm_bs": 1.0 + nrm((L, GMLP_GROUPS, GMLP_CHUNK), 0.01),
        "w_out": nrm((L, MIX_W, D_MODEL), DEEPNORM_BETA * MIX_W ** -0.5),
        "ln1_g": 1.0 + nrm((L, D_MODEL), 0.01),
        "ln1_b": nrm((L, D_MODEL), 0.01),
        "ln2_g": 1.0 + nrm((L, D_MODEL), 0.01),
        "ln2_b": nrm((L, D_MODEL), 0.01),
        "w_router": nrm((L, D_MODEL, N_EXPERTS), D_MODEL ** -0.5),
        "b_router": nrm((L, N_EXPERTS), 0.01),
        "w_gu": nrm((L, N_EXPERTS, D_MODEL, 2 * D_FF), D_MODEL ** -0.5),
        "b_gu": nrm((L, N_EXPERTS, 2 * D_FF), 0.01),
        "w_down": nrm((L, N_EXPERTS, D_FF, D_MODEL), DEEPNORM_BETA * D_FF ** -0.5),
        "b_down": nrm((L, N_EXPERTS, D_MODEL), 0.01),
    }


def reference(x_prompt, x_sample, state_delta, c, c_ctx, w_ada, b_ada, w_in, conv_dw, conv_b,
              conv_ln_g, conv_ln_b, dn_conv, dn_a_log, dn_dt_bias, dn_norm_g, gm_ln_g, gm_ln_b,
              gm_ws, gm_bs, w_out, ln1_g, ln1_b, ln2_g, ln2_b, w_router, b_router, w_gu, b_gu,
              w_down, b_down):
    def layer_params(l):
        return {
            "w_in": w_in[l], "conv_dw": conv_dw[l], "conv_b": conv_b[l],
            "conv_ln_g": conv_ln_g[l], "conv_ln_b": conv_ln_b[l], "dn_conv": dn_conv[l],
            "dn_a_log": dn_a_log[l], "dn_dt_bias": dn_dt_bias[l], "dn_norm_g": dn_norm_g[l],
            "gm_ln_g": gm_ln_g[l], "gm_ln_b": gm_ln_b[l], "gm_ws": gm_ws[l], "gm_bs": gm_bs[l],
            "w_out": w_out[l], "ln1_g": ln1_g[l], "ln1_b": ln1_b[l], "ln2_g": ln2_g[l],
            "ln2_b": ln2_b[l], "w_router": w_router[l], "b_router": b_router[l],
            "w_gu": w_gu[l], "b_gu": b_gu[l], "w_down": w_down[l], "b_down": b_down[l],
        }

    xp = x_prompt
    zero_state = jnp.zeros((x_prompt.shape[0], 2, DN_HEADS, DN_DK, DN_DV), jnp.float32)
    ctx_states = []
    for l in range(DEPTH):
        mod_ctx = _modulation(c_ctx, w_ada[l], b_ada[l])[None, None, :]
        xp, s_fin = _layer(xp, mod_ctx, zero_state, layer_params(l))
        ctx_states.append(s_fin)
    new_state_delta = jnp.stack(ctx_states, axis=1).astype(x_prompt.dtype)

    xs = x_sample + _grid_pos_embed(x_sample.shape[1], x_sample.shape[2], x_sample.dtype)[None]
    for l in range(DEPTH):
        mod = _modulation(c, w_ada[l], b_ada[l])[:, None, :]
        xs, _ = _layer(xs, mod, state_delta[:, l], layer_params(l))

    return (xp, xs, new_state_delta)
```

```python
import functools

import jax
import jax.numpy as jnp
from jax import lax
from jax.experimental import pallas as pl
from jax.experimental.pallas import tpu as pltpu

F32 = jnp.float32
BF16 = jnp.bfloat16
HIGHEST = lax.Precision.HIGHEST

D_MODEL = 1024
DEPTH = 2
GRID_W = 64
POS_BASE = 10000.0
CONV_W = 256
CONV_K = 31
CONV_HALO = 16
DN_HEADS = 4
DN_DK = 128
DN_DV = 128
QK_W = DN_HEADS * DN_DK
DN_W = DN_HEADS * DN_DV
DN_SHORT_K = 5
DN_HALO = 8
DN_CHUNK = 64
GMLP_W = 256
GMLP_GROUPS = 4
GMLP_CHUNK = 128
GROUP_W = 64
N_EXPERTS = 32
TOP_K = 4
D_FF = D_MODEL
SWIGLU_ALPHA = 1.702
SWIGLU_LIMIT = 7.0
DEEPNORM_ALPHA = (2 * DEPTH) ** 0.25
EPS = 1e-6

LANES = 128
COL_QKV = 0
COL_CONV = 2 * QK_W + DN_W
COL_Z = COL_CONV + 2 * CONV_W
COL_GM = COL_Z + DN_W
COL_BA = COL_GM + 2 * GMLP_W
PROJ_W = COL_BA + LANES

MIX_TILE = 256
TOK_TILE = 512
MOE_BM = 512
VMEM_LIMIT = 56 * 1024 * 1024


def _ln(x):
    mu = jnp.mean(x, axis=-1, keepdims=True)
    xc = x - mu
    return xc * lax.rsqrt(jnp.mean(xc * xc, axis=-1, keepdims=True) + EPS)


def _sigmoid(x):
    return jax.nn.sigmoid(x)


def _split_bf16(x, parts):
    out = []
    r = x
    for _ in range(parts):
        p = r.astype(BF16)
        out.append(p)
        r = r - p.astype(F32)
    return out


def _dot_exact_rhs(x, m_bf16, parts=3):
    acc = None
    for p in _split_bf16(x, parts):
        t = jnp.dot(p, m_bf16, preferred_element_type=F32)
        acc = t if acc is None else acc + t
    return acc


def _dot_exact_lhs(m_bf16, x, parts=3):
    acc = None
    for p in _split_bf16(x, parts):
        t = jnp.dot(m_bf16, p, preferred_element_type=F32)
        acc = t if acc is None else acc + t
    return acc


def _mod_row(mod_ref, start, n_ctx, dec_seq):
    row = jnp.where(start < n_ctx, 0, 1 + (start - n_ctx) // dec_seq)
    return mod_ref[pl.ds(row, 1), :]


def _mod_kernel(cond_ref, w_ref, b_ref, out_ref):
    c = cond_ref[...]
    s = c * _sigmoid(c)
    out_ref[0] = jnp.dot(s, w_ref[0], preferred_element_type=F32) + b_ref[0]


def _modulation(cond, w_ada, b_ada):
    nl, d, n = w_ada.shape
    r = cond.shape[0]
    tn = 1024
    return pl.pallas_call(
        _mod_kernel,
        grid=(nl, n // tn),
        in_specs=[
            pl.BlockSpec((r, d), lambda l, j: (0, 0)),
            pl.BlockSpec((1, d, tn), lambda l, j: (l, 0, j)),
            pl.BlockSpec((1, 1, tn), lambda l, j: (l, 0, j)),
        ],
        out_specs=pl.BlockSpec((1, r, tn), lambda l, j: (l, 0, j)),
        out_shape=jax.ShapeDtypeStruct((nl, r, n), F32),
        name="adaln_modulation",
    )(cond, w_ada, b_ada.reshape(nl, 1, n))


def _assemble_kernel(xp_ref, xs_ref, pos_ref, out_ref, *, n_ctx_tiles):
    i = pl.program_id(0)

    @pl.when(i < n_ctx_tiles)
    def _():
        out_ref[...] = xp_ref[...]

    @pl.when(i >= n_ctx_tiles)
    def _():
        out_ref[...] = xs_ref[...] + pos_ref[...]


def _assemble_tokens(xp_flat, xs_flat, pos, tile):
    n_ctx, d = xp_flat.shape
    n_den = xs_flat.shape[0]
    ct = n_ctx // tile
    pt = pos.shape[0] // tile
    return pl.pallas_call(
        functools.partial(_assemble_kernel, n_ctx_tiles=ct),
        grid=((n_ctx + n_den) // tile,),
        in_specs=[
            pl.BlockSpec((tile, d), lambda i: (jnp.minimum(i, ct - 1), 0)),
            pl.BlockSpec((tile, d), lambda i: (jnp.maximum(i - ct, 0), 0)),
            pl.BlockSpec((tile, d), lambda i: (jnp.maximum(i - ct, 0) % pt, 0)),
        ],
        out_specs=pl.BlockSpec((tile, d), lambda i: (i, 0)),
        out_shape=jax.ShapeDtypeStruct((n_ctx + n_den, d), F32),
        name="assemble_tokens",
    )(xp_flat, xs_flat, pos)


def _inproj_kernel(x_ref, mod_ref, w_ref, out_ref, *, tile, n_ctx, dec_seq):
    m = _mod_row(mod_ref, pl.program_id(0) * tile, n_ctx, dec_seq)
    sh1 = m[:, 0:D_MODEL]
    sc1 = m[:, D_MODEL:2 * D_MODEL]
    h = _ln(x_ref[...]) * (1.0 + sc1) + sh1
    out_ref[...] = jnp.dot(h.astype(BF16), w_ref[...], preferred_element_type=F32)


def _in_projection(x, mod, w_in_r, n_ctx, dec_seq):
    nt, d = x.shape
    tile = TOK_TILE
    return pl.pallas_call(
        functools.partial(_inproj_kernel, tile=tile, n_ctx=n_ctx, dec_seq=dec_seq),
        grid=(nt // tile,),
        in_specs=[
            pl.BlockSpec((tile, d), lambda i: (i, 0)),
            pl.BlockSpec(mod.shape, lambda i: (0, 0)),
            pl.BlockSpec(w_in_r.shape, lambda i: (0, 0)),
        ],
        out_specs=pl.BlockSpec((tile, PROJ_W), lambda i: (i, 0)),
        out_shape=jax.ShapeDtypeStruct((nt, PROJ_W), F32),
        compiler_params=pltpu.CompilerParams(vmem_limit_bytes=VMEM_LIMIT),
        name="in_projection",
    )(x, mod, w_in_r)


def _group_norm(x, gavg):
    mean = _dot_exact_rhs(x, gavg)
    xc = x - mean
    var = _dot_exact_rhs(xc * xc, gavg)
    return xc * lax.rsqrt(var + EPS)


def _mixpre_kernel(qkv_ref, qkv_p_ref, qkv_n_ref, cv_ref, cv_p_ref, cv_n_ref, gm_ref, ba_ref,
                   convw_ref, convp_ref, dnw_ref, gvec_ref, gmp_ref, ws_ref, bsf_ref, gavg_ref,
                   ya_ref, yc_ref, q_ref, k_ref, v_ref, gates_ref, cbuf, qbuf,
                   *, tile, n_ctx_tiles, ctx_tps, den_tps):
    i = pl.program_id(0)
    pos = jnp.where(i < n_ctx_tiles, i % ctx_tps, (i - n_ctx_tiles) % den_tps)
    tps = jnp.where(i < n_ctx_tiles, ctx_tps, den_tps)
    first = pos == 0
    last = pos == tps - 1
    gavg = gavg_ref[...]

    def glu(p):
        return p[:, :CONV_W] * _sigmoid(p[:, CONV_W:])

    cbuf[0:CONV_HALO, :] = jnp.where(first, 0.0, glu(cv_p_ref[...]))
    cbuf[CONV_HALO:CONV_HALO + tile, :] = glu(cv_ref[...])
    cbuf[CONV_HALO + tile:2 * CONV_HALO + tile, :] = jnp.where(last, 0.0, glu(cv_n_ref[...]))
    conv_b = convp_ref[0:1, :]
    conv_g = convp_ref[1:2, :]
    conv_beta = convp_ref[2:3, :]
    rc = 64
    off = CONV_HALO - CONV_K // 2
    for c in range(tile // rc):
        acc = jnp.zeros((rc, CONV_W), F32)
        for k in range(CONV_K):
            acc = acc + cbuf[c * rc + off + k:c * rc + off + k + rc, :] * convw_ref[k:k + 1, :]
        y = _group_norm(acc + conv_b, gavg) * conv_g + conv_beta
        ya_ref[c * rc:(c + 1) * rc, :] = y * _sigmoid(y)

    qbuf[0:DN_HALO, :] = jnp.where(first, 0.0, qkv_p_ref[...])
    qbuf[DN_HALO:DN_HALO + tile, :] = qkv_ref[...]
    qbuf[DN_HALO + tile:2 * DN_HALO + tile, :] = jnp.where(last, 0.0, qkv_n_ref[...])
    rq = 32
    offq = DN_HALO - DN_SHORT_K // 2
    outs = (q_ref, k_ref, v_ref)
    for part in range(3):
        c0 = part * QK_W
        for c in range(tile // rq):
            acc = jnp.zeros((rq, QK_W), F32)
            for k in range(DN_SHORT_K):
                acc = acc + (qbuf[c * rq + offq + k:c * rq + offq + k + rq, c0:c0 + QK_W]
                             * dnw_ref[k:k + 1, c0:c0 + QK_W])
            a = acc * _sigmoid(acc)
            if part < 2:
                scale = DN_DK ** -0.5 if part == 0 else 1.0
                hs = []
                for h in range(DN_HEADS):
                    ah = a[:, h * DN_DK:(h + 1) * DN_DK]
                    nrm = lax.rsqrt(jnp.sum(ah * ah, axis=-1, keepdims=True) + EPS)
                    hs.append(ah * (nrm * scale))
                a = jnp.concatenate(hs, axis=-1)
            outs[part][c * rq:(c + 1) * rq, :] = a

    p = ba_ref[...]
    beta = _sigmoid(p)
    xg = p + gvec_ref[1:2, :]
    softplus = jnp.maximum(xg, 0.0) + jnp.log1p(jnp.exp(-jnp.abs(xg)))
    g = -jnp.exp(gvec_ref[0:1, :]) * softplus
    lane = lax.broadcasted_iota(jnp.int32, p.shape, 1)
    gates_ref[...] = jnp.where(lane < 2 * DN_HEADS, beta, g)

    pg = gm_ref[...]
    ge = pg * (0.5 * (1.0 + jnp.tanh(0.7978845608028654 * (pg + 0.044715 * (pg * pg * pg)))))
    u = ge[:, :GMLP_W]
    vn = _group_norm(ge[:, GMLP_W:], gavg) * gmp_ref[0:1, :] + gmp_ref[1:2, :]
    grp = lax.broadcasted_iota(jnp.int32, (GMLP_CHUNK, GMLP_W), 1) // GROUP_W
    for n in range(tile // GMLP_CHUNK):
        vchunk = vn[n * GMLP_CHUNK:(n + 1) * GMLP_CHUNK, :]
        sg = bsf_ref[...]
        for gi in range(GMLP_GROUPS):
            r = jnp.dot(ws_ref[gi], vchunk, preferred_element_type=F32)
            sg = sg + jnp.where(grp == gi, r, 0.0)
        yc_ref[n * GMLP_CHUNK:(n + 1) * GMLP_CHUNK, :] = u[n * GMLP_CHUNK:(n + 1) * GMLP_CHUNK, :] * sg


def _local_mixers(proj, lp, n_ctx_tiles, ctx_tps, den_tps):
    nt = proj.shape[0]
    tile = MIX_TILE
    n_tiles = nt // tile
    cpb = tile // CONV_HALO
    qpb = tile // DN_HALO
    n_cblk = nt // CONV_HALO
    n_qblk = nt // DN_HALO
    col = lambda c, w: c // w
    full = lambda a: pl.BlockSpec(a.shape, lambda i: (0,) * a.ndim)
    in_specs = [
        pl.BlockSpec((tile, 3 * QK_W), lambda i: (i, col(COL_QKV, 3 * QK_W))),
        pl.BlockSpec((DN_HALO, 3 * QK_W), lambda i: (jnp.maximum(i * qpb - 1, 0), 0)),
        pl.BlockSpec((DN_HALO, 3 * QK_W), lambda i: (jnp.minimum((i + 1) * qpb, n_qblk - 1), 0)),
        pl.BlockSpec((tile, 2 * CONV_W), lambda i: (i, col(COL_CONV, 2 * CONV_W))),
        pl.BlockSpec((CONV_HALO, 2 * CONV_W), lambda i: (jnp.maximum(i * cpb - 1, 0), col(COL_CONV, 2 * CONV_W))),
        pl.BlockSpec((CONV_HALO, 2 * CONV_W),
                     lambda i: (jnp.minimum((i + 1) * cpb, n_cblk - 1), col(COL_CONV, 2 * CONV_W))),
        pl.BlockSpec((tile, 2 * GMLP_W), lambda i: (i, col(COL_GM, 2 * GMLP_W))),
        pl.BlockSpec((tile, LANES), lambda i: (i, col(COL_BA, LANES))),
        full(lp["conv_dw"]), full(lp["conv_p"]), full(lp["dn_conv"]), full(lp["gvec"]),
        full(lp["gm_p"]), full(lp["gm_ws"]), full(lp["gm_bsf"]), full(lp["gavg"]),
    ]
    tok = lambda w: pl.BlockSpec((tile, w), lambda i: (i, 0))
    shp = lambda w: jax.ShapeDtypeStruct((nt, w), F32)
    return pl.pallas_call(
        functools.partial(_mixpre_kernel, tile=tile, n_ctx_tiles=n_ctx_tiles, ctx_tps=ctx_tps, den_tps=den_tps),
        grid=(n_tiles,),
        in_specs=in_specs,
        out_specs=[tok(CONV_W), tok(GMLP_W), tok(QK_W), tok(QK_W), tok(DN_W), tok(LANES)],
        out_shape=[shp(CONV_W), shp(GMLP_W), shp(QK_W), shp(QK_W), shp(DN_W), shp(LANES)],
        scratch_shapes=[pltpu.VMEM((tile + 2 * CONV_HALO, CONV_W), F32),
                        pltpu.VMEM((tile + 2 * DN_HALO, 3 * QK_W), F32)],
        compiler_params=pltpu.CompilerParams(vmem_limit_bytes=VMEM_LIMIT),
        name="local_mixers",
    )(proj, proj, proj, proj, proj, proj, proj, proj,
      lp["conv_dw"], lp["conv_p"], lp["dn_conv"], lp["gvec"], lp["gm_p"], lp["gm_ws"], lp["gm_bsf"], lp["gavg"])


def _tri_inverse(a, ii, jj, eye):
    hdot = functools.partial(jnp.dot, precision=HIGHEST, preferred_element_type=F32)
    same = lambda s: (ii >> s) == (jj >> s)
    m8, m16, m32 = same(3), same(4), same(5)
    d = jnp.where(m8, a, 0.0)
    d2 = hdot(d, d)
    d3 = hdot(d, d2)
    d4 = hdot(d2, d2)
    p = eye - d + d2 - d3
    x = p + hdot(p, d4)
    for outer, inner in ((m16, m8), (m32, m16), (None, m32)):
        sel = ~inner if outer is None else (outer & ~inner)
        l = jnp.where(sel, a, 0.0)
        x = x - hdot(hdot(x, l), x)
    return x


def _deltanet_kernel(*refs, tile, zero_init, emit_state):
    if zero_init:
        (qf_ref, kf_ref, vf_ref, gf_ref, qb_ref, kb_ref, vb_ref, gb_ref), rest = refs[:8], refs[8:]
        s0_ref = None
    else:
        (qf_ref, kf_ref, vf_ref, gf_ref, qb_ref, kb_ref, vb_ref, gb_ref, s0_ref), rest = refs[:9], refs[9:]
    if emit_state:
        of_ref, ob_ref, sfin_ref, s_ref = rest
    else:
        of_ref, ob_ref, s_ref = rest
        sfin_ref = None
    t = pl.program_id(1)
    nt = pl.num_programs(1)
    c = DN_CHUNK
    n_chunks = tile // c

    @pl.when(t == 0)
    def _():
        if zero_init:
            s_ref[...] = jnp.zeros_like(s_ref)
        else:
            for d in range(2):
                for h in range(DN_HEADS):
                    s_ref[d * DN_HEADS + h] = s0_ref[0, 0, d, h]

    ii = lax.broadcasted_iota(jnp.int32, (c, c), 0)
    jj = lax.broadcasted_iota(jnp.int32, (c, c), 1)
    eye = (ii == jj).astype(F32)
    nt_dims = (((1,), (1,)), ((), ()))
    tn_dims = (((0,), (0,)), ((), ()))
    dirs = (
        (qf_ref, kf_ref, vf_ref, gf_ref, of_ref, ii >= jj, ii > jj, ii <= jj, c - 1),
        (qb_ref, kb_ref, vb_ref, gb_ref, ob_ref, ii <= jj, ii < jj, ii >= jj, 0),
    )

    def chunk_step(ci, carry):
        for d, (q_ref, k_ref, v_ref, g_ref, o_ref, incl, strict, incl_t, last_row) in enumerate(dirs):
            cidx = ci if d == 0 else n_chunks - 1 - ci
            r0 = pl.multiple_of(cidx * c, c)
            gates = g_ref[pl.ds(r0, c), :]
            cum = incl.astype(BF16)
            for h in range(DN_HEADS):
                lane_b = d * DN_HEADS + h
                lane_g = 2 * DN_HEADS + lane_b
                beta = gates[:, lane_b:lane_b + 1]
                gb = jnp.broadcast_to(gates[:, lane_g:lane_g + 1], (c, c))
                gi = _dot_exact_lhs(cum, gb)
                gj = jnp.sum(jnp.where(incl_t, gb, 0.0), axis=0, keepdims=True)
                dmat = jnp.where(incl, jnp.exp(jnp.minimum(gi - gj, 0.0)), 0.0)
                gc = gi[:, 0:1]
                gc_last = gi[last_row:last_row + 1, 0:1]
                e_gc = jnp.exp(gc)
                q = q_ref[pl.ds(r0, c), h * DN_DK:(h + 1) * DN_DK]
                k = k_ref[pl.ds(r0, c), h * DN_DK:(h + 1) * DN_DK]
                v = v_ref[pl.ds(r0, c), h * DN_DV:(h + 1) * DN_DV]
                kk = lax.dot_general(k, k, nt_dims, preferred_element_type=F32)
                a = jnp.where(strict, beta * kk * dmat, 0.0)
                tmat = _tri_inverse(a, ii, jj, eye)
                w = jnp.dot(tmat, k * (beta * e_gc), preferred_element_type=F32)
                u = jnp.dot(tmat, v * beta, preferred_element_type=F32)
                qk = lax.dot_general(q, k, nt_dims, preferred_element_type=F32) * dmat
                q_dec = q * e_gc
                k_dec = k * jnp.exp(gc_last - gc)
                s_old = s_ref[lane_b]
                v_new = u - jnp.dot(w, s_old, preferred_element_type=F32)
                o = (jnp.dot(q_dec, s_old, preferred_element_type=F32)
                     + jnp.dot(qk, v_new, preferred_element_type=F32))
                s_ref[lane_b] = (s_old * jnp.exp(gc_last)
                                 + lax.dot_general(k_dec, v_new, tn_dims, preferred_element_type=F32))
                o_ref[pl.ds(r0, c), h * DN_DV:(h + 1) * DN_DV] = o
        return carry

    lax.fori_loop(0, n_chunks, chunk_step, 0)

    if emit_state:
        @pl.when(t == nt - 1)
        def _():
            for d in range(2):
                for h in range(DN_HEADS):
                    sfin_ref[0, d, h] = s_ref[d * DN_HEADS + h]


def _deltanet(q, k, v, gates, s0, layer, row0, n_seq, seq_len, emit_state):
    tile = MIX_TILE
    tps = seq_len // tile
    t0 = row0 // tile
    zero_init = s0 is None
    fwd = lambda w: pl.BlockSpec((tile, w), lambda b, t: (t0 + b * tps + t, 0))
    bwd = lambda w: pl.BlockSpec((tile, w), lambda b, t: (t0 + b * tps + (tps - 1 - t), 0))
    in_specs = [fwd(QK_W), fwd(QK_W), fwd(DN_W), fwd(LANES), bwd(QK_W), bwd(QK_W), bwd(DN_W), bwd(LANES)]
    args = [q, k, v, gates, q, k, v, gates]
    if not zero_init:
        in_specs.append(pl.BlockSpec((1, 1, 2, DN_HEADS, DN_DK, DN_DV), lambda b, t: (b, layer, 0, 0, 0, 0)))
        args.append(s0)
    out_specs = [pl.BlockSpec((tile, DN_W), lambda b, t: (b * tps + t, 0)),
                 pl.BlockSpec((tile, DN_W), lambda b, t: (b * tps + (tps - 1 - t), 0))]
    out_shape = [jax.ShapeDtypeStruct((n_seq * seq_len, DN_W), F32)] * 2
    if emit_state:
        out_specs.append(pl.BlockSpec((1, 2, DN_HEADS, DN_DK, DN_DV), lambda b, t: (b, 0, 0, 0, 0)))
        out_shape.append(jax.ShapeDtypeStruct((n_seq, 2, DN_HEADS, DN_DK, DN_DV), F32))
    return pl.pallas_call(
        functools.partial(_deltanet_kernel, tile=tile, zero_init=zero_init, emit_state=emit_state),
        grid=(n_seq, tps),
        in_specs=in_specs,
        out_specs=out_specs,
        out_shape=out_shape,
        scratch_shapes=[pltpu.VMEM((2 * DN_HEADS, DN_DK, DN_DV), F32)],
        compiler_params=pltpu.CompilerParams(dimension_semantics=("arbitrary", "arbitrary"),
                                             vmem_limit_bytes=VMEM_LIMIT),
        name="deltanet_scan",
    )(*args)


def _outproj_kernel(ya_ref, of_ref, ob_ref, z_ref, yc_ref, x_ref, mod_ref, wout_ref, vec_ref, dng_ref,
                    wr_hi_ref, wr_lo_ref, br_ref, x1_ref, h_ref, lg_ref, *, tile, n_ctx, dec_seq):
    m = _mod_row(mod_ref, pl.program_id(0) * tile, n_ctx, dec_seq)
    gt1 = m[:, 2 * D_MODEL:3 * D_MODEL]
    sh2 = m[:, 3 * D_MODEL:4 * D_MODEL]
    sc2 = m[:, 4 * D_MODEL:5 * D_MODEL]
    o = of_ref[...] + ob_ref[...]
    z = z_ref[...]
    zz = z * _sigmoid(z)
    hs = []
    for h in range(DN_HEADS):
        oh = o[:, h * DN_DV:(h + 1) * DN_DV]
        on = oh * lax.rsqrt(jnp.mean(oh * oh, axis=-1, keepdims=True) + EPS) * dng_ref[...]
        hs.append(on * zz[:, h * DN_DV:(h + 1) * DN_DV])
    ycat = jnp.concatenate([ya_ref[...]] + hs + [yc_ref[...]], axis=-1).astype(BF16)
    y = jnp.dot(ycat, wout_ref[...], preferred_element_type=F32)
    x1 = _ln(DEEPNORM_ALPHA * x_ref[...] + gt1 * y) * vec_ref[0:1, :] + vec_ref[1:2, :]
    x1_ref[...] = x1
    hf = _ln(x1) * (1.0 + sc2) + sh2
    h_hi = hf.astype(BF16)
    h_lo = (hf - h_hi.astype(F32)).astype(BF16)
    h_ref[...] = h_hi
    lg_ref[...] = (jnp.dot(h_hi, wr_hi_ref[...], preferred_element_type=F32)
                   + jnp.dot(h_lo, wr_hi_ref[...], preferred_element_type=F32)
                   + jnp.dot(h_hi, wr_lo_ref[...], preferred_element_type=F32)
                   + br_ref[...])


def _out_projection(ya, o_f, o_b, proj, yc, x, mod, lp, n_ctx, dec_seq):
    nt, d = x.shape
    tile = TOK_TILE
    tok = lambda w: pl.BlockSpec((tile, w), lambda i: (i, 0))
    full = lambda a: pl.BlockSpec(a.shape, lambda i: (0,) * a.ndim)
    return pl.pallas_call(
        functools.partial(_outproj_kernel, tile=tile, n_ctx=n_ctx, dec_seq=dec_seq),
        grid=(nt // tile,),
        in_specs=[tok(CONV_W), tok(DN_W), tok(DN_W),
                  pl.BlockSpec((tile, DN_W), lambda i: (i, COL_Z // DN_W)),
                  tok(GMLP_W), tok(d), full(mod), full(lp["w_out"]), full(lp["ln1"]), full(lp["dn_norm_g"]),
                  full(lp["wr_hi"]), full(lp["wr_lo"]), full(lp["b_router"])],
        out_specs=[tok(d), tok(d), tok(LANES)],
        out_shape=[jax.ShapeDtypeStruct((nt, d), F32), jax.ShapeDtypeStruct((nt, d), BF16),
                   jax.ShapeDtypeStruct((nt, LANES), F32)],
        compiler_params=pltpu.CompilerParams(vmem_limit_bytes=VMEM_LIMIT),
        name="out_projection_router",
    )(ya, o_f, o_b, proj, yc, x, mod, lp["w_out"], lp["ln1"], lp["dn_norm_g"],
      lp["wr_hi"], lp["wr_lo"], lp["b_router"])


def _expert_kernel(be_ref, nu_ref, x_ref, wgu_ref, bgu_ref, wd_ref, bd_ref, gate_ref, out_ref):
    i = pl.program_id(0)

    @pl.when(i < nu_ref[0])
    def _():
        gu = jnp.dot(x_ref[...], wgu_ref[0, 0], preferred_element_type=F32) + bgu_ref[0, 0]
        gate = jnp.minimum(gu[:, :D_FF], SWIGLU_LIMIT)
        up = jnp.clip(gu[:, D_FF:], -SWIGLU_LIMIT, SWIGLU_LIMIT)
        act = gate * _sigmoid(SWIGLU_ALPHA * gate)
        hmid = ((up + 1.0) * act).astype(BF16)
        y = jnp.dot(hmid, wd_ref[0, 0], preferred_element_type=F32) + bd_ref[0, 0]
        out_ref[...] = y * gate_ref[...]

    @pl.when(i >= nu_ref[0])
    def _():
        out_ref[...] = jnp.zeros_like(out_ref)


def _experts(xg, slot_gate, block_e, n_used, w_gu, b_gu, w_down, b_down, layer):
    m_pad, d = xg.shape
    bm = MOE_BM
    grid_spec = pltpu.PrefetchScalarGridSpec(
        num_scalar_prefetch=2,
        grid=(m_pad // bm,),
        in_specs=[
            pl.BlockSpec((bm, d), lambda i, be, nu: (i, 0)),
            pl.BlockSpec((1, 1, d, 2 * D_FF), lambda i, be, nu: (layer, be[i], 0, 0)),
            pl.BlockSpec((1, 1, 1, 2 * D_FF), lambda i, be, nu: (layer, be[i], 0, 0)),
            pl.BlockSpec((1, 1, D_FF, d), lambda i, be, nu: (layer, be[i], 0, 0)),
            pl.BlockSpec((1, 1, 1, d), lambda i, be, nu: (layer, be[i], 0, 0)),
            pl.BlockSpec((bm, 1), lambda i, be, nu: (i, 0)),
        ],
        out_specs=pl.BlockSpec((bm, d), lambda i, be, nu: (i, 0)),
    )
    return pl.pallas_call(
        _expert_kernel,
        grid_spec=grid_spec,
        out_shape=jax.ShapeDtypeStruct((m_pad, d), F32),
        compiler_params=pltpu.CompilerParams(vmem_limit_bytes=VMEM_LIMIT),
        name="moe_experts",
    )(block_e, n_used, xg, w_gu, b_gu, w_down, b_down, slot_gate)


def _combine_kernel(yg_ref, x1_ref, mod_ref, vec_ref, out_ref, *, tile, n_ctx, dec_seq):
    m = _mod_row(mod_ref, pl.program_id(0) * tile, n_ctx, dec_seq)
    gt2 = m[:, 5 * D_MODEL:6 * D_MODEL]
    y = (yg_ref[0] + yg_ref[1]) + (yg_ref[2] + yg_ref[3])
    out_ref[...] = _ln(DEEPNORM_ALPHA * x1_ref[...] + gt2 * y) * vec_ref[0:1, :] + vec_ref[1:2, :]


def _combine(yg, x1, mod, ln2, n_ctx, dec_seq):
    nt, d = x1.shape
    tile = TOK_TILE
    return pl.pallas_call(
        functools.partial(_combine_kernel, tile=tile, n_ctx=n_ctx, dec_seq=dec_seq),
        grid=(nt // tile,),
        in_specs=[pl.BlockSpec((TOP_K, tile, d), lambda i: (0, i, 0)),
                  pl.BlockSpec((tile, d), lambda i: (i, 0)),
                  pl.BlockSpec(mod.shape, lambda i: (0, 0)),
                  pl.BlockSpec(ln2.shape, lambda i: (0, 0))],
        out_specs=pl.BlockSpec((tile, d), lambda i: (i, 0)),
        out_shape=jax.ShapeDtypeStruct((nt, d), F32),
        compiler_params=pltpu.CompilerParams(vmem_limit_bytes=VMEM_LIMIT),
        name="moe_combine_ln",
    )(yg, x1, mod, ln2)


def _route(logits):
    nt = logits.shape[0]
    nk = nt * TOP_K
    bm = MOE_BM
    top_val, top_idx = lax.top_k(logits[:, :N_EXPERTS], TOP_K)
    gates = jax.nn.softmax(top_val, axis=-1)
    flat_e = top_idx.reshape(-1)
    order = jnp.argsort(flat_e)
    sorted_e = flat_e[order]
    counts = jnp.bincount(flat_e, length=N_EXPERTS)
    padded = (counts + bm - 1) // bm * bm
    pad_end = jnp.cumsum(padded)
    pad_start = pad_end - padded
    grp_start = jnp.cumsum(counts) - counts
    dest = (pad_start[sorted_e] + jnp.arange(nk) - grp_start[sorted_e]).astype(jnp.int32)
    n_blocks = nk // bm + N_EXPERTS
    m_pad = n_blocks * bm
    slot_tok = jnp.zeros((m_pad,), jnp.int32).at[dest].set((order // TOP_K).astype(jnp.int32))
    slot_gate = jnp.zeros((m_pad,), F32).at[dest].set(gates.reshape(-1)[order])
    pair_slot = jnp.zeros((nk,), jnp.int32).at[order].set(dest)
    block_e = jnp.minimum(jnp.searchsorted(pad_end, jnp.arange(n_blocks) * bm, side="right"),
                          N_EXPERTS - 1).astype(jnp.int32)
    n_used = (pad_end[-1] // bm).astype(jnp.int32).reshape(1)
    return slot_tok, slot_gate.reshape(m_pad, 1), pair_slot.reshape(nt, TOP_K), block_e, n_used


def _grid_pos_embed(t, d):
    rows = t // GRID_W
    r, col = jnp.meshgrid(jnp.arange(rows), jnp.arange(GRID_W), indexing="ij")
    r = r.reshape(-1).astype(F32)[:, None]
    col = col.reshape(-1).astype(F32)[:, None]
    n_freq = d // 4
    omega = 1.0 / (POS_BASE ** (jnp.arange(n_freq, dtype=F32) / n_freq))
    return jnp.concatenate([jnp.sin(r * omega), jnp.cos(r * omega),
                            jnp.sin(col * omega), jnp.cos(col * omega)], axis=-1)


def _pad_lanes(a, offset):
    return jnp.zeros((1, LANES), F32).at[0, offset:offset + a.shape[0]].set(a.astype(F32))


def _layer_params(l, w_in, conv_dw, conv_b, conv_ln_g, conv_ln_b, dn_conv, dn_a_log, dn_dt_bias, dn_norm_g,
                  gm_ln_g, gm_ln_b, gm_ws, gm_bs, w_out, ln1_g, ln1_b, ln2_g, ln2_b, w_router, b_router):
    wi = w_in[l]
    c_conv, c_qkv, c_z, c_ba = 0, 2 * CONV_W, 2 * CONV_W + 2 * QK_W + DN_W, 2 * CONV_W + 2 * QK_W + 2 * DN_W
    c_gm = c_ba + 4 * DN_HEADS
    w_in_r = jnp.concatenate([
        wi[:, c_qkv:c_z], wi[:, c_conv:c_qkv], wi[:, c_z:c_ba], wi[:, c_gm:],
        wi[:, c_ba:c_gm], jnp.zeros((D_MODEL, LANES - 4 * DN_HEADS), F32)], axis=1).astype(BF16)
    grp = jnp.arange(CONV_W) // GROUP_W
    gavg = (grp[:, None] == grp[None, :]).astype(BF16) * (1.0 / GROUP_W)
    gvec = jnp.concatenate([_pad_lanes(dn_a_log[l].reshape(-1), 2 * DN_HEADS),
                            _pad_lanes(dn_dt_bias[l].reshape(-1), 2 * DN_HEADS)], axis=0)
    wr = jnp.zeros((D_MODEL, LANES), F32).at[:, :N_EXPERTS].set(w_router[l])
    wr_hi = wr.astype(BF16)
    wr_lo = (wr - wr_hi.astype(F32)).astype(BF16)
    return {
        "w_in_r": w_in_r,
        "conv_dw": conv_dw[l],
        "conv_p": jnp.stack([conv_b[l], conv_ln_g[l], conv_ln_b[l]]),
        "dn_conv": dn_conv[l],
        "gvec": gvec,
        "gm_p": jnp.stack([gm_ln_g[l], gm_ln_b[l]]),
        "gm_ws": gm_ws[l],
        "gm_bsf": jnp.repeat(jnp.transpose(gm_bs[l]), GROUP_W, axis=1),
        "gavg": gavg,
        "w_out": w_out[l].astype(BF16),
        "ln1": jnp.stack([ln1_g[l], ln1_b[l]]),
        "ln2": jnp.stack([ln2_g[l], ln2_b[l]]),
        "dn_norm_g": dn_norm_g[l].reshape(1, DN_DV),
        "wr_hi": wr_hi,
        "wr_lo": wr_lo,
        "b_router": _pad_lanes(b_router[l], 0),
    }


def kernel(x_prompt, x_sample, state_delta, c, c_ctx, w_ada, b_ada, w_in, conv_dw, conv_b, conv_ln_g, conv_ln_b, dn_conv, dn_a_log, dn_dt_bias, dn_norm_g, gm_ln_g, gm_ln_b, gm_ws, gm_bs, w_out, ln1_g, ln1_b, ln2_g, ln2_b, w_router, b_router, w_gu, b_gu, w_down, b_down):
    batch, seq, d = x_prompt.shape
    dec_batch, dec_seq, _ = x_sample.shape
    n_ctx = batch * seq
    n_den = dec_batch * dec_seq
    depth = w_in.shape[0]

    cond = jnp.zeros((16, d), F32).at[0].set(c_ctx).at[1:1 + dec_batch].set(c)
    mod_all = _modulation(cond, w_ada, b_ada)
    x = _assemble_tokens(x_prompt.reshape(n_ctx, d), x_sample.reshape(n_den, d),
                         _grid_pos_embed(dec_seq, d), TOK_TILE)

    w_gu_h = w_gu.astype(BF16)
    w_down_h = w_down.astype(BF16)
    b_gu_r = b_gu.reshape(depth, N_EXPERTS, 1, 2 * D_FF)
    b_down_r = b_down.reshape(depth, N_EXPERTS, 1, d)

    ctx_states = []
    for l in range(depth):
        lp = _layer_params(l, w_in, conv_dw, conv_b, conv_ln_g, conv_ln_b, dn_conv, dn_a_log, dn_dt_bias,
                           dn_norm_g, gm_ln_g, gm_ln_b, gm_ws, gm_bs, w_out, ln1_g, ln1_b, ln2_g, ln2_b,
                           w_router, b_router)
        mod = mod_all[l]
        proj = _in_projection(x, mod, lp["w_in_r"], n_ctx, dec_seq)
        ya, yc, q, k, v, gates = _local_mixers(proj, lp, n_ctx // MIX_TILE, seq // MIX_TILE, dec_seq // MIX_TILE)
        of_c, ob_c, s_fin = _deltanet(q, k, v, gates, None, l, 0, batch, seq, True)
        of_d, ob_d = _deltanet(q, k, v, gates, state_delta, l, n_ctx, dec_batch, dec_seq, False)
        ctx_states.append(s_fin)
        o_f = jnp.concatenate([of_c, of_d], axis=0)
        o_b = jnp.concatenate([ob_c, ob_d], axis=0)
        x1, hffn, logits = _out_projection(ya, o_f, o_b, proj, yc, x, mod, lp, n_ctx, dec_seq)
        slot_tok, slot_gate, pair_slot, block_e, n_used = _route(logits)
        xg = jnp.take(hffn, slot_tok, axis=0)
        yb = _experts(xg, slot_gate, block_e, n_used, w_gu_h, b_gu_r, w_down_h, b_down_r, l)
        yg = jnp.take(yb, jnp.transpose(pair_slot), axis=0)
        x = _combine(yg, x1, mod, lp["ln2"], n_ctx, dec_seq)

    new_state = jnp.stack(ctx_states, axis=1).astype(x_prompt.dtype)
    return (x[:n_ctx].reshape(batch, seq, d), x[n_ctx:].reshape(dec_batch, dec_seq, d), new_state)
```

```python
import functools

import jax
import jax.numpy as jnp
from jax import lax
from jax.experimental import pallas as pl
from jax.experimental.pallas import tpu as pltpu

F32 = jnp.float32
BF16 = jnp.bfloat16
HIGHEST = lax.Precision.HIGHEST

D_MODEL = 1024
DEPTH = 2
GRID_W = 64
POS_BASE = 10000.0
CONV_W = 256
CONV_K = 31
CONV_HALO = 16
DN_HEADS = 4
DN_DK = 128
DN_DV = 128
QK_W = DN_HEADS * DN_DK
DN_W = DN_HEADS * DN_DV
DN_SHORT_K = 5
DN_HALO = 8
DN_CHUNK = 64
GMLP_W = 256
GMLP_GROUPS = 4
GMLP_CHUNK = 128
GROUP_W = 64
N_EXPERTS = 32
TOP_K = 4
D_FF = D_MODEL
SWIGLU_ALPHA = 1.702
SWIGLU_LIMIT = 7.0
DEEPNORM_ALPHA = (2 * DEPTH) ** 0.25
EPS = 1e-6

LANES = 128
COL_QKV = 0
COL_CONV = 2 * QK_W + DN_W
COL_Z = COL_CONV + 2 * CONV_W
COL_GM = COL_Z + DN_W
COL_BA = COL_GM + 2 * GMLP_W
PROJ_W = COL_BA + LANES

MIX_TILE = 256
TOK_TILE = 512
MOE_BM = 512
VMEM_LIMIT = 56 * 1024 * 1024


def _ln(x):
    mu = jnp.mean(x, axis=-1, keepdims=True)
    xc = x - mu
    return xc * lax.rsqrt(jnp.mean(xc * xc, axis=-1, keepdims=True) + EPS)


def _sigmoid(x):
    return jax.nn.sigmoid(x)


def _split_bf16(x, parts):
    out = []
    r = x
    for _ in range(parts):
        p = r.astype(BF16)
        out.append(p)
        r = r - p.astype(F32)
    return out


def _dot_exact_rhs(x, m_bf16, parts=3):
    acc = None
    for p in _split_bf16(x, parts):
        t = jnp.dot(p, m_bf16, preferred_element_type=F32)
        acc = t if acc is None else acc + t
    return acc


def _dot_exact_lhs(m_bf16, x, parts=3):
    acc = None
    for p in _split_bf16(x, parts):
        t = jnp.dot(m_bf16, p, preferred_element_type=F32)
        acc = t if acc is None else acc + t
    return acc


def _mod_row(mod_ref, start, n_ctx, dec_seq):
    row = jnp.where(start < n_ctx, 0, 1 + (start - n_ctx) // dec_seq)
    return mod_ref[pl.ds(row, 1), :]


def _mod_kernel(cond_ref, w_ref, b_ref, out_ref):
    c = cond_ref[...]
    s = c * _sigmoid(c)
    out_ref[0] = jnp.dot(s, w_ref[0], preferred_element_type=F32) + b_ref[0]


def _modulation(cond, w_ada, b_ada):
    nl, d, n = w_ada.shape
    r = cond.shape[0]
    tn = 1024
    return pl.pallas_call(
        _mod_kernel,
        grid=(nl, n // tn),
        in_specs=[
            pl.BlockSpec((r, d), lambda l, j: (0, 0)),
            pl.BlockSpec((1, d, tn), lambda l, j: (l, 0, j)),
            pl.BlockSpec((1, 1, tn), lambda l, j: (l, 0, j)),
        ],
        out_specs=pl.BlockSpec((1, r, tn), lambda l, j: (l, 0, j)),
        out_shape=jax.ShapeDtypeStruct((nl, r, n), F32),
        name="adaln_modulation",
    )(cond, w_ada, b_ada.reshape(nl, 1, n))


def _assemble_kernel(xp_ref, xs_ref, pos_ref, out_ref, *, n_ctx_tiles):
    i = pl.program_id(0)

    @pl.when(i < n_ctx_tiles)
    def _():
        out_ref[...] = xp_ref[...]

    @pl.when(i >= n_ctx_tiles)
    def _():
        out_ref[...] = xs_ref[...] + pos_ref[...]


def _assemble_tokens(xp_flat, xs_flat, pos, tile):
    n_ctx, d = xp_flat.shape
    n_den = xs_flat.shape[0]
    ct = n_ctx // tile
    pt = pos.shape[0] // tile
    return pl.pallas_call(
        functools.partial(_assemble_kernel, n_ctx_tiles=ct),
        grid=((n_ctx + n_den) // tile,),
        in_specs=[
            pl.BlockSpec((tile, d), lambda i: (jnp.minimum(i, ct - 1), 0)),
            pl.BlockSpec((tile, d), lambda i: (jnp.maximum(i - ct, 0), 0)),
            pl.BlockSpec((tile, d), lambda i: (jnp.maximum(i - ct, 0) % pt, 0)),
        ],
        out_specs=pl.BlockSpec((tile, d), lambda i: (i, 0)),
        out_shape=jax.ShapeDtypeStruct((n_ctx + n_den, d), F32),
        name="assemble_tokens",
    )(xp_flat, xs_flat, pos)


def _inproj_kernel(x_ref, mod_ref, w_ref, out_ref, *, tile, n_ctx, dec_seq):
    m = _mod_row(mod_ref, pl.program_id(0) * tile, n_ctx, dec_seq)
    sh1 = m[:, 0:D_MODEL]
    sc1 = m[:, D_MODEL:2 * D_MODEL]
    h = _ln(x_ref[...]) * (1.0 + sc1) + sh1
    out_ref[...] = jnp.dot(h.astype(BF16), w_ref[...], preferred_element_type=F32)


def _in_projection(x, mod, w_in_r, n_ctx, dec_seq):
    nt, d = x.shape
    tile = TOK_TILE
    return pl.pallas_call(
        functools.partial(_inproj_kernel, tile=tile, n_ctx=n_ctx, dec_seq=dec_seq),
        grid=(nt // tile,),
        in_specs=[
            pl.BlockSpec((tile, d), lambda i: (i, 0)),
            pl.BlockSpec(mod.shape, lambda i: (0, 0)),
            pl.BlockSpec(w_in_r.shape, lambda i: (0, 0)),
        ],
        out_specs=pl.BlockSpec((tile, PROJ_W), lambda i: (i, 0)),
        out_shape=jax.ShapeDtypeStruct((nt, PROJ_W), F32),
        compiler_params=pltpu.CompilerParams(vmem_limit_bytes=VMEM_LIMIT),
        name="in_projection",
    )(x, mod, w_in_r)


def _group_norm(x, gavg):
    mean = _dot_exact_rhs(x, gavg)
    xc = x - mean
    var = _dot_exact_rhs(xc * xc, gavg)
    return xc * lax.rsqrt(var + EPS)


def _mixpre_kernel(qkv_ref, qkv_p_ref, qkv_n_ref, cv_ref, cv_p_ref, cv_n_ref, gm_ref, ba_ref,
                   convw_ref, convp_ref, dnw_ref, gvec_ref, gmp_ref, ws_ref, bsf_ref, gavg_ref,
                   ya_ref, yc_ref, q_ref, k_ref, v_ref, gates_ref, cbuf, qbuf,
                   *, tile, n_ctx_tiles, ctx_tps, den_tps):
    i = pl.program_id(0)
    pos = jnp.where(i < n_ctx_tiles, i % ctx_tps, (i - n_ctx_tiles) % den_tps)
    tps = jnp.where(i < n_ctx_tiles, ctx_tps, den_tps)
    first = pos == 0
    last = pos == tps - 1
    gavg = gavg_ref[...]

    def glu(p):
        return p[:, :CONV_W] * _sigmoid(p[:, CONV_W:])

    cbuf[0:CONV_HALO, :] = jnp.where(first, 0.0, glu(cv_p_ref[...]))
    cbuf[CONV_HALO:CONV_HALO + tile, :] = glu(cv_ref[...])
    cbuf[CONV_HALO + tile:2 * CONV_HALO + tile, :] = jnp.where(last, 0.0, glu(cv_n_ref[...]))
    conv_b = convp_ref[0:1, :]
    conv_g = convp_ref[1:2, :]
    conv_beta = convp_ref[2:3, :]
    rc = 64
    off = CONV_HALO - CONV_K // 2
    for c in range(tile // rc):
        acc = jnp.zeros((rc, CONV_W), F32)
        for k in range(CONV_K):
            acc = acc + cbuf[c * rc + off + k:c * rc + off + k + rc, :] * convw_ref[k:k + 1, :]
        y = _group_norm(acc + conv_b, gavg) * conv_g + conv_beta
        ya_ref[c * rc:(c + 1) * rc, :] = y * _sigmoid(y)

    qbuf[0:DN_HALO, :] = jnp.where(first, 0.0, qkv_p_ref[...])
    qbuf[DN_HALO:DN_HALO + tile, :] = qkv_ref[...]
    qbuf[DN_HALO + tile:2 * DN_HALO + tile, :] = jnp.where(last, 0.0, qkv_n_ref[...])
    rq = 32
    offq = DN_HALO - DN_SHORT_K // 2
    outs = (q_ref, k_ref, v_ref)
    for part in range(3):
        c0 = part * QK_W
        for c in range(tile // rq):
            acc = jnp.zeros((rq, QK_W), F32)
            for k in range(DN_SHORT_K):
                acc = acc + (qbuf[c * rq + offq + k:c * rq + offq + k + rq, c0:c0 + QK_W]
                             * dnw_ref[k:k + 1, c0:c0 + QK_W])
            a = acc * _sigmoid(acc)
            if part < 2:
                scale = DN_DK ** -0.5 if part == 0 else 1.0
                hs = []
                for h in range(DN_HEADS):
                    ah = a[:, h * DN_DK:(h + 1) * DN_DK]
                    nrm = lax.rsqrt(jnp.sum(ah * ah, axis=-1, keepdims=True) + EPS)
                    hs.append(ah * (nrm * scale))
                a = jnp.concatenate(hs, axis=-1)
            outs[part][c * rq:(c + 1) * rq, :] = a

    p = ba_ref[...]
    beta = _sigmoid(p)
    xg = p + gvec_ref[1:2, :]
    softplus = jnp.maximum(xg, 0.0) + jnp.log1p(jnp.exp(-jnp.abs(xg)))
    g = -jnp.exp(gvec_ref[0:1, :]) * softplus
    lane = lax.broadcasted_iota(jnp.int32, p.shape, 1)
    gates_ref[...] = jnp.where(lane < 2 * DN_HEADS, beta, g)

    pg = gm_ref[...]
    ge = pg * (0.5 * (1.0 + jnp.tanh(0.7978845608028654 * (pg + 0.044715 * (pg * pg * pg)))))
    u = ge[:, :GMLP_W]
    vn = _group_norm(ge[:, GMLP_W:], gavg) * gmp_ref[0:1, :] + gmp_ref[1:2, :]
    grp = lax.broadcasted_iota(jnp.int32, (GMLP_CHUNK, GMLP_W), 1) // GROUP_W
    for n in range(tile // GMLP_CHUNK):
        vchunk = vn[n * GMLP_CHUNK:(n + 1) * GMLP_CHUNK, :]
        sg = bsf_ref[...]
        for gi in range(GMLP_GROUPS):
            r = jnp.dot(ws_ref[gi], vchunk, preferred_element_type=F32)
            sg = sg + jnp.where(grp == gi, r, 0.0)
        yc_ref[n * GMLP_CHUNK:(n + 1) * GMLP_CHUNK, :] = u[n * GMLP_CHUNK:(n + 1) * GMLP_CHUNK, :] * sg


def _local_mixers(proj, lp, n_ctx_tiles, ctx_tps, den_tps):
    nt = proj.shape[0]
    tile = MIX_TILE
    n_tiles = nt // tile
    cpb = tile // CONV_HALO
    qpb = tile // DN_HALO
    n_cblk = nt // CONV_HALO
    n_qblk = nt // DN_HALO
    col = lambda c, w: c // w
    full = lambda a: pl.BlockSpec(a.shape, lambda i: (0,) * a.ndim)
    in_specs = [
        pl.BlockSpec((tile, 3 * QK_W), lambda i: (i, col(COL_QKV, 3 * QK_W))),
        pl.BlockSpec((DN_HALO, 3 * QK_W), lambda i: (jnp.maximum(i * qpb - 1, 0), 0)),
        pl.BlockSpec((DN_HALO, 3 * QK_W), lambda i: (jnp.minimum((i + 1) * qpb, n_qblk - 1), 0)),
        pl.BlockSpec((tile, 2 * CONV_W), lambda i: (i, col(COL_CONV, 2 * CONV_W))),
        pl.BlockSpec((CONV_HALO, 2 * CONV_W), lambda i: (jnp.maximum(i * cpb - 1, 0), col(COL_CONV, 2 * CONV_W))),
        pl.BlockSpec((CONV_HALO, 2 * CONV_W),
                     lambda i: (jnp.minimum((i + 1) * cpb, n_cblk - 1), col(COL_CONV, 2 * CONV_W))),
        pl.BlockSpec((tile, 2 * GMLP_W), lambda i: (i, col(COL_GM, 2 * GMLP_W))),
        pl.BlockSpec((tile, LANES), lambda i: (i, col(COL_BA, LANES))),
        full(lp["conv_dw"]), full(lp["conv_p"]), full(lp["dn_conv"]), full(lp["gvec"]),
        full(lp["gm_p"]), full(lp["gm_ws"]), full(lp["gm_bsf"]), full(lp["gavg"]),
    ]
    tok = lambda w: pl.BlockSpec((tile, w), lambda i: (i, 0))
    shp = lambda w: jax.ShapeDtypeStruct((nt, w), F32)
    return pl.pallas_call(
        functools.partial(_mixpre_kernel, tile=tile, n_ctx_tiles=n_ctx_tiles, ctx_tps=ctx_tps, den_tps=den_tps),
        grid=(n_tiles,),
        in_specs=in_specs,
        out_specs=[tok(CONV_W), tok(GMLP_W), tok(QK_W), tok(QK_W), tok(DN_W), tok(LANES)],
        out_shape=[shp(CONV_W), shp(GMLP_W), shp(QK_W), shp(QK_W), shp(DN_W), shp(LANES)],
        scratch_shapes=[pltpu.VMEM((tile + 2 * CONV_HALO, CONV_W), F32),
                        pltpu.VMEM((tile + 2 * DN_HALO, 3 * QK_W), F32)],
        compiler_params=pltpu.CompilerParams(vmem_limit_bytes=VMEM_LIMIT),
        name="local_mixers",
    )(proj, proj, proj, proj, proj, proj, proj, proj,
      lp["conv_dw"], lp["conv_p"], lp["dn_conv"], lp["gvec"], lp["gm_p"], lp["gm_ws"], lp["gm_bsf"], lp["gavg"])


def _deltanet_kernel(*refs, tile, zero_init, emit_state, n_aliased):
    n_in = (8 if zero_init else 9) + n_aliased
    qf_ref, kf_ref, vf_ref, gf_ref, qb_ref, kb_ref, vb_ref, gb_ref = refs[:8]
    s0_ref = None if zero_init else refs[8]
    rest = refs[n_in:]
    if emit_state:
        of_ref, ob_ref, sfin_ref = rest[:3]
        rest = rest[3:]
    else:
        of_ref, ob_ref = rest[:2]
        sfin_ref = None
        rest = rest[2:]
    s_ref, pq_scr, b_scr, o_scr, gt_scr = rest
    t = pl.program_id(1)
    nt = pl.num_programs(1)
    c = DN_CHUNK
    n_chunks = tile // c
    upd = n_chunks * DN_HEADS

    @pl.when(t == 0)
    def _():
        if zero_init:
            s_ref[...] = jnp.zeros_like(s_ref)
        else:
            for d in range(2):
                for h in range(DN_HEADS):
                    s_ref[d * DN_HEADS + h] = s0_ref[0, 0, d, h]

    ii = lax.broadcasted_iota(jnp.int32, (c, c), 0)
    jj = lax.broadcasted_iota(jnp.int32, (c, c), 1)
    same = lambda s: (ii >> s) == (jj >> s)
    m8 = same(3)
    l16 = same(4) & ~m8
    l32 = same(5) & ~same(4)
    l64 = ~same(5)
    ti = lax.broadcasted_iota(jnp.int32, (tile, tile), 0)
    tj = lax.broadcasted_iota(jnp.int32, (tile, tile), 1)
    same_chunk = (ti // c) == (tj // c)
    nt_dims = (((1,), (1,)), ((), ()))
    tn_dims = (((0,), (0,)), ((), ()))
    dirs = (
        (qf_ref, kf_ref, vf_ref, gf_ref, of_ref, ii >= jj, ii > jj, ti >= tj, c - 1),
        (qb_ref, kb_ref, vb_ref, gb_ref, ob_ref, ii <= jj, ii < jj, ti <= tj, 0),
    )

    def gate_context(d):
        g_ref, tile_tri = dirs[d][3], dirs[d][7]
        gates = g_ref[...]
        blockcum = (same_chunk & tile_tri).astype(BF16)
        gc_t = _dot_exact_lhs(blockcum, gates)
        return gates, gc_t, gc_t.T

    gate_ctx = [gate_context(0), gate_context(1)]

    b16 = lambda x: x.astype(BF16)
    mm = lambda x, y: jnp.dot(x, y, preferred_element_type=F32)

    def prep(d):
        q_ref, k_ref, v_ref, _, _, incl, strict, _, last_row = dirs[d]
        gates, gc_t, gc_tt = gate_ctx[d]
        units = [(ci, h) for ci in range(n_chunks) for h in range(DN_HEADS)]
        chunk = []
        for ci in range(n_chunks):
            r0 = ci * c
            gc_c = gc_t[r0:r0 + c, :]
            glast = gc_t[r0 + last_row:r0 + last_row + 1, :]
            chunk.append((gc_c, jnp.exp(gc_c), jnp.exp(glast - gc_c), jnp.exp(glast)))
        lane_b = lambda h: d * DN_HEADS + h
        lane_g = lambda h: 2 * DN_HEADS + d * DN_HEADS + h
        col = lambda x, l: x[:, l:l + 1]
        beta = [col(gates[ci * c:(ci + 1) * c, :], lane_b(h)) for ci, h in units]
        eg = [col(chunk[ci][1], lane_g(h)) for ci, h in units]
        dmat = [jnp.where(incl, jnp.exp(jnp.minimum(
            col(chunk[ci][0], lane_g(h)) - gc_tt[lane_g(h):lane_g(h) + 1, ci * c:(ci + 1) * c], 0.0)), 0.0)
            for ci, h in units]
        q = [q_ref[ci * c:(ci + 1) * c, h * DN_DK:(h + 1) * DN_DK] for ci, h in units]
        k = [k_ref[ci * c:(ci + 1) * c, h * DN_DK:(h + 1) * DN_DK] for ci, h in units]
        v = [v_ref[ci * c:(ci + 1) * c, h * DN_DV:(h + 1) * DN_DV] for ci, h in units]
        k16 = [b16(x) for x in k]
        n = range(len(units))
        kk = [lax.dot_general(k16[i], k16[i], nt_dims, preferred_element_type=F32) for i in n]
        qk = [lax.dot_general(b16(q[i]), k16[i], nt_dims, preferred_element_type=F32) for i in n]
        a = [jnp.where(strict, beta[i] * kk[i] * dmat[i], 0.0) for i in n]
        dd = [jnp.where(m8, a[i], 0.0) for i in n]
        dd16 = [b16(x) for x in dd]
        d2 = [mm(dd16[i], dd16[i]) for i in n]
        d216 = [b16(x) for x in d2]
        d3 = [mm(dd16[i], d216[i]) for i in n]
        d4 = [mm(d216[i], d216[i]) for i in n]
        e = [d2[i] - dd[i] - d3[i] for i in n]
        t = [mm(b16(e[i]), b16(d4[i])) for i in n]
        e = [e[i] + d4[i] + t[i] for i in n]
        for sel in (l16, l32, l64):
            l = [jnp.where(sel, a[i], 0.0) for i in n]
            e16 = [b16(x) for x in e]
            ly = [l[i] + mm(e16[i], b16(l[i])) for i in n]
            z = [mm(b16(ly[i]), e16[i]) for i in n]
            e = [e[i] - ly[i] - z[i] for i in n]
        r = [jnp.concatenate([k[i] * (beta[i] * eg[i]), v[i] * beta[i]], axis=1) for i in n]
        wu = [r[i] + mm(b16(e[i]), b16(r[i])) for i in n]
        wu16 = [b16(x) for x in wu]
        qo = [mm(b16(qk[i] * dmat[i]), wu16[i]) for i in n]
        kd = [b16(k[i] * col(chunk[units[i][0]][2], lane_g(units[i][1]))) for i in n]
        pb = [lax.dot_general(kd[i], wu16[i], tn_dims, preferred_element_type=F32) for i in n]
        for i, (ci, h) in enumerate(units):
            u = d * upd + i
            pq_scr[u, 0:DN_DK, :] = b16(pb[i][:, :DN_DK])
            pq_scr[u, DN_DK:DN_DK + c, :] = b16(q[i] * eg[i] - qo[i][:, :DN_DK])
            b_scr[u] = pb[i][:, DN_DK:]
            o_scr[u] = qo[i][:, DN_DK:]
            gt_scr[u] = jnp.broadcast_to(col(chunk[ci][3], lane_g(h)), (1, LANES))

    def scan(step):
        chains = [(0, step, h) for h in range(DN_HEADS)] + [(1, n_chunks - 1 - step, h) for h in range(DN_HEADS)]
        s = [s_ref[d * DN_HEADS + h] for d, ci, h in chains]
        ps = [mm(pq_scr[d * upd + ci * DN_HEADS + h], b16(s[i])) for i, (d, ci, h) in enumerate(chains)]
        for i, (d, ci, h) in enumerate(chains):
            u = d * upd + ci * DN_HEADS + h
            s_ref[d * DN_HEADS + h] = s[i] * gt_scr[u] - ps[i][0:DN_DK, :] + b_scr[u]
            dirs[d][4][ci * c:(ci + 1) * c, h * DN_DV:(h + 1) * DN_DV] = ps[i][DN_DK:DN_DK + c, :] + o_scr[u]

    prep(0)
    prep(1)
    for step in range(n_chunks):
        scan(step)

    if emit_state:
        @pl.when(t == nt - 1)
        def _():
            for d in range(2):
                for h in range(DN_HEADS):
                    sfin_ref[0, d, h] = s_ref[d * DN_HEADS + h]


def _deltanet(q, k, v, gates, s0, layer, row0, n_seq, seq_len, emit_state, o_prev=None):
    tile = MIX_TILE
    tps = seq_len // tile
    t0 = row0 // tile
    zero_init = s0 is None
    fwd = lambda w: pl.BlockSpec((tile, w), lambda b, t: (t0 + b * tps + t, 0))
    bwd = lambda w: pl.BlockSpec((tile, w), lambda b, t: (t0 + b * tps + (tps - 1 - t), 0))
    in_specs = [fwd(QK_W), fwd(QK_W), fwd(DN_W), fwd(LANES), bwd(QK_W), bwd(QK_W), bwd(DN_W), bwd(LANES)]
    args = [q, k, v, gates, q, k, v, gates]
    if not zero_init:
        in_specs.append(pl.BlockSpec((1, 1, 2, DN_HEADS, DN_DK, DN_DV), lambda b, t: (b, layer, 0, 0, 0, 0)))
        args.append(s0)
    aliases = {}
    if o_prev is not None:
        for j, a in enumerate(o_prev):
            aliases[len(args)] = j
            in_specs.append(pl.BlockSpec(memory_space=pl.ANY))
            args.append(a)
    out_specs = [fwd(DN_W), bwd(DN_W)]
    out_shape = [jax.ShapeDtypeStruct((q.shape[0], DN_W), F32)] * 2
    if emit_state:
        out_specs.append(pl.BlockSpec((1, 2, DN_HEADS, DN_DK, DN_DV), lambda b, t: (b, 0, 0, 0, 0)))
        out_shape.append(jax.ShapeDtypeStruct((n_seq, 2, DN_HEADS, DN_DK, DN_DV), F32))
    units = 2 * (tile // DN_CHUNK) * DN_HEADS
    return pl.pallas_call(
        functools.partial(_deltanet_kernel, tile=tile, zero_init=zero_init, emit_state=emit_state,
                          n_aliased=len(aliases)),
        grid=(n_seq, tps),
        in_specs=in_specs,
        out_specs=out_specs,
        out_shape=out_shape,
        input_output_aliases=aliases,
        scratch_shapes=[pltpu.VMEM((2 * DN_HEADS, DN_DK, DN_DV), F32),
                        pltpu.VMEM((units, DN_DK + DN_CHUNK, DN_DV), BF16),
                        pltpu.VMEM((units, DN_DK, DN_DV), F32),
                        pltpu.VMEM((units, DN_CHUNK, DN_DV), F32),
                        pltpu.VMEM((units, 1, LANES), F32)],
        compiler_params=pltpu.CompilerParams(dimension_semantics=("arbitrary", "arbitrary"),
                                             vmem_limit_bytes=VMEM_LIMIT),
        name="deltanet_scan",
    )(*args)


def _outproj_kernel(ya_ref, of_ref, ob_ref, z_ref, yc_ref, x_ref, mod_ref, wout_ref, vec_ref, dng_ref,
                    wr_hi_ref, wr_lo_ref, br_ref, x1_ref, h_ref, route_ref, cnt_ref, run_ref,
                    *, tile, n_ctx, dec_seq):
    m = _mod_row(mod_ref, pl.program_id(0) * tile, n_ctx, dec_seq)
    gt1 = m[:, 2 * D_MODEL:3 * D_MODEL]
    sh2 = m[:, 3 * D_MODEL:4 * D_MODEL]
    sc2 = m[:, 4 * D_MODEL:5 * D_MODEL]
    o = of_ref[...] + ob_ref[...]
    z = z_ref[...]
    zz = z * _sigmoid(z)
    hs = []
    for h in range(DN_HEADS):
        oh = o[:, h * DN_DV:(h + 1) * DN_DV]
        on = oh * lax.rsqrt(jnp.mean(oh * oh, axis=-1, keepdims=True) + EPS) * dng_ref[...]
        hs.append(on * zz[:, h * DN_DV:(h + 1) * DN_DV])
    ycat = jnp.concatenate([ya_ref[...]] + hs + [yc_ref[...]], axis=-1).astype(BF16)
    y = jnp.dot(ycat, wout_ref[...], preferred_element_type=F32)
    x1 = _ln(DEEPNORM_ALPHA * x_ref[...] + gt1 * y) * vec_ref[0:1, :] + vec_ref[1:2, :]
    x1_ref[...] = x1
    hf = _ln(x1) * (1.0 + sc2) + sh2
    h_hi = hf.astype(BF16)
    h_lo = (hf - h_hi.astype(F32)).astype(BF16)
    h_ref[...] = h_hi
    logits = (jnp.dot(h_hi, wr_hi_ref[...], preferred_element_type=F32)
              + jnp.dot(h_lo, wr_hi_ref[...], preferred_element_type=F32)
              + jnp.dot(h_hi, wr_lo_ref[...], preferred_element_type=F32)
              + br_ref[...])
    lane = lax.broadcasted_iota(jnp.int32, logits.shape, 1).astype(F32)
    vals, idxs = [], []
    for _ in range(TOP_K):
        m = jnp.max(logits, axis=-1, keepdims=True)
        idx = jnp.min(jnp.where(logits == m, lane, float(LANES)), axis=-1, keepdims=True)
        vals.append(m)
        idxs.append(idx)
        logits = jnp.where(lane == idx, -jnp.inf, logits)
    es = [jnp.exp(v - vals[0]) for v in vals]
    den = (es[0] + es[1]) + (es[2] + es[3])
    @pl.when(pl.program_id(0) == 0)
    def _():
        run_ref[...] = jnp.zeros_like(run_ref)

    onehot = jnp.zeros(logits.shape, F32)
    for idx in idxs:
        onehot = onehot + (lane == idx).astype(F32)
    ri = lax.broadcasted_iota(jnp.int32, (tile, tile), 0)
    rj = lax.broadcasted_iota(jnp.int32, (tile, tile), 1)
    before = jnp.dot((ri > rj).astype(BF16), onehot.astype(BF16), preferred_element_type=F32) + run_ref[...]
    run_ref[...] = run_ref[...] + jnp.sum(onehot, axis=0, keepdims=True)
    cnt_ref[...] = jnp.broadcast_to(run_ref[...], cnt_ref.shape)
    route = jnp.zeros(logits.shape, F32)
    for j in range(TOP_K):
        rank = jnp.sum(jnp.where(lane == idxs[j], before, 0.0), axis=-1, keepdims=True)
        route = jnp.where(lane == float(j), idxs[j], route)
        route = jnp.where(lane == float(TOP_K + j), es[j] / den, route)
        route = jnp.where(lane == float(2 * TOP_K + j), rank, route)
    route_ref[...] = route


def _out_projection(ya, o_f, o_b, proj, yc, x, mod, lp, n_ctx, dec_seq):
    nt, d = x.shape
    tile = TOK_TILE
    tok = lambda w: pl.BlockSpec((tile, w), lambda i: (i, 0))
    full = lambda a: pl.BlockSpec(a.shape, lambda i: (0,) * a.ndim)
    return pl.pallas_call(
        functools.partial(_outproj_kernel, tile=tile, n_ctx=n_ctx, dec_seq=dec_seq),
        grid=(nt // tile,),
        in_specs=[tok(CONV_W), tok(DN_W), tok(DN_W),
                  pl.BlockSpec((tile, DN_W), lambda i: (i, COL_Z // DN_W)),
                  tok(GMLP_W), tok(d), full(mod), full(lp["w_out"]), full(lp["ln1"]), full(lp["dn_norm_g"]),
                  full(lp["wr_hi"]), full(lp["wr_lo"]), full(lp["b_router"])],
        out_specs=[tok(d), tok(d), tok(LANES), pl.BlockSpec((8, LANES), lambda i: (0, 0))],
        out_shape=[jax.ShapeDtypeStruct((nt, d), F32), jax.ShapeDtypeStruct((nt, d), BF16),
                   jax.ShapeDtypeStruct((nt, LANES), F32), jax.ShapeDtypeStruct((8, LANES), F32)],
        scratch_shapes=[pltpu.VMEM((1, LANES), F32)],
        compiler_params=pltpu.CompilerParams(dimension_semantics=("arbitrary",), vmem_limit_bytes=VMEM_LIMIT),
        name="out_projection_router",
    )(ya, o_f, o_b, proj, yc, x, mod, lp["w_out"], lp["ln1"], lp["dn_norm_g"],
      lp["wr_hi"], lp["wr_lo"], lp["b_router"])


def _expert_kernel(be_ref, nu_ref, x_ref, wgu_ref, bgu_ref, wd_ref, bd_ref, out_ref, wgu16, wd16):
    i = pl.program_id(0)

    @pl.when((i == 0) | (be_ref[i] != be_ref[jnp.maximum(i - 1, 0)]))
    def _():
        wgu16[...] = wgu_ref[0, 0].astype(BF16)
        wd16[...] = wd_ref[0, 0].astype(BF16)

    @pl.when(i < nu_ref[0])
    def _():
        gu = jnp.dot(x_ref[...], wgu16[...], preferred_element_type=F32) + bgu_ref[0, 0]
        gate = jnp.minimum(gu[:, :D_FF], SWIGLU_LIMIT)
        up = jnp.clip(gu[:, D_FF:], -SWIGLU_LIMIT, SWIGLU_LIMIT)
        act = gate * _sigmoid(SWIGLU_ALPHA * gate)
        hmid = ((up + 1.0) * act).astype(BF16)
        out_ref[...] = jnp.dot(hmid, wd16[...], preferred_element_type=F32) + bd_ref[0, 0]

    @pl.when(i >= nu_ref[0])
    def _():
        out_ref[...] = jnp.zeros_like(out_ref)


def _experts(xg, block_e, n_used, w_gu, b_gu, w_down, b_down, layer):
    m_pad, d = xg.shape
    bm = MOE_BM
    grid_spec = pltpu.PrefetchScalarGridSpec(
        num_scalar_prefetch=2,
        grid=(m_pad // bm,),
        in_specs=[
            pl.BlockSpec((bm, d), lambda i, be, nu: (i, 0)),
            pl.BlockSpec((1, 1, d, 2 * D_FF), lambda i, be, nu: (layer, be[i], 0, 0)),
            pl.BlockSpec((1, 1, 1, 2 * D_FF), lambda i, be, nu: (layer, be[i], 0, 0)),
            pl.BlockSpec((1, 1, D_FF, d), lambda i, be, nu: (layer, be[i], 0, 0)),
            pl.BlockSpec((1, 1, 1, d), lambda i, be, nu: (layer, be[i], 0, 0)),
        ],
        out_specs=pl.BlockSpec((bm, d), lambda i, be, nu: (i, 0)),
        scratch_shapes=[pltpu.VMEM((d, 2 * D_FF), BF16), pltpu.VMEM((D_FF, d), BF16)],
    )
    return pl.pallas_call(
        _expert_kernel,
        grid_spec=grid_spec,
        out_shape=jax.ShapeDtypeStruct((m_pad, d), F32),
        compiler_params=pltpu.CompilerParams(dimension_semantics=("arbitrary",), vmem_limit_bytes=VMEM_LIMIT),
        name="moe_experts",
    )(block_e, n_used, xg, w_gu, b_gu, w_down, b_down)


def _combine_kernel(yg_ref, route_ref, x1_ref, mod_ref, vec_ref, out_ref, *, tile, n_ctx, dec_seq):
    m = _mod_row(mod_ref, pl.program_id(0) * tile, n_ctx, dec_seq)
    gt2 = m[:, 5 * D_MODEL:6 * D_MODEL]
    gate = lambda j: route_ref[:, TOP_K + j:TOP_K + j + 1]
    y = (yg_ref[0] * gate(0) + yg_ref[1] * gate(1)) + (yg_ref[2] * gate(2) + yg_ref[3] * gate(3))
    out_ref[...] = _ln(DEEPNORM_ALPHA * x1_ref[...] + gt2 * y) * vec_ref[0:1, :] + vec_ref[1:2, :]


def _combine(yg, route, x1, mod, ln2, n_ctx, dec_seq):
    nt, d = x1.shape
    tile = TOK_TILE
    return pl.pallas_call(
        functools.partial(_combine_kernel, tile=tile, n_ctx=n_ctx, dec_seq=dec_seq),
        grid=(nt // tile,),
        in_specs=[pl.BlockSpec((TOP_K, tile, d), lambda i: (0, i, 0)),
                  pl.BlockSpec((tile, LANES), lambda i: (i, 0)),
                  pl.BlockSpec((tile, d), lambda i: (i, 0)),
                  pl.BlockSpec(mod.shape, lambda i: (0, 0)),
                  pl.BlockSpec(ln2.shape, lambda i: (0, 0))],
        out_specs=pl.BlockSpec((tile, d), lambda i: (i, 0)),
        out_shape=jax.ShapeDtypeStruct((nt, d), F32),
        compiler_params=pltpu.CompilerParams(vmem_limit_bytes=VMEM_LIMIT),
        name="moe_combine_ln",
    )(yg, route, x1, mod, ln2)


def _route(route, counts):
    nt = route.shape[0]
    nk = nt * TOP_K
    bm = MOE_BM
    expert = route[:, 0:TOP_K].astype(jnp.int32)
    rank = route[:, 2 * TOP_K:3 * TOP_K].astype(jnp.int32)
    counts = counts[0, :N_EXPERTS].astype(jnp.int32)
    padded = (counts + bm - 1) // bm * bm
    pad_end = jnp.cumsum(padded)
    pad_start = pad_end - padded
    grp_start = jnp.cumsum(counts) - counts
    pair_slot = pad_start[expert] + rank
    tok = jnp.arange(nk, dtype=jnp.int32) // TOP_K
    _, sorted_tok = lax.sort((pair_slot.reshape(-1), tok), num_keys=1)
    n_blocks = nk // bm + N_EXPERTS
    blk_start = jnp.arange(n_blocks, dtype=jnp.int32) * bm
    block_e = jnp.minimum(jnp.sum((pad_end[None, :] <= blk_start[:, None]).astype(jnp.int32), axis=1),
                          N_EXPERTS - 1)
    slot_e = jnp.repeat(block_e, bm)
    j = jnp.arange(n_blocks * bm, dtype=jnp.int32) - pad_start[slot_e]
    src = jnp.clip(grp_start[slot_e] + j, 0, nk - 1)
    slot_tok = jnp.where(j < counts[slot_e], sorted_tok[src], 0)
    n_used = (pad_end[-1] // bm).astype(jnp.int32).reshape(1)
    return slot_tok, pair_slot, block_e.astype(jnp.int32), n_used


def _grid_pos_embed(t, d):
    rows = t // GRID_W
    r, col = jnp.meshgrid(jnp.arange(rows), jnp.arange(GRID_W), indexing="ij")
    r = r.reshape(-1).astype(F32)[:, None]
    col = col.reshape(-1).astype(F32)[:, None]
    n_freq = d // 4
    omega = 1.0 / (POS_BASE ** (jnp.arange(n_freq, dtype=F32) / n_freq))
    return jnp.concatenate([jnp.sin(r * omega), jnp.cos(r * omega),
                            jnp.sin(col * omega), jnp.cos(col * omega)], axis=-1)


def _pad_lanes(a, offset):
    return jnp.zeros((1, LANES), F32).at[0, offset:offset + a.shape[0]].set(a.astype(F32))


def _layer_params(l, w_in, conv_dw, conv_b, conv_ln_g, conv_ln_b, dn_conv, dn_a_log, dn_dt_bias, dn_norm_g,
                  gm_ln_g, gm_ln_b, gm_ws, gm_bs, w_out, ln1_g, ln1_b, ln2_g, ln2_b, w_router, b_router):
    wi = w_in[l]
    c_conv, c_qkv, c_z, c_ba = 0, 2 * CONV_W, 2 * CONV_W + 2 * QK_W + DN_W, 2 * CONV_W + 2 * QK_W + 2 * DN_W
    c_gm = c_ba + 4 * DN_HEADS
    w_in_r = jnp.concatenate([
        wi[:, c_qkv:c_z], wi[:, c_conv:c_qkv], wi[:, c_z:c_ba], wi[:, c_gm:],
        wi[:, c_ba:c_gm], jnp.zeros((D_MODEL, LANES - 4 * DN_HEADS), F32)], axis=1).astype(BF16)
    grp = jnp.arange(CONV_W) // GROUP_W
    gavg = (grp[:, None] == grp[None, :]).astype(BF16) * (1.0 / GROUP_W)
    gvec = jnp.concatenate([_pad_lanes(dn_a_log[l].reshape(-1), 2 * DN_HEADS),
                            _pad_lanes(dn_dt_bias[l].reshape(-1), 2 * DN_HEADS)], axis=0)
    wr = jnp.zeros((D_MODEL, LANES), F32).at[:, :N_EXPERTS].set(w_router[l])
    wr_hi = wr.astype(BF16)
    wr_lo = (wr - wr_hi.astype(F32)).astype(BF16)
    return {
        "w_in_r": w_in_r,
        "conv_dw": conv_dw[l],
        "conv_p": jnp.stack([conv_b[l], conv_ln_g[l], conv_ln_b[l]]),
        "dn_conv": dn_conv[l],
        "gvec": gvec,
        "gm_p": jnp.stack([gm_ln_g[l], gm_ln_b[l]]),
        "gm_ws": gm_ws[l],
        "gm_bsf": jnp.repeat(jnp.transpose(gm_bs[l]), GROUP_W, axis=1),
        "gavg": gavg,
        "w_out": w_out[l].astype(BF16),
        "ln1": jnp.stack([ln1_g[l], ln1_b[l]]),
        "ln2": jnp.stack([ln2_g[l], ln2_b[l]]),
        "dn_norm_g": dn_norm_g[l].reshape(1, DN_DV),
        "wr_hi": wr_hi,
        "wr_lo": wr_lo,
        "b_router": jnp.full((1, LANES), -1e30, F32).at[0, :N_EXPERTS].set(b_router[l]),
    }


def kernel(x_prompt, x_sample, state_delta, c, c_ctx, w_ada, b_ada, w_in, conv_dw, conv_b, conv_ln_g, conv_ln_b, dn_conv, dn_a_log, dn_dt_bias, dn_norm_g, gm_ln_g, gm_ln_b, gm_ws, gm_bs, w_out, ln1_g, ln1_b, ln2_g, ln2_b, w_router, b_router, w_gu, b_gu, w_down, b_down):
    batch, seq, d = x_prompt.shape
    dec_batch, dec_seq, _ = x_sample.shape
    n_ctx = batch * seq
    n_den = dec_batch * dec_seq
    depth = w_in.shape[0]

    cond = jnp.zeros((16, d), F32).at[0].set(c_ctx).at[1:1 + dec_batch].set(c)
    mod_all = _modulation(cond, w_ada, b_ada)
    x = _assemble_tokens(x_prompt.reshape(n_ctx, d), x_sample.reshape(n_den, d),
                         _grid_pos_embed(dec_seq, d), TOK_TILE)

    b_gu_r = b_gu.reshape(depth, N_EXPERTS, 1, 2 * D_FF)
    b_down_r = b_down.reshape(depth, N_EXPERTS, 1, d)

    ctx_states = []
    for l in range(depth):
        lp = _layer_params(l, w_in, conv_dw, conv_b, conv_ln_g, conv_ln_b, dn_conv, dn_a_log, dn_dt_bias,
                           dn_norm_g, gm_ln_g, gm_ln_b, gm_ws, gm_bs, w_out, ln1_g, ln1_b, ln2_g, ln2_b,
                           w_router, b_router)
        mod = mod_all[l]
        proj = _in_projection(x, mod, lp["w_in_r"], n_ctx, dec_seq)
        ya, yc, q, k, v, gates = _local_mixers(proj, lp, n_ctx // MIX_TILE, seq // MIX_TILE, dec_seq // MIX_TILE)
        o_f, o_b, s_fin = _deltanet(q, k, v, gates, None, l, 0, batch, seq, True)
        o_f, o_b = _deltanet(q, k, v, gates, state_delta, l, n_ctx, dec_batch, dec_seq, False, o_prev=(o_f, o_b))
        ctx_states.append(s_fin)
        x1, hffn, route, counts = _out_projection(ya, o_f, o_b, proj, yc, x, mod, lp, n_ctx, dec_seq)
        slot_tok, pair_slot, block_e, n_used = _route(route, counts)
        xg = jnp.take(hffn, slot_tok, axis=0)
        yb = _experts(xg, block_e, n_used, w_gu, b_gu_r, w_down, b_down_r, l)
        yg = jnp.take(yb, jnp.transpose(pair_slot), axis=0)
        x = _combine(yg, route, x1, mod, lp["ln2"], n_ctx, dec_seq)

    new_state = jnp.stack(ctx_states, axis=1).astype(x_prompt.dtype)
    return (x[:n_ctx].reshape(batch, seq, d), x[n_ctx:].reshape(dec_batch, dec_seq, d), new_state)
```

```python
import functools

import jax
import jax.numpy as jnp
from jax import lax
from jax.experimental import pallas as pl
from jax.experimental.pallas import tpu as pltpu

F32 = jnp.float32
BF16 = jnp.bfloat16
HIGHEST = lax.Precision.HIGHEST

D_MODEL = 1024
DEPTH = 2
GRID_W = 64
POS_BASE = 10000.0
CONV_W = 256
CONV_K = 31
CONV_HALO = 16
DN_HEADS = 4
DN_DK = 128
DN_DV = 128
QK_W = DN_HEADS * DN_DK
DN_W = DN_HEADS * DN_DV
DN_SHORT_K = 5
DN_HALO = 8
DN_CHUNK = 64
GMLP_W = 256
GMLP_GROUPS = 4
GMLP_CHUNK = 128
GROUP_W = 64
N_EXPERTS = 32
TOP_K = 4
D_FF = D_MODEL
SWIGLU_ALPHA = 1.702
SWIGLU_LIMIT = 7.0
DEEPNORM_ALPHA = (2 * DEPTH) ** 0.25
EPS = 1e-6

LANES = 128
COL_QKV = 0
COL_CONV = 2 * QK_W + DN_W
COL_Z = COL_CONV + 2 * CONV_W
COL_GM = COL_Z + DN_W
COL_BA = COL_GM + 2 * GMLP_W
PROJ_W = COL_BA + LANES

MIX_TILE = 256
TOK_TILE = 512
MOE_BM = 512
VMEM_LIMIT = 56 * 1024 * 1024


def _ln(x):
    mu = jnp.mean(x, axis=-1, keepdims=True)
    xc = x - mu
    return xc * lax.rsqrt(jnp.mean(xc * xc, axis=-1, keepdims=True) + EPS)


def _sigmoid(x):
    return jax.nn.sigmoid(x)


def _split_bf16(x, parts):
    out = []
    r = x
    for _ in range(parts):
        p = r.astype(BF16)
        out.append(p)
        r = r - p.astype(F32)
    return out


def _dot_exact_rhs(x, m_bf16, parts=3):
    acc = None
    for p in _split_bf16(x, parts):
        t = jnp.dot(p, m_bf16, preferred_element_type=F32)
        acc = t if acc is None else acc + t
    return acc


def _dot_exact_lhs(m_bf16, x, parts=3):
    acc = None
    for p in _split_bf16(x, parts):
        t = jnp.dot(m_bf16, p, preferred_element_type=F32)
        acc = t if acc is None else acc + t
    return acc


def _mod_row(mod_ref, start, n_ctx, dec_seq):
    row = jnp.where(start < n_ctx, 0, 1 + (start - n_ctx) // dec_seq)
    return mod_ref[pl.ds(row, 1), :]


def _mod_kernel(cond_ref, w_ref, b_ref, out_ref):
    c = cond_ref[...]
    s = c * _sigmoid(c)
    out_ref[0] = jnp.dot(s, w_ref[0], preferred_element_type=F32) + b_ref[0]


def _modulation(cond, w_ada, b_ada):
    nl, d, n = w_ada.shape
    r = cond.shape[0]
    tn = 1024
    return pl.pallas_call(
        _mod_kernel,
        grid=(nl, n // tn),
        in_specs=[
            pl.BlockSpec((r, d), lambda l, j: (0, 0)),
            pl.BlockSpec((1, d, tn), lambda l, j: (l, 0, j)),
            pl.BlockSpec((1, 1, tn), lambda l, j: (l, 0, j)),
        ],
        out_specs=pl.BlockSpec((1, r, tn), lambda l, j: (l, 0, j)),
        out_shape=jax.ShapeDtypeStruct((nl, r, n), F32),
        name="adaln_modulation",
    )(cond, w_ada, b_ada.reshape(nl, 1, n))


def _assemble_kernel(xp_ref, xs_ref, pos_ref, out_ref, *, n_ctx_tiles):
    i = pl.program_id(0)

    @pl.when(i < n_ctx_tiles)
    def _():
        out_ref[...] = xp_ref[...]

    @pl.when(i >= n_ctx_tiles)
    def _():
        out_ref[...] = xs_ref[...] + pos_ref[...]


def _assemble_tokens(xp_flat, xs_flat, pos, tile):
    n_ctx, d = xp_flat.shape
    n_den = xs_flat.shape[0]
    ct = n_ctx // tile
    pt = pos.shape[0] // tile
    return pl.pallas_call(
        functools.partial(_assemble_kernel, n_ctx_tiles=ct),
        grid=((n_ctx + n_den) // tile,),
        in_specs=[
            pl.BlockSpec((tile, d), lambda i: (jnp.minimum(i, ct - 1), 0)),
            pl.BlockSpec((tile, d), lambda i: (jnp.maximum(i - ct, 0), 0)),
            pl.BlockSpec((tile, d), lambda i: (jnp.maximum(i - ct, 0) % pt, 0)),
        ],
        out_specs=pl.BlockSpec((tile, d), lambda i: (i, 0)),
        out_shape=jax.ShapeDtypeStruct((n_ctx + n_den, d), F32),
        name="assemble_tokens",
    )(xp_flat, xs_flat, pos)


def _inproj_kernel(x_ref, mod_ref, w_ref, out_ref, *, tile, n_ctx, dec_seq):
    m = _mod_row(mod_ref, pl.program_id(0) * tile, n_ctx, dec_seq)
    sh1 = m[:, 0:D_MODEL]
    sc1 = m[:, D_MODEL:2 * D_MODEL]
    h = _ln(x_ref[...]) * (1.0 + sc1) + sh1
    out_ref[...] = jnp.dot(h.astype(BF16), w_ref[...], preferred_element_type=F32)


def _in_projection(x, mod, w_in_r, n_ctx, dec_seq):
    nt, d = x.shape
    tile = TOK_TILE
    return pl.pallas_call(
        functools.partial(_inproj_kernel, tile=tile, n_ctx=n_ctx, dec_seq=dec_seq),
        grid=(nt // tile,),
        in_specs=[
            pl.BlockSpec((tile, d), lambda i: (i, 0)),
            pl.BlockSpec(mod.shape, lambda i: (0, 0)),
            pl.BlockSpec(w_in_r.shape, lambda i: (0, 0)),
        ],
        out_specs=pl.BlockSpec((tile, PROJ_W), lambda i: (i, 0)),
        out_shape=jax.ShapeDtypeStruct((nt, PROJ_W), F32),
        compiler_params=pltpu.CompilerParams(vmem_limit_bytes=VMEM_LIMIT),
        name="in_projection",
    )(x, mod, w_in_r)


def _group_norm(x, gavg):
    mean = _dot_exact_rhs(x, gavg)
    xc = x - mean
    var = _dot_exact_rhs(xc * xc, gavg)
    return xc * lax.rsqrt(var + EPS)


def _mixpre_kernel(qkv_ref, qkv_p_ref, qkv_n_ref, cv_ref, cv_p_ref, cv_n_ref, gm_ref, ba_ref,
                   convw_ref, convp_ref, dnw_ref, gvec_ref, gmp_ref, ws_ref, bsf_ref, gavg_ref,
                   ya_ref, yc_ref, q_ref, k_ref, v_ref, gates_ref, cbuf, qbuf,
                   *, tile, n_ctx_tiles, ctx_tps, den_tps):
    i = pl.program_id(0)
    pos = jnp.where(i < n_ctx_tiles, i % ctx_tps, (i - n_ctx_tiles) % den_tps)
    tps = jnp.where(i < n_ctx_tiles, ctx_tps, den_tps)
    first = pos == 0
    last = pos == tps - 1
    gavg = gavg_ref[...]

    def glu(p):
        return p[:, :CONV_W] * _sigmoid(p[:, CONV_W:])

    cbuf[0:CONV_HALO, :] = jnp.where(first, 0.0, glu(cv_p_ref[...]))
    cbuf[CONV_HALO:CONV_HALO + tile, :] = glu(cv_ref[...])
    cbuf[CONV_HALO + tile:2 * CONV_HALO + tile, :] = jnp.where(last, 0.0, glu(cv_n_ref[...]))
    conv_b = convp_ref[0:1, :]
    conv_g = convp_ref[1:2, :]
    conv_beta = convp_ref[2:3, :]
    rc = 64
    off = CONV_HALO - CONV_K // 2
    for c in range(tile // rc):
        acc = jnp.zeros((rc, CONV_W), F32)
        for k in range(CONV_K):
            acc = acc + cbuf[c * rc + off + k:c * rc + off + k + rc, :] * convw_ref[k:k + 1, :]
        y = _group_norm(acc + conv_b, gavg) * conv_g + conv_beta
        ya_ref[c * rc:(c + 1) * rc, :] = y * _sigmoid(y)

    qbuf[0:DN_HALO, :] = jnp.where(first, 0.0, qkv_p_ref[...])
    qbuf[DN_HALO:DN_HALO + tile, :] = qkv_ref[...]
    qbuf[DN_HALO + tile:2 * DN_HALO + tile, :] = jnp.where(last, 0.0, qkv_n_ref[...])
    rq = 32
    offq = DN_HALO - DN_SHORT_K // 2
    outs = (q_ref, k_ref, v_ref)
    for part in range(3):
        c0 = part * QK_W
        for c in range(tile // rq):
            acc = jnp.zeros((rq, QK_W), F32)
            for k in range(DN_SHORT_K):
                acc = acc + (qbuf[c * rq + offq + k:c * rq + offq + k + rq, c0:c0 + QK_W]
                             * dnw_ref[k:k + 1, c0:c0 + QK_W])
            a = acc * _sigmoid(acc)
            if part < 2:
                scale = DN_DK ** -0.5 if part == 0 else 1.0
                hs = []
                for h in range(DN_HEADS):
                    ah = a[:, h * DN_DK:(h + 1) * DN_DK]
                    nrm = lax.rsqrt(jnp.sum(ah * ah, axis=-1, keepdims=True) + EPS)
                    hs.append(ah * (nrm * scale))
                a = jnp.concatenate(hs, axis=-1)
            outs[part][c * rq:(c + 1) * rq, :] = a

    p = ba_ref[...]
    beta = _sigmoid(p)
    xg = p + gvec_ref[1:2, :]
    softplus = jnp.maximum(xg, 0.0) + jnp.log1p(jnp.exp(-jnp.abs(xg)))
    g = -jnp.exp(gvec_ref[0:1, :]) * softplus
    lane = lax.broadcasted_iota(jnp.int32, p.shape, 1)
    gates_ref[...] = jnp.where(lane < 2 * DN_HEADS, beta, g)

    pg = gm_ref[...]
    ge = pg * (0.5 * (1.0 + jnp.tanh(0.7978845608028654 * (pg + 0.044715 * (pg * pg * pg)))))
    u = ge[:, :GMLP_W]
    vn = _group_norm(ge[:, GMLP_W:], gavg) * gmp_ref[0:1, :] + gmp_ref[1:2, :]
    grp = lax.broadcasted_iota(jnp.int32, (GMLP_CHUNK, GMLP_W), 1) // GROUP_W
    for n in range(tile // GMLP_CHUNK):
        vchunk = vn[n * GMLP_CHUNK:(n + 1) * GMLP_CHUNK, :]
        sg = bsf_ref[...]
        for gi in range(GMLP_GROUPS):
            r = jnp.dot(ws_ref[gi], vchunk, preferred_element_type=F32)
            sg = sg + jnp.where(grp == gi, r, 0.0)
        yc_ref[n * GMLP_CHUNK:(n + 1) * GMLP_CHUNK, :] = u[n * GMLP_CHUNK:(n + 1) * GMLP_CHUNK, :] * sg


def _local_mixers(proj, lp, n_ctx_tiles, ctx_tps, den_tps):
    nt = proj.shape[0]
    tile = MIX_TILE
    n_tiles = nt // tile
    cpb = tile // CONV_HALO
    qpb = tile // DN_HALO
    n_cblk = nt // CONV_HALO
    n_qblk = nt // DN_HALO
    col = lambda c, w: c // w
    full = lambda a: pl.BlockSpec(a.shape, lambda i: (0,) * a.ndim)
    in_specs = [
        pl.BlockSpec((tile, 3 * QK_W), lambda i: (i, col(COL_QKV, 3 * QK_W))),
        pl.BlockSpec((DN_HALO, 3 * QK_W), lambda i: (jnp.maximum(i * qpb - 1, 0), 0)),
        pl.BlockSpec((DN_HALO, 3 * QK_W), lambda i: (jnp.minimum((i + 1) * qpb, n_qblk - 1), 0)),
        pl.BlockSpec((tile, 2 * CONV_W), lambda i: (i, col(COL_CONV, 2 * CONV_W))),
        pl.BlockSpec((CONV_HALO, 2 * CONV_W), lambda i: (jnp.maximum(i * cpb - 1, 0), col(COL_CONV, 2 * CONV_W))),
        pl.BlockSpec((CONV_HALO, 2 * CONV_W),
                     lambda i: (jnp.minimum((i + 1) * cpb, n_cblk - 1), col(COL_CONV, 2 * CONV_W))),
        pl.BlockSpec((tile, 2 * GMLP_W), lambda i: (i, col(COL_GM, 2 * GMLP_W))),
        pl.BlockSpec((tile, LANES), lambda i: (i, col(COL_BA, LANES))),
        full(lp["conv_dw"]), full(lp["conv_p"]), full(lp["dn_conv"]), full(lp["gvec"]),
        full(lp["gm_p"]), full(lp["gm_ws"]), full(lp["gm_bsf"]), full(lp["gavg"]),
    ]
    tok = lambda w: pl.BlockSpec((tile, w), lambda i: (i, 0))
    shp = lambda w: jax.ShapeDtypeStruct((nt, w), F32)
    return pl.pallas_call(
        functools.partial(_mixpre_kernel, tile=tile, n_ctx_tiles=n_ctx_tiles, ctx_tps=ctx_tps, den_tps=den_tps),
        grid=(n_tiles,),
        in_specs=in_specs,
        out_specs=[tok(CONV_W), tok(GMLP_W), tok(QK_W), tok(QK_W), tok(DN_W), tok(LANES)],
        out_shape=[shp(CONV_W), shp(GMLP_W), shp(QK_W), shp(QK_W), shp(DN_W), shp(LANES)],
        scratch_shapes=[pltpu.VMEM((tile + 2 * CONV_HALO, CONV_W), F32),
                        pltpu.VMEM((tile + 2 * DN_HALO, 3 * QK_W), F32)],
        compiler_params=pltpu.CompilerParams(vmem_limit_bytes=VMEM_LIMIT),
        name="local_mixers",
    )(proj, proj, proj, proj, proj, proj, proj, proj,
      lp["conv_dw"], lp["conv_p"], lp["dn_conv"], lp["gvec"], lp["gm_p"], lp["gm_ws"], lp["gm_bsf"], lp["gavg"])


def _deltanet_kernel(*refs, tile, zero_init, emit_state, n_aliased):
    n_in = (8 if zero_init else 9) + n_aliased
    qf_ref, kf_ref, vf_ref, gf_ref, qb_ref, kb_ref, vb_ref, gb_ref = refs[:8]
    s0_ref = None if zero_init else refs[8]
    rest = refs[n_in:]
    if emit_state:
        of_ref, ob_ref, sfin_ref = rest[:3]
        rest = rest[3:]
    else:
        of_ref, ob_ref = rest[:2]
        sfin_ref = None
        rest = rest[2:]
    s_ref, pq_scr, b_scr, o_scr, gt_scr = rest
    t = pl.program_id(1)
    nt = pl.num_programs(1)
    c = DN_CHUNK
    n_chunks = tile // c
    upd = n_chunks * DN_HEADS

    @pl.when(t == 0)
    def _():
        if zero_init:
            s_ref[...] = jnp.zeros_like(s_ref)
        else:
            for d in range(2):
                for h in range(DN_HEADS):
                    s_ref[d * DN_HEADS + h] = s0_ref[0, 0, d, h]

    ii = lax.broadcasted_iota(jnp.int32, (c, c), 0)
    jj = lax.broadcasted_iota(jnp.int32, (c, c), 1)
    same = lambda s: (ii >> s) == (jj >> s)
    m8 = same(3)
    l16 = same(4) & ~m8
    l32 = same(5) & ~same(4)
    l64 = ~same(5)
    ti = lax.broadcasted_iota(jnp.int32, (tile, tile), 0)
    tj = lax.broadcasted_iota(jnp.int32, (tile, tile), 1)
    same_chunk = (ti // c) == (tj // c)
    nt_dims = (((1,), (1,)), ((), ()))
    tn_dims = (((0,), (0,)), ((), ()))
    dirs = (
        (qf_ref, kf_ref, vf_ref, gf_ref, of_ref, ii >= jj, ii > jj, ti >= tj, c - 1),
        (qb_ref, kb_ref, vb_ref, gb_ref, ob_ref, ii <= jj, ii < jj, ti <= tj, 0),
    )

    def gate_context(d):
        g_ref, tile_tri = dirs[d][3], dirs[d][7]
        gates = g_ref[...]
        blockcum = (same_chunk & tile_tri).astype(BF16)
        gc_t = _dot_exact_lhs(blockcum, gates)
        return gates, gc_t, gc_t.T

    gate_ctx = [gate_context(0), gate_context(1)]

    b16 = lambda x: x.astype(BF16)
    mm = lambda x, y: jnp.dot(x, y, preferred_element_type=F32)

    def prep(d):
        q_ref, k_ref, v_ref, _, _, incl, strict, _, last_row = dirs[d]
        gates, gc_t, gc_tt = gate_ctx[d]
        units = [(ci, h) for ci in range(n_chunks) for h in range(DN_HEADS)]
        chunk = []
        for ci in range(n_chunks):
            r0 = ci * c
            gc_c = gc_t[r0:r0 + c, :]
            glast = gc_t[r0 + last_row:r0 + last_row + 1, :]
            chunk.append((gc_c, jnp.exp(gc_c), jnp.exp(glast - gc_c), jnp.exp(glast)))
        lane_b = lambda h: d * DN_HEADS + h
        lane_g = lambda h: 2 * DN_HEADS + d * DN_HEADS + h
        col = lambda x, l: x[:, l:l + 1]
        beta = [col(gates[ci * c:(ci + 1) * c, :], lane_b(h)) for ci, h in units]
        eg = [col(chunk[ci][1], lane_g(h)) for ci, h in units]
        dmat = [jnp.where(incl, jnp.exp(jnp.minimum(
            col(chunk[ci][0], lane_g(h)) - gc_tt[lane_g(h):lane_g(h) + 1, ci * c:(ci + 1) * c], 0.0)), 0.0)
            for ci, h in units]
        q = [q_ref[ci * c:(ci + 1) * c, h * DN_DK:(h + 1) * DN_DK] for ci, h in units]
        k = [k_ref[ci * c:(ci + 1) * c, h * DN_DK:(h + 1) * DN_DK] for ci, h in units]
        v = [v_ref[ci * c:(ci + 1) * c, h * DN_DV:(h + 1) * DN_DV] for ci, h in units]
        k16 = [b16(x) for x in k]
        n = range(len(units))
        kk = [lax.dot_general(k16[i], k16[i], nt_dims, preferred_element_type=F32) for i in n]
        qk = [lax.dot_general(b16(q[i]), k16[i], nt_dims, preferred_element_type=F32) for i in n]
        a = [jnp.where(strict, beta[i] * kk[i] * dmat[i], 0.0) for i in n]
        dd = [jnp.where(m8, a[i], 0.0) for i in n]
        dd16 = [b16(x) for x in dd]
        d2 = [mm(dd16[i], dd16[i]) for i in n]
        d216 = [b16(x) for x in d2]
        d3 = [mm(dd16[i], d216[i]) for i in n]
        d4 = [mm(d216[i], d216[i]) for i in n]
        e = [d2[i] - dd[i] - d3[i] for i in n]
        t = [mm(b16(e[i]), b16(d4[i])) for i in n]
        e = [e[i] + d4[i] + t[i] for i in n]
        for sel in (l16, l32, l64):
            l = [jnp.where(sel, a[i], 0.0) for i in n]
            e16 = [b16(x) for x in e]
            ly = [l[i] + mm(e16[i], b16(l[i])) for i in n]
            z = [mm(b16(ly[i]), e16[i]) for i in n]
            e = [e[i] - ly[i] - z[i] for i in n]
        r = [jnp.concatenate([k[i] * (beta[i] * eg[i]), v[i] * beta[i]], axis=1) for i in n]
        wu = [r[i] + mm(b16(e[i]), b16(r[i])) for i in n]
        wu16 = [b16(x) for x in wu]
        qo = [mm(b16(qk[i] * dmat[i]), wu16[i]) for i in n]
        kd = [b16(k[i] * col(chunk[units[i][0]][2], lane_g(units[i][1]))) for i in n]
        pb = [lax.dot_general(kd[i], wu16[i], tn_dims, preferred_element_type=F32) for i in n]
        for i, (ci, h) in enumerate(units):
            u = d * upd + i
            pq_scr[u, 0:DN_DK, :] = b16(pb[i][:, :DN_DK])
            pq_scr[u, DN_DK:DN_DK + c, :] = b16(q[i] * eg[i] - qo[i][:, :DN_DK])
            b_scr[u] = pb[i][:, DN_DK:]
            o_scr[u] = qo[i][:, DN_DK:]
            gt_scr[u] = jnp.broadcast_to(col(chunk[ci][3], lane_g(h)), (1, LANES))

    def scan(step):
        chains = [(0, step, h) for h in range(DN_HEADS)] + [(1, n_chunks - 1 - step, h) for h in range(DN_HEADS)]
        s = [s_ref[d * DN_HEADS + h] for d, ci, h in chains]
        ps = [mm(pq_scr[d * upd + ci * DN_HEADS + h], b16(s[i])) for i, (d, ci, h) in enumerate(chains)]
        for i, (d, ci, h) in enumerate(chains):
            u = d * upd + ci * DN_HEADS + h
            s_ref[d * DN_HEADS + h] = s[i] * gt_scr[u] - ps[i][0:DN_DK, :] + b_scr[u]
            dirs[d][4][ci * c:(ci + 1) * c, h * DN_DV:(h + 1) * DN_DV] = ps[i][DN_DK:DN_DK + c, :] + o_scr[u]

    prep(0)
    prep(1)
    for step in range(n_chunks):
        scan(step)

    if emit_state:
        @pl.when(t == nt - 1)
        def _():
            for d in range(2):
                for h in range(DN_HEADS):
                    sfin_ref[0, d, h] = s_ref[d * DN_HEADS + h]


def _deltanet(q, k, v, gates, s0, layer, row0, n_seq, seq_len, emit_state, o_prev=None):
    tile = MIX_TILE
    tps = seq_len // tile
    t0 = row0 // tile
    zero_init = s0 is None
    fwd = lambda w: pl.BlockSpec((tile, w), lambda b, t: (t0 + b * tps + t, 0))
    bwd = lambda w: pl.BlockSpec((tile, w), lambda b, t: (t0 + b * tps + (tps - 1 - t), 0))
    in_specs = [fwd(QK_W), fwd(QK_W), fwd(DN_W), fwd(LANES), bwd(QK_W), bwd(QK_W), bwd(DN_W), bwd(LANES)]
    args = [q, k, v, gates, q, k, v, gates]
    if not zero_init:
        in_specs.append(pl.BlockSpec((1, 1, 2, DN_HEADS, DN_DK, DN_DV), lambda b, t: (b, layer, 0, 0, 0, 0)))
        args.append(s0)
    aliases = {}
    if o_prev is not None:
        for j, a in enumerate(o_prev):
            aliases[len(args)] = j
            in_specs.append(pl.BlockSpec(memory_space=pl.ANY))
            args.append(a)
    out_specs = [fwd(DN_W), bwd(DN_W)]
    out_shape = [jax.ShapeDtypeStruct((q.shape[0], DN_W), F32)] * 2
    if emit_state:
        out_specs.append(pl.BlockSpec((1, 2, DN_HEADS, DN_DK, DN_DV), lambda b, t: (b, 0, 0, 0, 0)))
        out_shape.append(jax.ShapeDtypeStruct((n_seq, 2, DN_HEADS, DN_DK, DN_DV), F32))
    units = 2 * (tile // DN_CHUNK) * DN_HEADS
    return pl.pallas_call(
        functools.partial(_deltanet_kernel, tile=tile, zero_init=zero_init, emit_state=emit_state,
                          n_aliased=len(aliases)),
        grid=(n_seq, tps),
        in_specs=in_specs,
        out_specs=out_specs,
        out_shape=out_shape,
        input_output_aliases=aliases,
        scratch_shapes=[pltpu.VMEM((2 * DN_HEADS, DN_DK, DN_DV), F32),
                        pltpu.VMEM((units, DN_DK + DN_CHUNK, DN_DV), BF16),
                        pltpu.VMEM((units, DN_DK, DN_DV), F32),
                        pltpu.VMEM((units, DN_CHUNK, DN_DV), F32),
                        pltpu.VMEM((units, 1, LANES), F32)],
        compiler_params=pltpu.CompilerParams(dimension_semantics=("arbitrary", "arbitrary"),
                                             vmem_limit_bytes=VMEM_LIMIT),
        name="deltanet_scan",
    )(*args)


def _outproj_kernel(ya_ref, of_ref, ob_ref, z_ref, yc_ref, x_ref, mod_ref, wout_ref, vec_ref, dng_ref,
                    wr_hi_ref, wr_lo_ref, br_ref, x1_ref, h_ref, route_ref, cnt_ref, run_ref,
                    *, tile, n_ctx, dec_seq):
    m = _mod_row(mod_ref, pl.program_id(0) * tile, n_ctx, dec_seq)
    gt1 = m[:, 2 * D_MODEL:3 * D_MODEL]
    sh2 = m[:, 3 * D_MODEL:4 * D_MODEL]
    sc2 = m[:, 4 * D_MODEL:5 * D_MODEL]
    o = of_ref[...] + ob_ref[...]
    z = z_ref[...]
    zz = z * _sigmoid(z)
    hs = []
    for h in range(DN_HEADS):
        oh = o[:, h * DN_DV:(h + 1) * DN_DV]
        on = oh * lax.rsqrt(jnp.mean(oh * oh, axis=-1, keepdims=True) + EPS) * dng_ref[...]
        hs.append(on * zz[:, h * DN_DV:(h + 1) * DN_DV])
    ycat = jnp.concatenate([ya_ref[...]] + hs + [yc_ref[...]], axis=-1).astype(BF16)
    y = jnp.dot(ycat, wout_ref[...], preferred_element_type=F32)
    x1 = _ln(DEEPNORM_ALPHA * x_ref[...] + gt1 * y) * vec_ref[0:1, :] + vec_ref[1:2, :]
    x1_ref[...] = x1
    hf = _ln(x1) * (1.0 + sc2) + sh2
    h_hi = hf.astype(BF16)
    h_lo = (hf - h_hi.astype(F32)).astype(BF16)
    h_ref[...] = h_hi
    logits = (jnp.dot(h_hi, wr_hi_ref[...], preferred_element_type=F32)
              + jnp.dot(h_lo, wr_hi_ref[...], preferred_element_type=F32)
              + jnp.dot(h_hi, wr_lo_ref[...], preferred_element_type=F32)
              + br_ref[...])
    lane = lax.broadcasted_iota(jnp.int32, logits.shape, 1).astype(F32)
    vals, idxs = [], []
    for _ in range(TOP_K):
        m = jnp.max(logits, axis=-1, keepdims=True)
        idx = jnp.min(jnp.where(logits == m, lane, float(LANES)), axis=-1, keepdims=True)
        vals.append(m)
        idxs.append(idx)
        logits = jnp.where(lane == idx, -jnp.inf, logits)
    es = [jnp.exp(v - vals[0]) for v in vals]
    den = (es[0] + es[1]) + (es[2] + es[3])
    @pl.when(pl.program_id(0) == 0)
    def _():
        run_ref[...] = jnp.zeros_like(run_ref)

    onehot = jnp.zeros(logits.shape, F32)
    for idx in idxs:
        onehot = onehot + (lane == idx).astype(F32)
    ri = lax.broadcasted_iota(jnp.int32, (tile, tile), 0)
    rj = lax.broadcasted_iota(jnp.int32, (tile, tile), 1)
    before = jnp.dot((ri > rj).astype(BF16), onehot.astype(BF16), preferred_element_type=F32) + run_ref[...]
    run_ref[...] = run_ref[...] + jnp.sum(onehot, axis=0, keepdims=True)
    cnt_ref[...] = jnp.broadcast_to(run_ref[...], cnt_ref.shape)
    route = jnp.zeros(logits.shape, F32)
    for j in range(TOP_K):
        rank = jnp.sum(jnp.where(lane == idxs[j], before, 0.0), axis=-1, keepdims=True)
        route = jnp.where(lane == float(j), idxs[j], route)
        route = jnp.where(lane == float(TOP_K + j), es[j] / den, route)
        route = jnp.where(lane == float(2 * TOP_K + j), rank, route)
    route_ref[...] = route


def _out_projection(ya, o_f, o_b, proj, yc, x, mod, lp, n_ctx, dec_seq):
    nt, d = x.shape
    tile = TOK_TILE
    tok = lambda w: pl.BlockSpec((tile, w), lambda i: (i, 0))
    full = lambda a: pl.BlockSpec(a.shape, lambda i: (0,) * a.ndim)
    return pl.pallas_call(
        functools.partial(_outproj_kernel, tile=tile, n_ctx=n_ctx, dec_seq=dec_seq),
        grid=(nt // tile,),
        in_specs=[tok(CONV_W), tok(DN_W), tok(DN_W),
                  pl.BlockSpec((tile, DN_W), lambda i: (i, COL_Z // DN_W)),
                  tok(GMLP_W), tok(d), full(mod), full(lp["w_out"]), full(lp["ln1"]), full(lp["dn_norm_g"]),
                  full(lp["wr_hi"]), full(lp["wr_lo"]), full(lp["b_router"])],
        out_specs=[tok(d), tok(d), tok(LANES), pl.BlockSpec((8, LANES), lambda i: (0, 0))],
        out_shape=[jax.ShapeDtypeStruct((nt, d), F32), jax.ShapeDtypeStruct((nt, d), BF16),
                   jax.ShapeDtypeStruct((nt, LANES), F32), jax.ShapeDtypeStruct((8, LANES), F32)],
        scratch_shapes=[pltpu.VMEM((1, LANES), F32)],
        compiler_params=pltpu.CompilerParams(dimension_semantics=("arbitrary",), vmem_limit_bytes=VMEM_LIMIT),
        name="out_projection_router",
    )(ya, o_f, o_b, proj, yc, x, mod, lp["w_out"], lp["ln1"], lp["dn_norm_g"],
      lp["wr_hi"], lp["wr_lo"], lp["b_router"])


def _expert_kernel(be_ref, nu_ref, x_ref, wgu_ref, bgu_ref, wd_ref, bd_ref, out_ref, wgu16, wd16):
    i = pl.program_id(0)

    @pl.when((i == 0) | (be_ref[i] != be_ref[jnp.maximum(i - 1, 0)]))
    def _():
        wgu16[...] = wgu_ref[0, 0].astype(BF16)
        wd16[...] = wd_ref[0, 0].astype(BF16)

    @pl.when(i < nu_ref[0])
    def _():
        gu = jnp.dot(x_ref[...], wgu16[...], preferred_element_type=F32) + bgu_ref[0, 0]
        gate = jnp.minimum(gu[:, :D_FF], SWIGLU_LIMIT)
        up = jnp.clip(gu[:, D_FF:], -SWIGLU_LIMIT, SWIGLU_LIMIT)
        act = gate * _sigmoid(SWIGLU_ALPHA * gate)
        hmid = ((up + 1.0) * act).astype(BF16)
        y = jnp.dot(hmid, wd16[...], preferred_element_type=F32) + bd_ref[0, 0]
        out_ref[...] = y.astype(out_ref.dtype)

    @pl.when(i >= nu_ref[0])
    def _():
        out_ref[...] = jnp.zeros_like(out_ref)


def _experts(xg, block_e, n_used, w_gu, b_gu, w_down, b_down, layer):
    m_pad, d = xg.shape
    bm = MOE_BM
    grid_spec = pltpu.PrefetchScalarGridSpec(
        num_scalar_prefetch=2,
        grid=(m_pad // bm,),
        in_specs=[
            pl.BlockSpec((bm, d), lambda i, be, nu: (i, 0)),
            pl.BlockSpec((1, 1, d, 2 * D_FF), lambda i, be, nu: (layer, be[i], 0, 0)),
            pl.BlockSpec((1, 1, 1, 2 * D_FF), lambda i, be, nu: (layer, be[i], 0, 0)),
            pl.BlockSpec((1, 1, D_FF, d), lambda i, be, nu: (layer, be[i], 0, 0)),
            pl.BlockSpec((1, 1, 1, d), lambda i, be, nu: (layer, be[i], 0, 0)),
        ],
        out_specs=pl.BlockSpec((bm, d), lambda i, be, nu: (i, 0)),
        scratch_shapes=[pltpu.VMEM((d, 2 * D_FF), BF16), pltpu.VMEM((D_FF, d), BF16)],
    )
    return pl.pallas_call(
        _expert_kernel,
        grid_spec=grid_spec,
        out_shape=jax.ShapeDtypeStruct((m_pad, d), BF16),
        compiler_params=pltpu.CompilerParams(dimension_semantics=("arbitrary",), vmem_limit_bytes=VMEM_LIMIT),
        name="moe_experts",
    )(block_e, n_used, xg, w_gu, b_gu, w_down, b_down)


def _combine_kernel(yg_ref, route_ref, x1_ref, mod_ref, vec_ref, out_ref, *, tile, n_ctx, dec_seq):
    m = _mod_row(mod_ref, pl.program_id(0) * tile, n_ctx, dec_seq)
    gt2 = m[:, 5 * D_MODEL:6 * D_MODEL]
    gate = lambda j: route_ref[:, TOP_K + j:TOP_K + j + 1]
    ye = lambda j: yg_ref[j].astype(F32) * gate(j)
    y = (ye(0) + ye(1)) + (ye(2) + ye(3))
    out_ref[...] = _ln(DEEPNORM_ALPHA * x1_ref[...] + gt2 * y) * vec_ref[0:1, :] + vec_ref[1:2, :]


def _combine(yg, route, x1, mod, ln2, n_ctx, dec_seq):
    nt, d = x1.shape
    tile = TOK_TILE
    return pl.pallas_call(
        functools.partial(_combine_kernel, tile=tile, n_ctx=n_ctx, dec_seq=dec_seq),
        grid=(nt // tile,),
        in_specs=[pl.BlockSpec((TOP_K, tile, d), lambda i: (0, i, 0)),
                  pl.BlockSpec((tile, LANES), lambda i: (i, 0)),
                  pl.BlockSpec((tile, d), lambda i: (i, 0)),
                  pl.BlockSpec(mod.shape, lambda i: (0, 0)),
                  pl.BlockSpec(ln2.shape, lambda i: (0, 0))],
        out_specs=pl.BlockSpec((tile, d), lambda i: (i, 0)),
        out_shape=jax.ShapeDtypeStruct((nt, d), F32),
        compiler_params=pltpu.CompilerParams(vmem_limit_bytes=VMEM_LIMIT),
        name="moe_combine_ln",
    )(yg, route, x1, mod, ln2)


def _route(route, counts):
    nt = route.shape[0]
    nk = nt * TOP_K
    bm = MOE_BM
    expert = route[:, 0:TOP_K].astype(jnp.int32)
    rank = route[:, 2 * TOP_K:3 * TOP_K].astype(jnp.int32)
    counts = counts[0, :N_EXPERTS].astype(jnp.int32)
    padded = (counts + bm - 1) // bm * bm
    pad_end = jnp.cumsum(padded)
    pad_start = pad_end - padded
    grp_start = jnp.cumsum(counts) - counts
    pair_slot = pad_start[expert] + rank
    tok = jnp.arange(nk, dtype=jnp.int32) // TOP_K
    _, sorted_tok = lax.sort((pair_slot.reshape(-1), tok), num_keys=1)
    n_blocks = nk // bm + N_EXPERTS
    blk_start = jnp.arange(n_blocks, dtype=jnp.int32) * bm
    block_e = jnp.minimum(jnp.sum((pad_end[None, :] <= blk_start[:, None]).astype(jnp.int32), axis=1),
                          N_EXPERTS - 1)
    slot_e = jnp.repeat(block_e, bm)
    j = jnp.arange(n_blocks * bm, dtype=jnp.int32) - pad_start[slot_e]
    src = jnp.clip(grp_start[slot_e] + j, 0, nk - 1)
    slot_tok = jnp.where(j < counts[slot_e], sorted_tok[src], 0)
    n_used = (pad_end[-1] // bm).astype(jnp.int32).reshape(1)
    return slot_tok, pair_slot, block_e.astype(jnp.int32), n_used


def _grid_pos_embed(t, d):
    rows = t // GRID_W
    r, col = jnp.meshgrid(jnp.arange(rows), jnp.arange(GRID_W), indexing="ij")
    r = r.reshape(-1).astype(F32)[:, None]
    col = col.reshape(-1).astype(F32)[:, None]
    n_freq = d // 4
    omega = 1.0 / (POS_BASE ** (jnp.arange(n_freq, dtype=F32) / n_freq))
    return jnp.concatenate([jnp.sin(r * omega), jnp.cos(r * omega),
                            jnp.sin(col * omega), jnp.cos(col * omega)], axis=-1)


def _pad_lanes(a, offset):
    return jnp.zeros((1, LANES), F32).at[0, offset:offset + a.shape[0]].set(a.astype(F32))


def _layer_params(l, w_in, conv_dw, conv_b, conv_ln_g, conv_ln_b, dn_conv, dn_a_log, dn_dt_bias, dn_norm_g,
                  gm_ln_g, gm_ln_b, gm_ws, gm_bs, w_out, ln1_g, ln1_b, ln2_g, ln2_b, w_router, b_router):
    wi = w_in[l]
    c_conv, c_qkv, c_z, c_ba = 0, 2 * CONV_W, 2 * CONV_W + 2 * QK_W + DN_W, 2 * CONV_W + 2 * QK_W + 2 * DN_W
    c_gm = c_ba + 4 * DN_HEADS
    w_in_r = jnp.concatenate([
        wi[:, c_qkv:c_z], wi[:, c_conv:c_qkv], wi[:, c_z:c_ba], wi[:, c_gm:],
        wi[:, c_ba:c_gm], jnp.zeros((D_MODEL, LANES - 4 * DN_HEADS), F32)], axis=1).astype(BF16)
    grp = jnp.arange(CONV_W) // GROUP_W
    gavg = (grp[:, None] == grp[None, :]).astype(BF16) * (1.0 / GROUP_W)
    gvec = jnp.concatenate([_pad_lanes(dn_a_log[l].reshape(-1), 2 * DN_HEADS),
                            _pad_lanes(dn_dt_bias[l].reshape(-1), 2 * DN_HEADS)], axis=0)
    wr = jnp.zeros((D_MODEL, LANES), F32).at[:, :N_EXPERTS].set(w_router[l])
    wr_hi = wr.astype(BF16)
    wr_lo = (wr - wr_hi.astype(F32)).astype(BF16)
    return {
        "w_in_r": w_in_r,
        "conv_dw": conv_dw[l],
        "conv_p": jnp.stack([conv_b[l], conv_ln_g[l], conv_ln_b[l]]),
        "dn_conv": dn_conv[l],
        "gvec": gvec,
        "gm_p": jnp.stack([gm_ln_g[l], gm_ln_b[l]]),
        "gm_ws": gm_ws[l],
        "gm_bsf": jnp.repeat(jnp.transpose(gm_bs[l]), GROUP_W, axis=1),
        "gavg": gavg,
        "w_out": w_out[l].astype(BF16),
        "ln1": jnp.stack([ln1_g[l], ln1_b[l]]),
        "ln2": jnp.stack([ln2_g[l], ln2_b[l]]),
        "dn_norm_g": dn_norm_g[l].reshape(1, DN_DV),
        "wr_hi": wr_hi,
        "wr_lo": wr_lo,
        "b_router": jnp.full((1, LANES), -1e30, F32).at[0, :N_EXPERTS].set(b_router[l]),
    }


def kernel(x_prompt, x_sample, state_delta, c, c_ctx, w_ada, b_ada, w_in, conv_dw, conv_b, conv_ln_g, conv_ln_b, dn_conv, dn_a_log, dn_dt_bias, dn_norm_g, gm_ln_g, gm_ln_b, gm_ws, gm_bs, w_out, ln1_g, ln1_b, ln2_g, ln2_b, w_router, b_router, w_gu, b_gu, w_down, b_down):
    batch, seq, d = x_prompt.shape
    dec_batch, dec_seq, _ = x_sample.shape
    n_ctx = batch * seq
    n_den = dec_batch * dec_seq
    depth = w_in.shape[0]

    cond = jnp.zeros((16, d), F32).at[0].set(c_ctx).at[1:1 + dec_batch].set(c)
    mod_all = _modulation(cond, w_ada, b_ada)
    x = _assemble_tokens(x_prompt.reshape(n_ctx, d), x_sample.reshape(n_den, d),
                         _grid_pos_embed(dec_seq, d), TOK_TILE)

    b_gu_r = b_gu.reshape(depth, N_EXPERTS, 1, 2 * D_FF)
    b_down_r = b_down.reshape(depth, N_EXPERTS, 1, d)

    ctx_states = []
    for l in range(depth):
        lp = _layer_params(l, w_in, conv_dw, conv_b, conv_ln_g, conv_ln_b, dn_conv, dn_a_log, dn_dt_bias,
                           dn_norm_g, gm_ln_g, gm_ln_b, gm_ws, gm_bs, w_out, ln1_g, ln1_b, ln2_g, ln2_b,
                           w_router, b_router)
        mod = mod_all[l]
        proj = _in_projection(x, mod, lp["w_in_r"], n_ctx, dec_seq)
        ya, yc, q, k, v, gates = _local_mixers(proj, lp, n_ctx // MIX_TILE, seq // MIX_TILE, dec_seq // MIX_TILE)
        o_f, o_b, s_fin = _deltanet(q, k, v, gates, None, l, 0, batch, seq, True)
        o_f, o_b = _deltanet(q, k, v, gates, state_delta, l, n_ctx, dec_batch, dec_seq, False, o_prev=(o_f, o_b))
        ctx_states.append(s_fin)
        x1, hffn, route, counts = _out_projection(ya, o_f, o_b, proj, yc, x, mod, lp, n_ctx, dec_seq)
        slot_tok, pair_slot, block_e, n_used = _route(route, counts)
        xg = hffn.at[slot_tok].get(mode="promise_in_bounds")
        yb = _experts(xg, block_e, n_used, w_gu, b_gu_r, w_down, b_down_r, l)
        yg = yb.at[jnp.transpose(pair_slot)].get(mode="promise_in_bounds")
        x = _combine(yg, route, x1, mod, lp["ln2"], n_ctx, dec_seq)

    new_state = jnp.stack(ctx_states, axis=1).astype(x_prompt.dtype)
    return (x[:n_ctx].reshape(batch, seq, d), x[n_ctx:].reshape(dec_batch, dec_seq, d), new_state)
```

```python
import functools

import jax
import jax.numpy as jnp
from jax import lax
from jax.experimental import pallas as pl
from jax.experimental.pallas import tpu as pltpu

F32 = jnp.float32
BF16 = jnp.bfloat16
HIGHEST = lax.Precision.HIGHEST

D_MODEL = 1024
DEPTH = 2
GRID_W = 64
POS_BASE = 10000.0
CONV_W = 256
CONV_K = 31
CONV_HALO = 16
DN_HEADS = 4
DN_DK = 128
DN_DV = 128
QK_W = DN_HEADS * DN_DK
DN_W = DN_HEADS * DN_DV
DN_SHORT_K = 5
DN_HALO = 8
DN_CHUNK = 64
GMLP_W = 256
GMLP_GROUPS = 4
GMLP_CHUNK = 128
GROUP_W = 64
N_EXPERTS = 32
TOP_K = 4
D_FF = D_MODEL
SWIGLU_ALPHA = 1.702
SWIGLU_LIMIT = 7.0
DEEPNORM_ALPHA = (2 * DEPTH) ** 0.25
EPS = 1e-6

LANES = 128
COL_QKV = 0
COL_CONV = 2 * QK_W + DN_W
COL_Z = COL_CONV + 2 * CONV_W
COL_GM = COL_Z + DN_W
COL_BA = COL_GM + 2 * GMLP_W
PROJ_W = COL_BA + LANES

MIX_TILE = 256
TOK_TILE = 512
MOE_BM = 512
VMEM_LIMIT = 56 * 1024 * 1024


def _ln(x):
    mu = jnp.mean(x, axis=-1, keepdims=True)
    xc = x - mu
    return xc * lax.rsqrt(jnp.mean(xc * xc, axis=-1, keepdims=True) + EPS)


def _sigmoid(x):
    return jax.nn.sigmoid(x)


def _split_bf16(x, parts):
    out = []
    r = x
    for _ in range(parts):
        p = r.astype(BF16)
        out.append(p)
        r = r - p.astype(F32)
    return out


def _dot_exact_rhs(x, m_bf16, parts=3):
    acc = None
    for p in _split_bf16(x, parts):
        t = jnp.dot(p, m_bf16, preferred_element_type=F32)
        acc = t if acc is None else acc + t
    return acc


def _dot_exact_lhs(m_bf16, x, parts=3):
    acc = None
    for p in _split_bf16(x, parts):
        t = jnp.dot(m_bf16, p, preferred_element_type=F32)
        acc = t if acc is None else acc + t
    return acc


def _mod_row(mod_ref, start, n_ctx, dec_seq):
    row = jnp.where(start < n_ctx, 0, 1 + (start - n_ctx) // dec_seq)
    return mod_ref[pl.ds(row, 1), :]


def _mod_kernel(cond_ref, w_ref, b_ref, out_ref):
    c = cond_ref[...]
    s = c * _sigmoid(c)
    out_ref[0] = jnp.dot(s, w_ref[0], preferred_element_type=F32) + b_ref[0]


def _modulation(cond, w_ada, b_ada):
    nl, d, n = w_ada.shape
    r = cond.shape[0]
    tn = 1024
    return pl.pallas_call(
        _mod_kernel,
        grid=(nl, n // tn),
        in_specs=[
            pl.BlockSpec((r, d), lambda l, j: (0, 0)),
            pl.BlockSpec((1, d, tn), lambda l, j: (l, 0, j)),
            pl.BlockSpec((1, 1, tn), lambda l, j: (l, 0, j)),
        ],
        out_specs=pl.BlockSpec((1, r, tn), lambda l, j: (l, 0, j)),
        out_shape=jax.ShapeDtypeStruct((nl, r, n), F32),
        name="adaln_modulation",
    )(cond, w_ada, b_ada.reshape(nl, 1, n))


def _assemble_kernel(xp_ref, xs_ref, pos_ref, out_ref, *, n_ctx_tiles):
    i = pl.program_id(0)

    @pl.when(i < n_ctx_tiles)
    def _():
        out_ref[...] = xp_ref[...]

    @pl.when(i >= n_ctx_tiles)
    def _():
        out_ref[...] = xs_ref[...] + pos_ref[...]


def _assemble_tokens(xp_flat, xs_flat, pos, tile):
    n_ctx, d = xp_flat.shape
    n_den = xs_flat.shape[0]
    ct = n_ctx // tile
    pt = pos.shape[0] // tile
    return pl.pallas_call(
        functools.partial(_assemble_kernel, n_ctx_tiles=ct),
        grid=((n_ctx + n_den) // tile,),
        in_specs=[
            pl.BlockSpec((tile, d), lambda i: (jnp.minimum(i, ct - 1), 0)),
            pl.BlockSpec((tile, d), lambda i: (jnp.maximum(i - ct, 0), 0)),
            pl.BlockSpec((tile, d), lambda i: (jnp.maximum(i - ct, 0) % pt, 0)),
        ],
        out_specs=pl.BlockSpec((tile, d), lambda i: (i, 0)),
        out_shape=jax.ShapeDtypeStruct((n_ctx + n_den, d), F32),
        name="assemble_tokens",
    )(xp_flat, xs_flat, pos)


def _inproj_kernel(x_ref, mod_ref, w_ref, out_ref, *, tile, n_ctx, dec_seq):
    m = _mod_row(mod_ref, pl.program_id(0) * tile, n_ctx, dec_seq)
    sh1 = m[:, 0:D_MODEL]
    sc1 = m[:, D_MODEL:2 * D_MODEL]
    h = _ln(x_ref[...]) * (1.0 + sc1) + sh1
    out_ref[...] = jnp.dot(h.astype(BF16), w_ref[...], preferred_element_type=F32)


def _in_projection(x, mod, w_in_r, n_ctx, dec_seq):
    nt, d = x.shape
    tile = TOK_TILE
    return pl.pallas_call(
        functools.partial(_inproj_kernel, tile=tile, n_ctx=n_ctx, dec_seq=dec_seq),
        grid=(nt // tile,),
        in_specs=[
            pl.BlockSpec((tile, d), lambda i: (i, 0)),
            pl.BlockSpec(mod.shape, lambda i: (0, 0)),
            pl.BlockSpec(w_in_r.shape, lambda i: (0, 0)),
        ],
        out_specs=pl.BlockSpec((tile, PROJ_W), lambda i: (i, 0)),
        out_shape=jax.ShapeDtypeStruct((nt, PROJ_W), F32),
        compiler_params=pltpu.CompilerParams(vmem_limit_bytes=VMEM_LIMIT),
        name="in_projection",
    )(x, mod, w_in_r)


def _group_norm(x, gavg):
    mean = _dot_exact_rhs(x, gavg)
    xc = x - mean
    var = _dot_exact_rhs(xc * xc, gavg)
    return xc * lax.rsqrt(var + EPS)


def _mixpre_kernel(qkv_ref, qkv_p_ref, qkv_n_ref, cv_ref, cv_p_ref, cv_n_ref, gm_ref, ba_ref,
                   convw_ref, convp_ref, dnw_ref, gvec_ref, gmp_ref, ws_ref, bsf_ref, gavg_ref,
                   ya_ref, yc_ref, q_ref, k_ref, v_ref, gates_ref, cbuf, qbuf,
                   *, tile, n_ctx_tiles, ctx_tps, den_tps):
    i = pl.program_id(0)
    pos = jnp.where(i < n_ctx_tiles, i % ctx_tps, (i - n_ctx_tiles) % den_tps)
    tps = jnp.where(i < n_ctx_tiles, ctx_tps, den_tps)
    first = pos == 0
    last = pos == tps - 1
    gavg = gavg_ref[...]

    def glu(p):
        return p[:, :CONV_W] * _sigmoid(p[:, CONV_W:])

    cbuf[0:CONV_HALO, :] = jnp.where(first, 0.0, glu(cv_p_ref[...]))
    cbuf[CONV_HALO:CONV_HALO + tile, :] = glu(cv_ref[...])
    cbuf[CONV_HALO + tile:2 * CONV_HALO + tile, :] = jnp.where(last, 0.0, glu(cv_n_ref[...]))
    conv_b = convp_ref[0:1, :]
    conv_g = convp_ref[1:2, :]
    conv_beta = convp_ref[2:3, :]
    rc = 64
    off = CONV_HALO - CONV_K // 2
    for c in range(tile // rc):
        acc = jnp.zeros((rc, CONV_W), F32)
        for k in range(CONV_K):
            acc = acc + cbuf[c * rc + off + k:c * rc + off + k + rc, :] * convw_ref[k:k + 1, :]
        y = _group_norm(acc + conv_b, gavg) * conv_g + conv_beta
        ya_ref[c * rc:(c + 1) * rc, :] = y * _sigmoid(y)

    qbuf[0:DN_HALO, :] = jnp.where(first, 0.0, qkv_p_ref[...])
    qbuf[DN_HALO:DN_HALO + tile, :] = qkv_ref[...]
    qbuf[DN_HALO + tile:2 * DN_HALO + tile, :] = jnp.where(last, 0.0, qkv_n_ref[...])
    rq = 32
    offq = DN_HALO - DN_SHORT_K // 2
    outs = (q_ref, k_ref, v_ref)
    for part in range(3):
        c0 = part * QK_W
        for c in range(tile // rq):
            acc = jnp.zeros((rq, QK_W), F32)
            for k in range(DN_SHORT_K):
                acc = acc + (qbuf[c * rq + offq + k:c * rq + offq + k + rq, c0:c0 + QK_W]
                             * dnw_ref[k:k + 1, c0:c0 + QK_W])
            a = acc * _sigmoid(acc)
            if part < 2:
                scale = DN_DK ** -0.5 if part == 0 else 1.0
                hs = []
                for h in range(DN_HEADS):
                    ah = a[:, h * DN_DK:(h + 1) * DN_DK]
                    nrm = lax.rsqrt(jnp.sum(ah * ah, axis=-1, keepdims=True) + EPS)
                    hs.append(ah * (nrm * scale))
                a = jnp.concatenate(hs, axis=-1)
            outs[part][c * rq:(c + 1) * rq, :] = a

    p = ba_ref[...]
    beta = _sigmoid(p)
    xg = p + gvec_ref[1:2, :]
    softplus = jnp.maximum(xg, 0.0) + jnp.log1p(jnp.exp(-jnp.abs(xg)))
    g = -jnp.exp(gvec_ref[0:1, :]) * softplus
    lane = lax.broadcasted_iota(jnp.int32, p.shape, 1)
    gates_ref[...] = jnp.where(lane < 2 * DN_HEADS, beta, g)

    pg = gm_ref[...]
    ge = pg * (0.5 * (1.0 + jnp.tanh(0.7978845608028654 * (pg + 0.044715 * (pg * pg * pg)))))
    u = ge[:, :GMLP_W]
    vn = _group_norm(ge[:, GMLP_W:], gavg) * gmp_ref[0:1, :] + gmp_ref[1:2, :]
    grp = lax.broadcasted_iota(jnp.int32, (GMLP_CHUNK, GMLP_W), 1) // GROUP_W
    for n in range(tile // GMLP_CHUNK):
        vchunk = vn[n * GMLP_CHUNK:(n + 1) * GMLP_CHUNK, :]
        sg = bsf_ref[...]
        for gi in range(GMLP_GROUPS):
            r = jnp.dot(ws_ref[gi], vchunk, preferred_element_type=F32)
            sg = sg + jnp.where(grp == gi, r, 0.0)
        yc_ref[n * GMLP_CHUNK:(n + 1) * GMLP_CHUNK, :] = u[n * GMLP_CHUNK:(n + 1) * GMLP_CHUNK, :] * sg


def _local_mixers(proj, lp, n_ctx_tiles, ctx_tps, den_tps):
    nt = proj.shape[0]
    tile = MIX_TILE
    n_tiles = nt // tile
    cpb = tile // CONV_HALO
    qpb = tile // DN_HALO
    n_cblk = nt // CONV_HALO
    n_qblk = nt // DN_HALO
    col = lambda c, w: c // w
    full = lambda a: pl.BlockSpec(a.shape, lambda i: (0,) * a.ndim)
    in_specs = [
        pl.BlockSpec((tile, 3 * QK_W), lambda i: (i, col(COL_QKV, 3 * QK_W))),
        pl.BlockSpec((DN_HALO, 3 * QK_W), lambda i: (jnp.maximum(i * qpb - 1, 0), 0)),
        pl.BlockSpec((DN_HALO, 3 * QK_W), lambda i: (jnp.minimum((i + 1) * qpb, n_qblk - 1), 0)),
        pl.BlockSpec((tile, 2 * CONV_W), lambda i: (i, col(COL_CONV, 2 * CONV_W))),
        pl.BlockSpec((CONV_HALO, 2 * CONV_W), lambda i: (jnp.maximum(i * cpb - 1, 0), col(COL_CONV, 2 * CONV_W))),
        pl.BlockSpec((CONV_HALO, 2 * CONV_W),
                     lambda i: (jnp.minimum((i + 1) * cpb, n_cblk - 1), col(COL_CONV, 2 * CONV_W))),
        pl.BlockSpec((tile, 2 * GMLP_W), lambda i: (i, col(COL_GM, 2 * GMLP_W))),
        pl.BlockSpec((tile, LANES), lambda i: (i, col(COL_BA, LANES))),
        full(lp["conv_dw"]), full(lp["conv_p"]), full(lp["dn_conv"]), full(lp["gvec"]),
        full(lp["gm_p"]), full(lp["gm_ws"]), full(lp["gm_bsf"]), full(lp["gavg"]),
    ]
    tok = lambda w: pl.BlockSpec((tile, w), lambda i: (i, 0))
    shp = lambda w: jax.ShapeDtypeStruct((nt, w), F32)
    return pl.pallas_call(
        functools.partial(_mixpre_kernel, tile=tile, n_ctx_tiles=n_ctx_tiles, ctx_tps=ctx_tps, den_tps=den_tps),
        grid=(n_tiles,),
        in_specs=in_specs,
        out_specs=[tok(CONV_W), tok(GMLP_W), tok(QK_W), tok(QK_W), tok(DN_W), tok(LANES)],
        out_shape=[shp(CONV_W), shp(GMLP_W), shp(QK_W), shp(QK_W), shp(DN_W), shp(LANES)],
        scratch_shapes=[pltpu.VMEM((tile + 2 * CONV_HALO, CONV_W), F32),
                        pltpu.VMEM((tile + 2 * DN_HALO, 3 * QK_W), F32)],
        compiler_params=pltpu.CompilerParams(vmem_limit_bytes=VMEM_LIMIT),
        name="local_mixers",
    )(proj, proj, proj, proj, proj, proj, proj, proj,
      lp["conv_dw"], lp["conv_p"], lp["dn_conv"], lp["gvec"], lp["gm_p"], lp["gm_ws"], lp["gm_bsf"], lp["gavg"])


DN_FIRST, DN_LAST_EMIT, DN_ZERO_INIT = 1, 2, 4


def _deltanet_kernel(tf_ref, tb_ref, flag_ref, s0i_ref, sfi_ref,
                     qf_ref, kf_ref, vf_ref, gf_ref, qb_ref, kb_ref, vb_ref, gb_ref, s0_ref,
                     of_ref, ob_ref, sfin_ref, s_ref, pq_scr, b_scr, o_scr, gt_scr, *, tile):
    flags = flag_ref[pl.program_id(0)]
    first = (flags & DN_FIRST) != 0
    zero_init = (flags & DN_ZERO_INIT) != 0
    c = DN_CHUNK
    n_chunks = tile // c
    upd = n_chunks * DN_HEADS

    @pl.when(first & zero_init)
    def _():
        s_ref[...] = jnp.zeros_like(s_ref)

    @pl.when(first & jnp.logical_not(zero_init))
    def _():
        for d in range(2):
            for h in range(DN_HEADS):
                s_ref[d * DN_HEADS + h] = s0_ref[0, 0, d, h]

    ii = lax.broadcasted_iota(jnp.int32, (c, c), 0)
    jj = lax.broadcasted_iota(jnp.int32, (c, c), 1)
    same = lambda s: (ii >> s) == (jj >> s)
    m8 = same(3)
    l16 = same(4) & ~m8
    l32 = same(5) & ~same(4)
    l64 = ~same(5)
    ti = lax.broadcasted_iota(jnp.int32, (tile, tile), 0)
    tj = lax.broadcasted_iota(jnp.int32, (tile, tile), 1)
    same_chunk = (ti // c) == (tj // c)
    nt_dims = (((1,), (1,)), ((), ()))
    tn_dims = (((0,), (0,)), ((), ()))
    dirs = (
        (qf_ref, kf_ref, vf_ref, gf_ref, of_ref, ii >= jj, ii > jj, ti >= tj, c - 1),
        (qb_ref, kb_ref, vb_ref, gb_ref, ob_ref, ii <= jj, ii < jj, ti <= tj, 0),
    )

    def gate_context(d):
        g_ref, tile_tri = dirs[d][3], dirs[d][7]
        gates = g_ref[...]
        blockcum = (same_chunk & tile_tri).astype(BF16)
        gc_t = _dot_exact_lhs(blockcum, gates)
        return gates, gc_t, gc_t.T

    gate_ctx = [gate_context(0), gate_context(1)]

    b16 = lambda x: x.astype(BF16)
    mm = lambda x, y: jnp.dot(x, y, preferred_element_type=F32)

    def prep(d):
        q_ref, k_ref, v_ref, _, _, incl, strict, _, last_row = dirs[d]
        gates, gc_t, gc_tt = gate_ctx[d]
        units = [(ci, h) for ci in range(n_chunks) for h in range(DN_HEADS)]
        chunk = []
        for ci in range(n_chunks):
            r0 = ci * c
            gc_c = gc_t[r0:r0 + c, :]
            glast = gc_t[r0 + last_row:r0 + last_row + 1, :]
            chunk.append((gc_c, jnp.exp(gc_c), jnp.exp(glast - gc_c), jnp.exp(glast)))
        lane_b = lambda h: d * DN_HEADS + h
        lane_g = lambda h: 2 * DN_HEADS + d * DN_HEADS + h
        col = lambda x, l: x[:, l:l + 1]
        beta = [col(gates[ci * c:(ci + 1) * c, :], lane_b(h)) for ci, h in units]
        eg = [col(chunk[ci][1], lane_g(h)) for ci, h in units]
        dmat = [jnp.where(incl, jnp.exp(jnp.minimum(
            col(chunk[ci][0], lane_g(h)) - gc_tt[lane_g(h):lane_g(h) + 1, ci * c:(ci + 1) * c], 0.0)), 0.0)
            for ci, h in units]
        q = [q_ref[ci * c:(ci + 1) * c, h * DN_DK:(h + 1) * DN_DK] for ci, h in units]
        k = [k_ref[ci * c:(ci + 1) * c, h * DN_DK:(h + 1) * DN_DK] for ci, h in units]
        v = [v_ref[ci * c:(ci + 1) * c, h * DN_DV:(h + 1) * DN_DV] for ci, h in units]
        k16 = [b16(x) for x in k]
        n = range(len(units))
        kk = [lax.dot_general(k16[i], k16[i], nt_dims, preferred_element_type=F32) for i in n]
        qk = [lax.dot_general(b16(q[i]), k16[i], nt_dims, preferred_element_type=F32) for i in n]
        a = [jnp.where(strict, beta[i] * kk[i] * dmat[i], 0.0) for i in n]
        dd = [jnp.where(m8, a[i], 0.0) for i in n]
        dd16 = [b16(x) for x in dd]
        d2 = [mm(dd16[i], dd16[i]) for i in n]
        d216 = [b16(x) for x in d2]
        d3 = [mm(dd16[i], d216[i]) for i in n]
        d4 = [mm(d216[i], d216[i]) for i in n]
        e = [d2[i] - dd[i] - d3[i] for i in n]
        t = [mm(b16(e[i]), b16(d4[i])) for i in n]
        e = [e[i] + d4[i] + t[i] for i in n]
        for sel in (l16, l32, l64):
            l = [jnp.where(sel, a[i], 0.0) for i in n]
            e16 = [b16(x) for x in e]
            ly = [l[i] + mm(e16[i], b16(l[i])) for i in n]
            z = [mm(b16(ly[i]), e16[i]) for i in n]
            e = [e[i] - ly[i] - z[i] for i in n]
        r = [jnp.concatenate([k[i] * (beta[i] * eg[i]), v[i] * beta[i]], axis=1) for i in n]
        wu = [r[i] + mm(b16(e[i]), b16(r[i])) for i in n]
        wu16 = [b16(x) for x in wu]
        qo = [mm(b16(qk[i] * dmat[i]), wu16[i]) for i in n]
        kd = [b16(k[i] * col(chunk[units[i][0]][2], lane_g(units[i][1]))) for i in n]
        pb = [lax.dot_general(kd[i], wu16[i], tn_dims, preferred_element_type=F32) for i in n]
        for i, (ci, h) in enumerate(units):
            u = d * upd + i
            pq_scr[u, 0:DN_DK, :] = b16(pb[i][:, :DN_DK])
            pq_scr[u, DN_DK:DN_DK + c, :] = b16(q[i] * eg[i] - qo[i][:, :DN_DK])
            b_scr[u] = pb[i][:, DN_DK:]
            o_scr[u] = qo[i][:, DN_DK:]
            gt_scr[u] = jnp.broadcast_to(col(chunk[ci][3], lane_g(h)), (1, LANES))

    def scan(step):
        chains = [(0, step, h) for h in range(DN_HEADS)] + [(1, n_chunks - 1 - step, h) for h in range(DN_HEADS)]
        s = [s_ref[d * DN_HEADS + h] for d, ci, h in chains]
        ps = [mm(pq_scr[d * upd + ci * DN_HEADS + h], b16(s[i])) for i, (d, ci, h) in enumerate(chains)]
        for i, (d, ci, h) in enumerate(chains):
            u = d * upd + ci * DN_HEADS + h
            s_ref[d * DN_HEADS + h] = s[i] * gt_scr[u] - ps[i][0:DN_DK, :] + b_scr[u]
            dirs[d][4][ci * c:(ci + 1) * c, h * DN_DV:(h + 1) * DN_DV] = ps[i][DN_DK:DN_DK + c, :] + o_scr[u]

    prep(0)
    prep(1)
    for step in range(n_chunks):
        scan(step)

    @pl.when((flags & DN_LAST_EMIT) != 0)
    def _():
        for d in range(2):
            for h in range(DN_HEADS):
                sfin_ref[0, d, h] = s_ref[d * DN_HEADS + h]


def _deltanet_schedule(batch, seq, dec_batch, dec_seq, tile):
    tf, tb, flags, s0i, sfi = [], [], [], [], []
    t0 = 0
    for n_seq, length, is_ctx in ((batch, seq, True), (dec_batch, dec_seq, False)):
        tps = length // tile
        for b in range(n_seq):
            for t in range(tps):
                tf.append(t0 + b * tps + t)
                tb.append(t0 + b * tps + tps - 1 - t)
                flags.append((DN_FIRST if t == 0 else 0)
                             | (DN_LAST_EMIT if (is_ctx and t == tps - 1) else 0)
                             | (DN_ZERO_INIT if is_ctx else 0))
                s0i.append(0 if is_ctx else b)
                sfi.append(b if is_ctx else batch - 1)
        t0 += n_seq * tps
    return [jnp.asarray(a, jnp.int32) for a in (tf, tb, flags, s0i, sfi)]


def _deltanet(q, k, v, gates, s0, layer, batch, seq, dec_batch, dec_seq):
    tile = MIX_TILE
    sched = _deltanet_schedule(batch, seq, dec_batch, dec_seq, tile)
    fwd = lambda w: pl.BlockSpec((tile, w), lambda s, tf, tb, fl, s0i, sfi: (tf[s], 0))
    bwd = lambda w: pl.BlockSpec((tile, w), lambda s, tf, tb, fl, s0i, sfi: (tb[s], 0))
    state_block = (2, DN_HEADS, DN_DK, DN_DV)
    units = 2 * (tile // DN_CHUNK) * DN_HEADS
    grid_spec = pltpu.PrefetchScalarGridSpec(
        num_scalar_prefetch=len(sched),
        grid=(sched[0].shape[0],),
        in_specs=[fwd(QK_W), fwd(QK_W), fwd(DN_W), fwd(LANES), bwd(QK_W), bwd(QK_W), bwd(DN_W), bwd(LANES),
                  pl.BlockSpec((1, 1) + state_block,
                               lambda s, tf, tb, fl, s0i, sfi: (s0i[s], layer, 0, 0, 0, 0))],
        out_specs=[fwd(DN_W), bwd(DN_W),
                   pl.BlockSpec((1,) + state_block, lambda s, tf, tb, fl, s0i, sfi: (sfi[s], 0, 0, 0, 0))],
        scratch_shapes=[pltpu.VMEM((2 * DN_HEADS, DN_DK, DN_DV), F32),
                        pltpu.VMEM((units, DN_DK + DN_CHUNK, DN_DV), BF16),
                        pltpu.VMEM((units, DN_DK, DN_DV), F32),
                        pltpu.VMEM((units, DN_CHUNK, DN_DV), F32),
                        pltpu.VMEM((units, 1, LANES), F32)],
    )
    return pl.pallas_call(
        functools.partial(_deltanet_kernel, tile=tile),
        grid_spec=grid_spec,
        out_shape=[jax.ShapeDtypeStruct((q.shape[0], DN_W), F32)] * 2
        + [jax.ShapeDtypeStruct((batch,) + state_block, F32)],
        compiler_params=pltpu.CompilerParams(dimension_semantics=("arbitrary",), vmem_limit_bytes=VMEM_LIMIT),
        name="deltanet_scan",
    )(*sched, q, k, v, gates, q, k, v, gates, s0)


def _outproj_kernel(ya_ref, of_ref, ob_ref, z_ref, yc_ref, x_ref, mod_ref, wout_ref, vec_ref, dng_ref,
                    wr_hi_ref, wr_lo_ref, br_ref, x1_ref, h_ref, route_ref, cnt_ref, run_ref,
                    *, tile, n_ctx, dec_seq):
    m = _mod_row(mod_ref, pl.program_id(0) * tile, n_ctx, dec_seq)
    gt1 = m[:, 2 * D_MODEL:3 * D_MODEL]
    sh2 = m[:, 3 * D_MODEL:4 * D_MODEL]
    sc2 = m[:, 4 * D_MODEL:5 * D_MODEL]
    o = of_ref[...] + ob_ref[...]
    z = z_ref[...]
    zz = z * _sigmoid(z)
    hs = []
    for h in range(DN_HEADS):
        oh = o[:, h * DN_DV:(h + 1) * DN_DV]
        on = oh * lax.rsqrt(jnp.mean(oh * oh, axis=-1, keepdims=True) + EPS) * dng_ref[...]
        hs.append(on * zz[:, h * DN_DV:(h + 1) * DN_DV])
    ycat = jnp.concatenate([ya_ref[...]] + hs + [yc_ref[...]], axis=-1).astype(BF16)
    y = jnp.dot(ycat, wout_ref[...], preferred_element_type=F32)
    x1 = _ln(DEEPNORM_ALPHA * x_ref[...] + gt1 * y) * vec_ref[0:1, :] + vec_ref[1:2, :]
    x1_ref[...] = x1
    hf = _ln(x1) * (1.0 + sc2) + sh2
    h_hi = hf.astype(BF16)
    h_lo = (hf - h_hi.astype(F32)).astype(BF16)
    h_ref[...] = h_hi
    logits = (jnp.dot(h_hi, wr_hi_ref[...], preferred_element_type=F32)
              + jnp.dot(h_lo, wr_hi_ref[...], preferred_element_type=F32)
              + jnp.dot(h_hi, wr_lo_ref[...], preferred_element_type=F32)
              + br_ref[...])
    lane = lax.broadcasted_iota(jnp.int32, logits.shape, 1).astype(F32)
    vals, idxs = [], []
    for _ in range(TOP_K):
        m = jnp.max(logits, axis=-1, keepdims=True)
        idx = jnp.min(jnp.where(logits == m, lane, float(LANES)), axis=-1, keepdims=True)
        vals.append(m)
        idxs.append(idx)
        logits = jnp.where(lane == idx, -jnp.inf, logits)
    es = [jnp.exp(v - vals[0]) for v in vals]
    den = (es[0] + es[1]) + (es[2] + es[3])
    @pl.when(pl.program_id(0) == 0)
    def _():
        run_ref[...] = jnp.zeros_like(run_ref)

    onehot = jnp.zeros(logits.shape, F32)
    for idx in idxs:
        onehot = onehot + (lane == idx).astype(F32)
    ri = lax.broadcasted_iota(jnp.int32, (tile, tile), 0)
    rj = lax.broadcasted_iota(jnp.int32, (tile, tile), 1)
    before = jnp.dot((ri > rj).astype(BF16), onehot.astype(BF16), preferred_element_type=F32) + run_ref[...]
    run_ref[...] = run_ref[...] + jnp.sum(onehot, axis=0, keepdims=True)
    cnt_ref[...] = jnp.broadcast_to(run_ref[...], cnt_ref.shape)
    route = jnp.zeros(logits.shape, F32)
    for j in range(TOP_K):
        rank = jnp.sum(jnp.where(lane == idxs[j], before, 0.0), axis=-1, keepdims=True)
        route = jnp.where(lane == float(j), idxs[j], route)
        route = jnp.where(lane == float(TOP_K + j), es[j] / den, route)
        route = jnp.where(lane == float(2 * TOP_K + j), rank, route)
    route_ref[...] = route


def _out_projection(ya, o_f, o_b, proj, yc, x, mod, lp, n_ctx, dec_seq):
    nt, d = x.shape
    tile = TOK_TILE
    tok = lambda w: pl.BlockSpec((tile, w), lambda i: (i, 0))
    full = lambda a: pl.BlockSpec(a.shape, lambda i: (0,) * a.ndim)
    return pl.pallas_call(
        functools.partial(_outproj_kernel, tile=tile, n_ctx=n_ctx, dec_seq=dec_seq),
        grid=(nt // tile,),
        in_specs=[tok(CONV_W), tok(DN_W), tok(DN_W),
                  pl.BlockSpec((tile, DN_W), lambda i: (i, COL_Z // DN_W)),
                  tok(GMLP_W), tok(d), full(mod), full(lp["w_out"]), full(lp["ln1"]), full(lp["dn_norm_g"]),
                  full(lp["wr_hi"]), full(lp["wr_lo"]), full(lp["b_router"])],
        out_specs=[tok(d), tok(d), tok(LANES), pl.BlockSpec((8, LANES), lambda i: (0, 0))],
        out_shape=[jax.ShapeDtypeStruct((nt, d), F32), jax.ShapeDtypeStruct((nt, d), BF16),
                   jax.ShapeDtypeStruct((nt, LANES), F32), jax.ShapeDtypeStruct((8, LANES), F32)],
        scratch_shapes=[pltpu.VMEM((1, LANES), F32)],
        compiler_params=pltpu.CompilerParams(dimension_semantics=("arbitrary",), vmem_limit_bytes=VMEM_LIMIT),
        name="out_projection_router",
    )(ya, o_f, o_b, proj, yc, x, mod, lp["w_out"], lp["ln1"], lp["dn_norm_g"],
      lp["wr_hi"], lp["wr_lo"], lp["b_router"])


def _expert_kernel(be_ref, nu_ref, x_ref, wgu_ref, bgu_ref, wd_ref, bd_ref, out_ref, wgu16, wd16):
    i = pl.program_id(0)

    @pl.when((i == 0) | (be_ref[i] != be_ref[jnp.maximum(i - 1, 0)]))
    def _():
        wgu16[...] = wgu_ref[0, 0].astype(BF16)
        wd16[...] = wd_ref[0, 0].astype(BF16)

    @pl.when(i < nu_ref[0])
    def _():
        gu = jnp.dot(x_ref[...], wgu16[...], preferred_element_type=F32) + bgu_ref[0, 0]
        gate = jnp.minimum(gu[:, :D_FF], SWIGLU_LIMIT)
        up = jnp.clip(gu[:, D_FF:], -SWIGLU_LIMIT, SWIGLU_LIMIT)
        act = gate * _sigmoid(SWIGLU_ALPHA * gate)
        hmid = ((up + 1.0) * act).astype(BF16)
        y = jnp.dot(hmid, wd16[...], preferred_element_type=F32) + bd_ref[0, 0]
        out_ref[...] = y.astype(out_ref.dtype)

    @pl.when(i >= nu_ref[0])
    def _():
        out_ref[...] = jnp.zeros_like(out_ref)


def _experts(xg, block_e, n_used, w_gu, b_gu, w_down, b_down, layer):
    m_pad, d = xg.shape
    bm = MOE_BM
    grid_spec = pltpu.PrefetchScalarGridSpec(
        num_scalar_prefetch=2,
        grid=(m_pad // bm,),
        in_specs=[
            pl.BlockSpec((bm, d), lambda i, be, nu: (i, 0)),
            pl.BlockSpec((1, 1, d, 2 * D_FF), lambda i, be, nu: (layer, be[i], 0, 0)),
            pl.BlockSpec((1, 1, 1, 2 * D_FF), lambda i, be, nu: (layer, be[i], 0, 0)),
            pl.BlockSpec((1, 1, D_FF, d), lambda i, be, nu: (layer, be[i], 0, 0)),
            pl.BlockSpec((1, 1, 1, d), lambda i, be, nu: (layer, be[i], 0, 0)),
        ],
        out_specs=pl.BlockSpec((bm, d), lambda i, be, nu: (i, 0)),
        scratch_shapes=[pltpu.VMEM((d, 2 * D_FF), BF16), pltpu.VMEM((D_FF, d), BF16)],
    )
    return pl.pallas_call(
        _expert_kernel,
        grid_spec=grid_spec,
        out_shape=jax.ShapeDtypeStruct((m_pad, d), BF16),
        compiler_params=pltpu.CompilerParams(dimension_semantics=("arbitrary",), vmem_limit_bytes=VMEM_LIMIT),
        name="moe_experts",
    )(block_e, n_used, xg, w_gu, b_gu, w_down, b_down)


def _combine_kernel(yg_ref, route_ref, x1_ref, mod_ref, vec_ref, out_ref, *, tile, n_ctx, dec_seq):
    m = _mod_row(mod_ref, pl.program_id(0) * tile, n_ctx, dec_seq)
    gt2 = m[:, 5 * D_MODEL:6 * D_MODEL]
    gate = lambda j: route_ref[:, TOP_K + j:TOP_K + j + 1]
    ye = lambda j: yg_ref[j].astype(F32) * gate(j)
    y = (ye(0) + ye(1)) + (ye(2) + ye(3))
    out_ref[...] = _ln(DEEPNORM_ALPHA * x1_ref[...] + gt2 * y) * vec_ref[0:1, :] + vec_ref[1:2, :]


def _combine(yg, route, x1, mod, ln2, n_ctx, dec_seq):
    nt, d = x1.shape
    tile = TOK_TILE
    return pl.pallas_call(
        functools.partial(_combine_kernel, tile=tile, n_ctx=n_ctx, dec_seq=dec_seq),
        grid=(nt // tile,),
        in_specs=[pl.BlockSpec((TOP_K, tile, d), lambda i: (0, i, 0)),
                  pl.BlockSpec((tile, LANES), lambda i: (i, 0)),
                  pl.BlockSpec((tile, d), lambda i: (i, 0)),
                  pl.BlockSpec(mod.shape, lambda i: (0, 0)),
                  pl.BlockSpec(ln2.shape, lambda i: (0, 0))],
        out_specs=pl.BlockSpec((tile, d), lambda i: (i, 0)),
        out_shape=jax.ShapeDtypeStruct((nt, d), F32),
        compiler_params=pltpu.CompilerParams(vmem_limit_bytes=VMEM_LIMIT),
        name="moe_combine_ln",
    )(yg, route, x1, mod, ln2)


def _route(route, counts):
    nt = route.shape[0]
    nk = nt * TOP_K
    bm = MOE_BM
    expert = route[:, 0:TOP_K].astype(jnp.int32)
    rank = route[:, 2 * TOP_K:3 * TOP_K].astype(jnp.int32)
    counts = counts[0, :N_EXPERTS].astype(jnp.int32)
    padded = (counts + bm - 1) // bm * bm
    pad_end = jnp.cumsum(padded)
    pad_start = pad_end - padded
    grp_start = jnp.cumsum(counts) - counts
    pair_slot = pad_start[expert] + rank
    tok = jnp.arange(nk, dtype=jnp.int32) // TOP_K
    _, sorted_tok = lax.sort((pair_slot.reshape(-1), tok), num_keys=1)
    n_blocks = nk // bm + N_EXPERTS
    blk_start = jnp.arange(n_blocks, dtype=jnp.int32) * bm
    block_e = jnp.minimum(jnp.sum((pad_end[None, :] <= blk_start[:, None]).astype(jnp.int32), axis=1),
                          N_EXPERTS - 1)
    slot_e = jnp.repeat(block_e, bm)
    slot = jnp.arange(n_blocks * bm, dtype=jnp.int32)
    j = slot - pad_start[slot_e]
    src = jnp.clip(grp_start[slot_e] + j, 0, nk - 1)
    slot_tok = jnp.where(j < counts[slot_e], sorted_tok[src], slot % nt)
    n_used = (pad_end[-1] // bm).astype(jnp.int32).reshape(1)
    return slot_tok, pair_slot, block_e.astype(jnp.int32), n_used


def _grid_pos_embed(t, d):
    rows = t // GRID_W
    r, col = jnp.meshgrid(jnp.arange(rows), jnp.arange(GRID_W), indexing="ij")
    r = r.reshape(-1).astype(F32)[:, None]
    col = col.reshape(-1).astype(F32)[:, None]
    n_freq = d // 4
    omega = 1.0 / (POS_BASE ** (jnp.arange(n_freq, dtype=F32) / n_freq))
    return jnp.concatenate([jnp.sin(r * omega), jnp.cos(r * omega),
                            jnp.sin(col * omega), jnp.cos(col * omega)], axis=-1)


def _pad_lanes(a, offset):
    return jnp.zeros((1, LANES), F32).at[0, offset:offset + a.shape[0]].set(a.astype(F32))


def _layer_params(l, w_in, conv_dw, conv_b, conv_ln_g, conv_ln_b, dn_conv, dn_a_log, dn_dt_bias, dn_norm_g,
                  gm_ln_g, gm_ln_b, gm_ws, gm_bs, w_out, ln1_g, ln1_b, ln2_g, ln2_b, w_router, b_router):
    wi = w_in[l]
    c_conv, c_qkv, c_z, c_ba = 0, 2 * CONV_W, 2 * CONV_W + 2 * QK_W + DN_W, 2 * CONV_W + 2 * QK_W + 2 * DN_W
    c_gm = c_ba + 4 * DN_HEADS
    w_in_r = jnp.concatenate([
        wi[:, c_qkv:c_z], wi[:, c_conv:c_qkv], wi[:, c_z:c_ba], wi[:, c_gm:],
        wi[:, c_ba:c_gm], jnp.zeros((D_MODEL, LANES - 4 * DN_HEADS), F32)], axis=1).astype(BF16)
    grp = jnp.arange(CONV_W) // GROUP_W
    gavg = (grp[:, None] == grp[None, :]).astype(BF16) * (1.0 / GROUP_W)
    gvec = jnp.concatenate([_pad_lanes(dn_a_log[l].reshape(-1), 2 * DN_HEADS),
                            _pad_lanes(dn_dt_bias[l].reshape(-1), 2 * DN_HEADS)], axis=0)
    wr = jnp.zeros((D_MODEL, LANES), F32).at[:, :N_EXPERTS].set(w_router[l])
    wr_hi = wr.astype(BF16)
    wr_lo = (wr - wr_hi.astype(F32)).astype(BF16)
    return {
        "w_in_r": w_in_r,
        "conv_dw": conv_dw[l],
        "conv_p": jnp.stack([conv_b[l], conv_ln_g[l], conv_ln_b[l]]),
        "dn_conv": dn_conv[l],
        "gvec": gvec,
        "gm_p": jnp.stack([gm_ln_g[l], gm_ln_b[l]]),
        "gm_ws": gm_ws[l],
        "gm_bsf": jnp.repeat(jnp.transpose(gm_bs[l]), GROUP_W, axis=1),
        "gavg": gavg,
        "w_out": w_out[l].astype(BF16),
        "ln1": jnp.stack([ln1_g[l], ln1_b[l]]),
        "ln2": jnp.stack([ln2_g[l], ln2_b[l]]),
        "dn_norm_g": dn_norm_g[l].reshape(1, DN_DV),
        "wr_hi": wr_hi,
        "wr_lo": wr_lo,
        "b_router": jnp.full((1, LANES), -1e30, F32).at[0, :N_EXPERTS].set(b_router[l]),
    }


def kernel(x_prompt, x_sample, state_delta, c, c_ctx, w_ada, b_ada, w_in, conv_dw, conv_b, conv_ln_g, conv_ln_b, dn_conv, dn_a_log, dn_dt_bias, dn_norm_g, gm_ln_g, gm_ln_b, gm_ws, gm_bs, w_out, ln1_g, ln1_b, ln2_g, ln2_b, w_router, b_router, w_gu, b_gu, w_down, b_down):
    batch, seq, d = x_prompt.shape
    dec_batch, dec_seq, _ = x_sample.shape
    n_ctx = batch * seq
    n_den = dec_batch * dec_seq
    depth = w_in.shape[0]

    cond = jnp.zeros((16, d), F32).at[0].set(c_ctx).at[1:1 + dec_batch].set(c)
    mod_all = _modulation(cond, w_ada, b_ada)
    x = _assemble_tokens(x_prompt.reshape(n_ctx, d), x_sample.reshape(n_den, d),
                         _grid_pos_embed(dec_seq, d), TOK_TILE)

    b_gu_r = b_gu.reshape(depth, N_EXPERTS, 1, 2 * D_FF)
    b_down_r = b_down.reshape(depth, N_EXPERTS, 1, d)

    ctx_states = []
    for l in range(depth):
        lp = _layer_params(l, w_in, conv_dw, conv_b, conv_ln_g, conv_ln_b, dn_conv, dn_a_log, dn_dt_bias,
                           dn_norm_g, gm_ln_g, gm_ln_b, gm_ws, gm_bs, w_out, ln1_g, ln1_b, ln2_g, ln2_b,
                           w_router, b_router)
        mod = mod_all[l]
        proj = _in_projection(x, mod, lp["w_in_r"], n_ctx, dec_seq)
        ya, yc, q, k, v, gates = _local_mixers(proj, lp, n_ctx // MIX_TILE, seq // MIX_TILE, dec_seq // MIX_TILE)
        o_f, o_b, s_fin = _deltanet(q, k, v, gates, state_delta, l, batch, seq, dec_batch, dec_seq)
        ctx_states.append(s_fin)
        x1, hffn, route, counts = _out_projection(ya, o_f, o_b, proj, yc, x, mod, lp, n_ctx, dec_seq)
        slot_tok, pair_slot, block_e, n_used = _route(route, counts)
        xg = hffn.at[slot_tok].get(mode="promise_in_bounds")
        yb = _experts(xg, block_e, n_used, w_gu, b_gu_r, w_down, b_down_r, l)
        yg = yb.at[jnp.transpose(pair_slot)].get(mode="promise_in_bounds")
        x = _combine(yg, route, x1, mod, lp["ln2"], n_ctx, dec_seq)

    new_state = jnp.stack(ctx_states, axis=1).astype(x_prompt.dtype)
    return (x[:n_ctx].reshape(batch, seq, d), x[n_ctx:].reshape(dec_batch, dec_seq, d), new_state)
```

```python
import functools

import jax
import jax.numpy as jnp
from jax import lax
from jax.experimental import pallas as pl
from jax.experimental.pallas import tpu as pltpu

F32 = jnp.float32
BF16 = jnp.bfloat16

D_MODEL = 1024
DEPTH = 2
GRID_W = 64
POS_BASE = 10000.0
CONV_W = 256
CONV_K = 31
CONV_HALO = 16
DN_HEADS = 4
DN_DK = 128
DN_DV = 128
QK_W = DN_HEADS * DN_DK
DN_W = DN_HEADS * DN_DV
DN_SHORT_K = 5
DN_HALO = 8
DN_CHUNK = 64
GMLP_W = 256
GMLP_GROUPS = 4
GMLP_CHUNK = 128
GROUP_W = 64
N_EXPERTS = 32
TOP_K = 4
D_FF = D_MODEL
SWIGLU_ALPHA = 1.702
SWIGLU_LIMIT = 7.0
DEEPNORM_ALPHA = (2 * DEPTH) ** 0.25
EPS = 1e-6

LANES = 128
SUBLANES = 8
COL_QKV = 0
COL_CONV = 2 * QK_W + DN_W
COL_Z = COL_CONV + 2 * CONV_W
COL_GM = COL_Z + DN_W
COL_BA = COL_GM + 2 * GMLP_W
PROJ_W = COL_BA + LANES

MIX_TILE = 256
TOK_TILE = 512
MOE_BM = 512
VMEM_LIMIT = 56 * 1024 * 1024


def _ln(x):
    mu = jnp.mean(x, axis=-1, keepdims=True)
    xc = x - mu
    return xc * lax.rsqrt(jnp.mean(xc * xc, axis=-1, keepdims=True) + EPS)


def _sigmoid(x):
    return jax.nn.sigmoid(x)


def _split_bf16(x, parts):
    out = []
    r = x
    for _ in range(parts):
        p = r.astype(BF16)
        out.append(p)
        r = r - p.astype(F32)
    return out


def _dot_exact_rhs(x, m_bf16, parts=3):
    acc = None
    for p in _split_bf16(x, parts):
        t = jnp.dot(p, m_bf16, preferred_element_type=F32)
        acc = t if acc is None else acc + t
    return acc


def _dot_exact_lhs(m_bf16, x, parts=3):
    acc = None
    for p in _split_bf16(x, parts):
        t = jnp.dot(m_bf16, p, preferred_element_type=F32)
        acc = t if acc is None else acc + t
    return acc


def _mod_row(mod_ref, start, n_ctx, dec_seq):
    row = jnp.where(start < n_ctx, 0, 1 + (start - n_ctx) // dec_seq)
    return mod_ref[pl.ds(row, 1), :]


def _mod_kernel(cond_ref, w_ref, b_ref, out_ref):
    c = cond_ref[...]
    s = c * _sigmoid(c)
    out_ref[0] = jnp.dot(s, w_ref[0], preferred_element_type=F32) + b_ref[0]


def _modulation(cond, w_ada, b_ada):
    nl, d, n = w_ada.shape
    r = cond.shape[0]
    tn = 1024
    return pl.pallas_call(
        _mod_kernel,
        grid=(nl, n // tn),
        in_specs=[
            pl.BlockSpec((r, d), lambda l, j: (0, 0)),
            pl.BlockSpec((1, d, tn), lambda l, j: (l, 0, j)),
            pl.BlockSpec((1, 1, tn), lambda l, j: (l, 0, j)),
        ],
        out_specs=pl.BlockSpec((1, r, tn), lambda l, j: (l, 0, j)),
        out_shape=jax.ShapeDtypeStruct((nl, r, n), F32),
        name="adaln_modulation",
    )(cond, w_ada, b_ada.reshape(nl, 1, n))


def _assemble_kernel(xp_ref, xs_ref, pos_ref, out_ref, *, n_ctx_tiles):
    i = pl.program_id(0)

    @pl.when(i < n_ctx_tiles)
    def _():
        out_ref[...] = xp_ref[...]

    @pl.when(i >= n_ctx_tiles)
    def _():
        out_ref[...] = xs_ref[...] + pos_ref[...]


def _assemble_tokens(xp_flat, xs_flat, pos, tile, n_ctx, den_row0, n_den):
    d = xs_flat.shape[1]
    ct = n_ctx // tile
    st = den_row0 // tile
    pt = pos.shape[0] // tile
    return pl.pallas_call(
        functools.partial(_assemble_kernel, n_ctx_tiles=ct),
        grid=((n_ctx + n_den) // tile,),
        in_specs=[
            pl.BlockSpec((tile, d), lambda i: (jnp.minimum(i, max(ct - 1, 0)), 0)),
            pl.BlockSpec((tile, d), lambda i: (st + jnp.maximum(i - ct, 0), 0)),
            pl.BlockSpec((tile, d), lambda i: (jnp.maximum(i - ct, 0) % pt, 0)),
        ],
        out_specs=pl.BlockSpec((tile, d), lambda i: (i, 0)),
        out_shape=jax.ShapeDtypeStruct((n_ctx + n_den, d), F32),
        name="assemble_tokens",
    )(xp_flat, xs_flat, pos)


def _inproj_kernel(x_ref, mod_ref, w_ref, out_ref, *, tile, n_ctx, dec_seq):
    m = _mod_row(mod_ref, pl.program_id(0) * tile, n_ctx, dec_seq)
    sh1 = m[:, 0:D_MODEL]
    sc1 = m[:, D_MODEL:2 * D_MODEL]
    h = _ln(x_ref[...]) * (1.0 + sc1) + sh1
    out_ref[...] = jnp.dot(h.astype(BF16), w_ref[...], preferred_element_type=F32)


def _in_projection(x, mod, w_in_r, n_ctx, dec_seq):
    nt, d = x.shape
    tile = TOK_TILE
    return pl.pallas_call(
        functools.partial(_inproj_kernel, tile=tile, n_ctx=n_ctx, dec_seq=dec_seq),
        grid=(nt // tile,),
        in_specs=[
            pl.BlockSpec((tile, d), lambda i: (i, 0)),
            pl.BlockSpec(mod.shape, lambda i: (0, 0)),
            pl.BlockSpec(w_in_r.shape, lambda i: (0, 0)),
        ],
        out_specs=pl.BlockSpec((tile, PROJ_W), lambda i: (i, 0)),
        out_shape=jax.ShapeDtypeStruct((nt, PROJ_W), F32),
        compiler_params=pltpu.CompilerParams(vmem_limit_bytes=VMEM_LIMIT),
        name="in_projection",
    )(x, mod, w_in_r)


def _group_norm(x, gavg):
    mean = _dot_exact_rhs(x, gavg, parts=2)
    xc = x - mean
    var = _dot_exact_rhs(xc * xc, gavg, parts=2)
    return xc * lax.rsqrt(var + EPS)


def _mixpre_kernel(qkv_ref, qkv_p_ref, qkv_n_ref, cv_ref, cv_p_ref, cv_n_ref, gm_ref, ba_ref,
                   convw_ref, convp_ref, dnw_ref, gvec_ref, gmp_ref, ws_ref, bsf_ref, gavg_ref,
                   ya_ref, yc_ref, q_ref, k_ref, v_ref, gates_ref, cbuf, cshift, qbuf,
                   *, tile, n_ctx_tiles, ctx_tps, den_tps):
    i = pl.program_id(0)
    pos = jnp.where(i < n_ctx_tiles, i % ctx_tps, (i - n_ctx_tiles) % den_tps)
    tps = jnp.where(i < n_ctx_tiles, ctx_tps, den_tps)
    first = pos == 0
    last = pos == tps - 1
    gavg = gavg_ref[...]

    def glu(p):
        return p[:, :CONV_W] * _sigmoid(p[:, CONV_W:])

    cbuf[0:CONV_HALO, :] = jnp.where(first, 0.0, glu(cv_p_ref[...]))
    cbuf[CONV_HALO:CONV_HALO + tile, :] = glu(cv_ref[...])
    cbuf[CONV_HALO + tile:2 * CONV_HALO + tile, :] = jnp.where(last, 0.0, glu(cv_n_ref[...]))
    conv_b = convp_ref[0:1, :]
    conv_g = convp_ref[1:2, :]
    conv_beta = convp_ref[2:3, :]
    rc = 64
    off = CONV_HALO - CONV_K // 2
    span = tile + 2 * CONV_HALO - SUBLANES
    for b in range(1, SUBLANES):
        cshift[b - 1] = cbuf[b:b + span, :]
    for c in range(tile // rc):
        acc = jnp.zeros((rc, CONV_W), F32)
        for k in range(CONV_K):
            a, b = divmod(off + k, SUBLANES)
            r0 = c * rc + a * SUBLANES
            win = cbuf[r0:r0 + rc, :] if b == 0 else cshift[b - 1, r0:r0 + rc, :]
            acc = acc + win * convw_ref[k:k + 1, :]
        y = _group_norm(acc + conv_b, gavg) * conv_g + conv_beta
        ya_ref[c * rc:(c + 1) * rc, :] = y * _sigmoid(y)

    qbuf[0:DN_HALO, :] = jnp.where(first, 0.0, qkv_p_ref[...])
    qbuf[DN_HALO:DN_HALO + tile, :] = qkv_ref[...]
    qbuf[DN_HALO + tile:2 * DN_HALO + tile, :] = jnp.where(last, 0.0, qkv_n_ref[...])
    rq = 32
    offq = DN_HALO - DN_SHORT_K // 2
    outs = (q_ref, k_ref, v_ref)
    for part in range(3):
        c0 = part * QK_W
        for c in range(tile // rq):
            acc = jnp.zeros((rq, QK_W), F32)
            for k in range(DN_SHORT_K):
                acc = acc + (qbuf[c * rq + offq + k:c * rq + offq + k + rq, c0:c0 + QK_W]
                             * dnw_ref[k:k + 1, c0:c0 + QK_W])
            a = acc * _sigmoid(acc)
            if part < 2:
                scale = DN_DK ** -0.5 if part == 0 else 1.0
                hs = []
                for h in range(DN_HEADS):
                    ah = a[:, h * DN_DK:(h + 1) * DN_DK]
                    nrm = lax.rsqrt(jnp.sum(ah * ah, axis=-1, keepdims=True) + EPS)
                    hs.append(ah * (nrm * scale))
                a = jnp.concatenate(hs, axis=-1)
            outs[part][c * rq:(c + 1) * rq, :] = a

    p = ba_ref[...]
    beta = _sigmoid(p)
    xg = p + gvec_ref[1:2, :]
    softplus = jnp.maximum(xg, 0.0) + jnp.log1p(jnp.exp(-jnp.abs(xg)))
    g = -jnp.exp(gvec_ref[0:1, :]) * softplus
    lane = lax.broadcasted_iota(jnp.int32, p.shape, 1)
    gates_ref[...] = jnp.where(lane < 2 * DN_HEADS, beta, g)

    pg = gm_ref[...]
    ge = pg * (0.5 * (1.0 + jnp.tanh(0.7978845608028654 * (pg + 0.044715 * (pg * pg * pg)))))
    u = ge[:, :GMLP_W]
    vn = _group_norm(ge[:, GMLP_W:], gavg) * gmp_ref[0:1, :] + gmp_ref[1:2, :]
    grp = lax.broadcasted_iota(jnp.int32, (GMLP_CHUNK, GMLP_W), 1) // GROUP_W
    for n in range(tile // GMLP_CHUNK):
        vchunk = vn[n * GMLP_CHUNK:(n + 1) * GMLP_CHUNK, :]
        sg = bsf_ref[...]
        for gi in range(GMLP_GROUPS):
            r = jnp.dot(ws_ref[gi], vchunk, preferred_element_type=F32)
            sg = sg + jnp.where(grp == gi, r, 0.0)
        yc_ref[n * GMLP_CHUNK:(n + 1) * GMLP_CHUNK, :] = u[n * GMLP_CHUNK:(n + 1) * GMLP_CHUNK, :] * sg


def _local_mixers(proj, lp, n_ctx_tiles, ctx_tps, den_tps):
    nt = proj.shape[0]
    tile = MIX_TILE
    n_tiles = nt // tile
    cpb = tile // CONV_HALO
    qpb = tile // DN_HALO
    n_cblk = nt // CONV_HALO
    n_qblk = nt // DN_HALO
    col = lambda c, w: c // w
    full = lambda a: pl.BlockSpec(a.shape, lambda i: (0,) * a.ndim)
    in_specs = [
        pl.BlockSpec((tile, 3 * QK_W), lambda i: (i, col(COL_QKV, 3 * QK_W))),
        pl.BlockSpec((DN_HALO, 3 * QK_W), lambda i: (jnp.maximum(i * qpb - 1, 0), 0)),
        pl.BlockSpec((DN_HALO, 3 * QK_W), lambda i: (jnp.minimum((i + 1) * qpb, n_qblk - 1), 0)),
        pl.BlockSpec((tile, 2 * CONV_W), lambda i: (i, col(COL_CONV, 2 * CONV_W))),
        pl.BlockSpec((CONV_HALO, 2 * CONV_W), lambda i: (jnp.maximum(i * cpb - 1, 0), col(COL_CONV, 2 * CONV_W))),
        pl.BlockSpec((CONV_HALO, 2 * CONV_W),
                     lambda i: (jnp.minimum((i + 1) * cpb, n_cblk - 1), col(COL_CONV, 2 * CONV_W))),
        pl.BlockSpec((tile, 2 * GMLP_W), lambda i: (i, col(COL_GM, 2 * GMLP_W))),
        pl.BlockSpec((tile, LANES), lambda i: (i, col(COL_BA, LANES))),
        full(lp["conv_dw"]), full(lp["conv_p"]), full(lp["dn_conv"]), full(lp["gvec"]),
        full(lp["gm_p"]), full(lp["gm_ws"]), full(lp["gm_bsf"]), full(lp["gavg"]),
    ]
    tok = lambda w: pl.BlockSpec((tile, w), lambda i: (i, 0))
    shp = lambda w: jax.ShapeDtypeStruct((nt, w), F32)
    return pl.pallas_call(
        functools.partial(_mixpre_kernel, tile=tile, n_ctx_tiles=n_ctx_tiles, ctx_tps=ctx_tps, den_tps=den_tps),
        grid=(n_tiles,),
        in_specs=in_specs,
        out_specs=[tok(CONV_W), tok(GMLP_W), tok(QK_W), tok(QK_W), tok(DN_W), tok(LANES)],
        out_shape=[shp(CONV_W), shp(GMLP_W), shp(QK_W), shp(QK_W), shp(DN_W), shp(LANES)],
        scratch_shapes=[pltpu.VMEM((tile + 2 * CONV_HALO, CONV_W), F32),
                        pltpu.VMEM((SUBLANES - 1, tile + 2 * CONV_HALO - SUBLANES, CONV_W), F32),
                        pltpu.VMEM((tile + 2 * DN_HALO, 3 * QK_W), F32)],
        compiler_params=pltpu.CompilerParams(vmem_limit_bytes=VMEM_LIMIT),
        name="local_mixers",
    )(proj, proj, proj, proj, proj, proj, proj, proj,
      lp["conv_dw"], lp["conv_p"], lp["dn_conv"], lp["gvec"], lp["gm_p"], lp["gm_ws"], lp["gm_bsf"], lp["gavg"])


DN_FIRST, DN_LAST_EMIT, DN_ZERO_INIT = 1, 2, 4


def _deltanet_kernel(tf_ref, tb_ref, flag_ref, s0i_ref, sfi_ref,
                     qf_ref, kf_ref, vf_ref, gf_ref, qb_ref, kb_ref, vb_ref, gb_ref, s0_ref,
                     of_ref, ob_ref, *rest, tile, emit_state):
    if emit_state:
        sfin_ref, s_ref, pq_scr, b_scr, o_scr, gt_scr = rest
    else:
        s_ref, pq_scr, b_scr, o_scr, gt_scr = rest
    flags = flag_ref[pl.program_id(0)]
    first = (flags & DN_FIRST) != 0
    zero_init = (flags & DN_ZERO_INIT) != 0
    c = DN_CHUNK
    n_chunks = tile // c
    upd = n_chunks * DN_HEADS

    @pl.when(first & zero_init)
    def _():
        s_ref[...] = jnp.zeros_like(s_ref)

    @pl.when(first & jnp.logical_not(zero_init))
    def _():
        for d in range(2):
            for h in range(DN_HEADS):
                s_ref[d * DN_HEADS + h] = s0_ref[0, 0, d, h]

    ti = lax.broadcasted_iota(jnp.int32, (tile, tile), 0)
    tj = lax.broadcasted_iota(jnp.int32, (tile, tile), 1)
    same_chunk = (ti // c) == (tj // c)
    nt_dims = (((1,), (1,)), ((), ()))
    tn_dims = (((0,), (0,)), ((), ()))
    dirs = (
        (qf_ref, kf_ref, vf_ref, gf_ref, of_ref, ti >= tj, c - 1),
        (qb_ref, kb_ref, vb_ref, gb_ref, ob_ref, ti <= tj, 0),
    )

    def gate_context(d):
        g_ref, tile_tri = dirs[d][3], dirs[d][5]
        gates = g_ref[...]
        blockcum = (same_chunk & tile_tri).astype(BF16)
        gc_t = _dot_exact_lhs(blockcum, gates)
        return gates, gc_t, gc_t.T

    gate_ctx = [gate_context(0), gate_context(1)]

    b16 = lambda x: x.astype(BF16)
    mm = lambda x, y: jnp.dot(x, y, preferred_element_type=F32)

    pi = lax.broadcasted_iota(jnp.int32, (c, 2 * c), 0)
    plane = lax.broadcasted_iota(jnp.int32, (c, 2 * c), 1)
    pj = plane & (c - 1)
    left = plane < c
    psame = lambda s: (pi >> s) == (pj >> s)
    pm8 = psame(3)
    pl16 = psame(4) & ~pm8
    pl32 = psame(5) & ~psame(4)
    pl64 = ~psame(5)
    ptri = ((pi >= pj, pi > pj), (pi <= pj, pi < pj))

    def blockdiag(y):
        return b16(jnp.concatenate([jnp.where(left, y, 0.0), jnp.where(left, 0.0, y)], axis=0))

    def blockdiag_wide(ya, yb):
        z = jnp.zeros_like(ya)
        return b16(jnp.concatenate([jnp.concatenate([ya, z], axis=1), jnp.concatenate([z, yb], axis=1)], axis=0))

    def prep():
        pairs = [(d, ci, p) for d in range(2) for ci in range(n_chunks) for p in range(DN_HEADS // 2)]
        chunk = {}
        for d in range(2):
            gc_t, last_row = gate_ctx[d][1], dirs[d][6]
            for ci in range(n_chunks):
                r0 = ci * c
                gc_c = gc_t[r0:r0 + c, :]
                glast = gc_t[r0 + last_row:r0 + last_row + 1, :]
                chunk[d, ci] = (gc_c, jnp.exp(gc_c), jnp.exp(glast - gc_c), jnp.exp(glast))
        lane_b = lambda d, h: d * DN_HEADS + h
        lane_g = lambda d, h: 2 * DN_HEADS + d * DN_HEADS + h
        col = lambda x, l: x[:, l:l + 1]
        n = range(len(pairs))
        heads = [(2 * p, 2 * p + 1) for _, _, p in pairs]

        def gj_pair(d, ci, ha, hb):
            gc_tt = gate_ctx[d][2]
            grow = lambda h: gc_tt[lane_g(d, h):lane_g(d, h) + 1, (ci // 2) * 2 * c:(ci // 2 + 1) * 2 * c]
            ra, rb = grow(ha), grow(hb)
            if ci % 2 == 0:
                rb = pltpu.roll(rb, c, axis=1)
            else:
                ra = pltpu.roll(ra, c, axis=1)
            return jnp.where(left[0:1, :], ra, rb)

        beta_h = [[col(gate_ctx[d][0][ci * c:(ci + 1) * c, :], lane_b(d, h)) for h in heads[i]]
                  for i, (d, ci, _) in enumerate(pairs)]
        eg_h = [[col(chunk[d, ci][1], lane_g(d, h)) for h in heads[i]] for i, (d, ci, _) in enumerate(pairs)]
        beta = [jnp.where(left, beta_h[i][0], beta_h[i][1]) for i in n]
        dmat = [jnp.where(ptri[d][0], jnp.exp(jnp.minimum(
            jnp.where(left, col(chunk[d, ci][0], lane_g(d, heads[i][0])), col(chunk[d, ci][0], lane_g(d, heads[i][1])))
            - gj_pair(d, ci, *heads[i]), 0.0)), 0.0) for i, (d, ci, _) in enumerate(pairs)]
        w2 = 2 * DN_DK
        q = [dirs[d][0][ci * c:(ci + 1) * c, p * w2:(p + 1) * w2] for d, ci, p in pairs]
        k = [dirs[d][1][ci * c:(ci + 1) * c, p * w2:(p + 1) * w2] for d, ci, p in pairs]
        v = [dirs[d][2][ci * c:(ci + 1) * c, p * w2:(p + 1) * w2] for d, ci, p in pairs]
        k16 = [b16(x) for x in k]
        kbd = [blockdiag_wide(x[:, :DN_DK], x[:, DN_DK:]) for x in k]
        kk = [lax.dot_general(k16[i], kbd[i], nt_dims, preferred_element_type=F32) for i in n]
        qk = [lax.dot_general(b16(q[i]), kbd[i], nt_dims, preferred_element_type=F32) for i in n]
        a = [jnp.where(ptri[pairs[i][0]][1], beta[i] * kk[i] * dmat[i], 0.0) for i in n]
        dd = [jnp.where(pm8, a[i], 0.0) for i in n]
        dd16 = [b16(x) for x in dd]
        d2 = [mm(dd16[i], blockdiag(dd[i])) for i in n]
        d2bd = [blockdiag(x) for x in d2]
        d3 = [mm(dd16[i], d2bd[i]) for i in n]
        d4 = [mm(b16(d2[i]), d2bd[i]) for i in n]
        e = [d2[i] - dd[i] - d3[i] for i in n]
        t = [mm(b16(e[i]), blockdiag(d4[i])) for i in n]
        e = [e[i] + d4[i] + t[i] for i in n]
        for sel in (pl16, pl32, pl64):
            l = [jnp.where(sel, a[i], 0.0) for i in n]
            ly = [l[i] + mm(b16(e[i]), blockdiag(l[i])) for i in n]
            z = [mm(b16(ly[i]), blockdiag(e[i])) for i in n]
            e = [e[i] - ly[i] - z[i] for i in n]
        half = lambda x, j: x[:, j * DN_DK:(j + 1) * DN_DK]
        r = [[jnp.concatenate([half(k[i], j) * (beta_h[i][j] * eg_h[i][j]), half(v[i], j) * beta_h[i][j]], axis=1)
              for j in range(2)] for i in n]
        wu = [jnp.concatenate(r[i], axis=1) + mm(b16(e[i]), blockdiag_wide(*r[i])) for i in n]
        wu_h = [[wu[i][:, j * w2:(j + 1) * w2] for j in range(2)] for i in n]
        qo = [mm(b16(qk[i] * dmat[i]), blockdiag_wide(*wu_h[i])) for i in n]
        units = [(i, j) for i in n for j in range(2)]
        kd = [b16(half(k[i], j) * col(chunk[pairs[i][0], pairs[i][1]][2], lane_g(pairs[i][0], heads[i][j])))
              for i, j in units]
        pb = [lax.dot_general(kd[m], b16(wu_h[i][j]), tn_dims, preferred_element_type=F32)
              for m, (i, j) in enumerate(units)]
        for m, (i, j) in enumerate(units):
            d, ci, h = pairs[i][0], pairs[i][1], heads[i][j]
            u = d * upd + ci * DN_HEADS + h
            qo_h = qo[i][:, j * w2:(j + 1) * w2]
            pq_scr[u, 0:DN_DK, :] = b16(pb[m][:, :DN_DK])
            pq_scr[u, DN_DK:DN_DK + c, :] = b16(half(q[i], j) * eg_h[i][j] - qo_h[:, :DN_DK])
            b_scr[u] = pb[m][:, DN_DK:]
            o_scr[u] = qo_h[:, DN_DK:]
            gt_scr[u] = jnp.broadcast_to(col(chunk[d, ci][3], lane_g(d, h)), (1, LANES))

    def scan(step):
        chains = [(0, step, h) for h in range(DN_HEADS)] + [(1, n_chunks - 1 - step, h) for h in range(DN_HEADS)]
        s = [s_ref[d * DN_HEADS + h] for d, ci, h in chains]
        ps = [mm(pq_scr[d * upd + ci * DN_HEADS + h], b16(s[i])) for i, (d, ci, h) in enumerate(chains)]
        for i, (d, ci, h) in enumerate(chains):
            u = d * upd + ci * DN_HEADS + h
            s_ref[d * DN_HEADS + h] = s[i] * gt_scr[u] - ps[i][0:DN_DK, :] + b_scr[u]
            dirs[d][4][ci * c:(ci + 1) * c, h * DN_DV:(h + 1) * DN_DV] = ps[i][DN_DK:DN_DK + c, :] + o_scr[u]

    prep()
    for step in range(n_chunks):
        scan(step)

    if emit_state:
        @pl.when((flags & DN_LAST_EMIT) != 0)
        def _():
            for d in range(2):
                for h in range(DN_HEADS):
                    sfin_ref[0, d, h] = s_ref[d * DN_HEADS + h]


def _deltanet_schedule(batch, seq, dec_batch, dec_seq, s0_offset, tile):
    tf, tb, flags, s0i, sfi = [], [], [], [], []
    t0 = 0
    for n_seq, length, is_ctx in ((batch, seq, True), (dec_batch, dec_seq, False)):
        tps = length // tile
        for b in range(n_seq):
            for t in range(tps):
                tf.append(t0 + b * tps + t)
                tb.append(t0 + b * tps + tps - 1 - t)
                flags.append((DN_FIRST if t == 0 else 0)
                             | (DN_LAST_EMIT if (is_ctx and t == tps - 1) else 0)
                             | (DN_ZERO_INIT if is_ctx else 0))
                s0i.append(0 if is_ctx else s0_offset + b)
                sfi.append(b if is_ctx else max(batch - 1, 0))
        t0 += n_seq * tps
    return [jnp.asarray(a, jnp.int32) for a in (tf, tb, flags, s0i, sfi)]


def _deltanet(q, k, v, gates, s0, layer, batch, seq, dec_batch, dec_seq, s0_offset):
    tile = MIX_TILE
    emit_state = batch > 0
    sched = _deltanet_schedule(batch, seq, dec_batch, dec_seq, s0_offset, tile)
    fwd = lambda w: pl.BlockSpec((tile, w), lambda s, tf, tb, fl, s0i, sfi: (tf[s], 0))
    bwd = lambda w: pl.BlockSpec((tile, w), lambda s, tf, tb, fl, s0i, sfi: (tb[s], 0))
    state_block = (2, DN_HEADS, DN_DK, DN_DV)
    units = 2 * (tile // DN_CHUNK) * DN_HEADS
    grid_spec = pltpu.PrefetchScalarGridSpec(
        num_scalar_prefetch=len(sched),
        grid=(sched[0].shape[0],),
        in_specs=[fwd(QK_W), fwd(QK_W), fwd(DN_W), fwd(LANES), bwd(QK_W), bwd(QK_W), bwd(DN_W), bwd(LANES),
                  pl.BlockSpec((1, 1) + state_block,
                               lambda s, tf, tb, fl, s0i, sfi: (s0i[s], layer, 0, 0, 0, 0))],
        out_specs=[fwd(DN_W), bwd(DN_W)] + (
            [pl.BlockSpec((1,) + state_block, lambda s, tf, tb, fl, s0i, sfi: (sfi[s], 0, 0, 0, 0))]
            if emit_state else []),
        scratch_shapes=[pltpu.VMEM((2 * DN_HEADS, DN_DK, DN_DV), F32),
                        pltpu.VMEM((units, DN_DK + DN_CHUNK, DN_DV), BF16),
                        pltpu.VMEM((units, DN_DK, DN_DV), F32),
                        pltpu.VMEM((units, DN_CHUNK, DN_DV), F32),
                        pltpu.VMEM((units, 1, LANES), F32)],
    )
    return pl.pallas_call(
        functools.partial(_deltanet_kernel, tile=tile, emit_state=emit_state),
        grid_spec=grid_spec,
        out_shape=[jax.ShapeDtypeStruct((q.shape[0], DN_W), F32)] * 2
        + ([jax.ShapeDtypeStruct((batch,) + state_block, F32)] if emit_state else []),
        compiler_params=pltpu.CompilerParams(dimension_semantics=("arbitrary",), vmem_limit_bytes=VMEM_LIMIT),
        name="deltanet_scan",
    )(*sched, q, k, v, gates, q, k, v, gates, s0)


def _outproj_kernel(ya_ref, of_ref, ob_ref, z_ref, yc_ref, x_ref, mod_ref, wout_ref, vec_ref, dng_ref,
                    wr_hi_ref, wr_lo_ref, br_ref, x1_ref, h_ref, route_ref, cnt_ref, run_ref,
                    *, tile, n_ctx, dec_seq):
    m = _mod_row(mod_ref, pl.program_id(0) * tile, n_ctx, dec_seq)
    gt1 = m[:, 2 * D_MODEL:3 * D_MODEL]
    sh2 = m[:, 3 * D_MODEL:4 * D_MODEL]
    sc2 = m[:, 4 * D_MODEL:5 * D_MODEL]
    o = of_ref[...] + ob_ref[...]
    z = z_ref[...]
    zz = z * _sigmoid(z)
    hs = []
    for h in range(DN_HEADS):
        oh = o[:, h * DN_DV:(h + 1) * DN_DV]
        on = oh * lax.rsqrt(jnp.mean(oh * oh, axis=-1, keepdims=True) + EPS) * dng_ref[...]
        hs.append(on * zz[:, h * DN_DV:(h + 1) * DN_DV])
    ycat = jnp.concatenate([ya_ref[...]] + hs + [yc_ref[...]], axis=-1).astype(BF16)
    y = jnp.dot(ycat, wout_ref[...], preferred_element_type=F32)
    x1 = _ln(DEEPNORM_ALPHA * x_ref[...] + gt1 * y) * vec_ref[0:1, :] + vec_ref[1:2, :]
    x1_ref[...] = x1
    hf = _ln(x1) * (1.0 + sc2) + sh2
    h_hi = hf.astype(BF16)
    h_lo = (hf - h_hi.astype(F32)).astype(BF16)
    h_ref[...] = h_hi
    logits = (jnp.dot(h_hi, wr_hi_ref[...], preferred_element_type=F32)
              + jnp.dot(h_lo, wr_hi_ref[...], preferred_element_type=F32)
              + jnp.dot(h_hi, wr_lo_ref[...], preferred_element_type=F32)
              + br_ref[...])
    lane = lax.broadcasted_iota(jnp.int32, logits.shape, 1).astype(F32)
    vals, idxs = [], []
    for _ in range(TOP_K):
        m = jnp.max(logits, axis=-1, keepdims=True)
        idx = jnp.min(jnp.where(logits == m, lane, float(LANES)), axis=-1, keepdims=True)
        vals.append(m)
        idxs.append(idx)
        logits = jnp.where(lane == idx, -jnp.inf, logits)
    es = [jnp.exp(v - vals[0]) for v in vals]
    den = (es[0] + es[1]) + (es[2] + es[3])
    @pl.when(pl.program_id(0) == 0)
    def _():
        run_ref[...] = jnp.zeros_like(run_ref)

    onehot = jnp.zeros(logits.shape, F32)
    for idx in idxs:
        onehot = onehot + (lane == idx).astype(F32)
    ri = lax.broadcasted_iota(jnp.int32, (tile, tile), 0)
    rj = lax.broadcasted_iota(jnp.int32, (tile, tile), 1)
    before = jnp.dot((ri > rj).astype(BF16), onehot.astype(BF16), preferred_element_type=F32) + run_ref[...]
    run_ref[...] = run_ref[...] + jnp.sum(onehot, axis=0, keepdims=True)
    cnt_ref[...] = jnp.broadcast_to(run_ref[...], cnt_ref.shape)
    route = jnp.zeros(logits.shape, F32)
    for j in range(TOP_K):
        rank = jnp.sum(jnp.where(lane == idxs[j], before, 0.0), axis=-1, keepdims=True)
        route = jnp.where(lane == float(j), idxs[j], route)
        route = jnp.where(lane == float(TOP_K + j), es[j] / den, route)
        route = jnp.where(lane == float(2 * TOP_K + j), rank, route)
    route_ref[...] = route


def _out_projection(ya, o_f, o_b, proj, yc, x, mod, lp, n_ctx, dec_seq):
    nt, d = x.shape
    tile = TOK_TILE
    tok = lambda w: pl.BlockSpec((tile, w), lambda i: (i, 0))
    full = lambda a: pl.BlockSpec(a.shape, lambda i: (0,) * a.ndim)
    return pl.pallas_call(
        functools.partial(_outproj_kernel, tile=tile, n_ctx=n_ctx, dec_seq=dec_seq),
        grid=(nt // tile,),
        in_specs=[tok(CONV_W), tok(DN_W), tok(DN_W),
                  pl.BlockSpec((tile, DN_W), lambda i: (i, COL_Z // DN_W)),
                  tok(GMLP_W), tok(d), full(mod), full(lp["w_out"]), full(lp["ln1"]), full(lp["dn_norm_g"]),
                  full(lp["wr_hi"]), full(lp["wr_lo"]), full(lp["b_router"])],
        out_specs=[tok(d), tok(d), tok(LANES), pl.BlockSpec((8, LANES), lambda i: (0, 0))],
        out_shape=[jax.ShapeDtypeStruct((nt, d), F32), jax.ShapeDtypeStruct((nt, d), BF16),
                   jax.ShapeDtypeStruct((nt, LANES), F32), jax.ShapeDtypeStruct((8, LANES), F32)],
        scratch_shapes=[pltpu.VMEM((1, LANES), F32)],
        compiler_params=pltpu.CompilerParams(dimension_semantics=("arbitrary",), vmem_limit_bytes=VMEM_LIMIT),
        name="out_projection_router",
    )(ya, o_f, o_b, proj, yc, x, mod, lp["w_out"], lp["ln1"], lp["dn_norm_g"],
      lp["wr_hi"], lp["wr_lo"], lp["b_router"])


def _expert_kernel(be_ref, nu_ref, x_ref, wgu_ref, bgu_ref, wd_ref, bd_ref, out_ref, wgu16, wd16):
    i = pl.program_id(0)

    @pl.when((i == 0) | (be_ref[i] != be_ref[jnp.maximum(i - 1, 0)]))
    def _():
        wgu16[...] = wgu_ref[0, 0].astype(BF16)
        wd16[...] = wd_ref[0, 0].astype(BF16)

    @pl.when(i < nu_ref[0])
    def _():
        gu = jnp.dot(x_ref[...], wgu16[...], preferred_element_type=F32) + bgu_ref[0, 0]
        gate = jnp.minimum(gu[:, :D_FF], SWIGLU_LIMIT)
        up = jnp.clip(gu[:, D_FF:], -SWIGLU_LIMIT, SWIGLU_LIMIT)
        act = gate * _sigmoid(SWIGLU_ALPHA * gate)
        hmid = ((up + 1.0) * act).astype(BF16)
        y = jnp.dot(hmid, wd16[...], preferred_element_type=F32) + bd_ref[0, 0]
        out_ref[...] = y.astype(out_ref.dtype)

    @pl.when(i >= nu_ref[0])
    def _():
        out_ref[...] = jnp.zeros_like(out_ref)


def _experts(xg, block_e, n_used, w_gu, b_gu, w_down, b_down, layer):
    m_pad, d = xg.shape
    bm = MOE_BM
    grid_spec = pltpu.PrefetchScalarGridSpec(
        num_scalar_prefetch=2,
        grid=(m_pad // bm,),
        in_specs=[
            pl.BlockSpec((bm, d), lambda i, be, nu: (i, 0)),
            pl.BlockSpec((1, 1, d, 2 * D_FF), lambda i, be, nu: (layer, be[i], 0, 0)),
            pl.BlockSpec((1, 1, 1, 2 * D_FF), lambda i, be, nu: (layer, be[i], 0, 0)),
            pl.BlockSpec((1, 1, D_FF, d), lambda i, be, nu: (layer, be[i], 0, 0)),
            pl.BlockSpec((1, 1, 1, d), lambda i, be, nu: (layer, be[i], 0, 0)),
        ],
        out_specs=pl.BlockSpec((bm, d), lambda i, be, nu: (i, 0)),
        scratch_shapes=[pltpu.VMEM((d, 2 * D_FF), BF16), pltpu.VMEM((D_FF, d), BF16)],
    )
    return pl.pallas_call(
        _expert_kernel,
        grid_spec=grid_spec,
        out_shape=jax.ShapeDtypeStruct((m_pad, d), BF16),
        compiler_params=pltpu.CompilerParams(dimension_semantics=("arbitrary",), vmem_limit_bytes=VMEM_LIMIT),
        name="moe_experts",
    )(block_e, n_used, xg, w_gu, b_gu, w_down, b_down)


def _combine_kernel(yg_ref, route_ref, x1_ref, mod_ref, vec_ref, out_ref, *, tile, n_ctx, dec_seq):
    m = _mod_row(mod_ref, pl.program_id(0) * tile, n_ctx, dec_seq)
    gt2 = m[:, 5 * D_MODEL:6 * D_MODEL]
    gate = lambda j: route_ref[:, TOP_K + j:TOP_K + j + 1]
    ye = lambda j: yg_ref[j].astype(F32) * gate(j)
    y = (ye(0) + ye(1)) + (ye(2) + ye(3))
    out_ref[...] = _ln(DEEPNORM_ALPHA * x1_ref[...] + gt2 * y) * vec_ref[0:1, :] + vec_ref[1:2, :]


def _combine(yg, route, x1, mod, ln2, n_ctx, dec_seq):
    nt, d = x1.shape
    tile = TOK_TILE
    return pl.pallas_call(
        functools.partial(_combine_kernel, tile=tile, n_ctx=n_ctx, dec_seq=dec_seq),
        grid=(nt // tile,),
        in_specs=[pl.BlockSpec((TOP_K, tile, d), lambda i: (0, i, 0)),
                  pl.BlockSpec((tile, LANES), lambda i: (i, 0)),
                  pl.BlockSpec((tile, d), lambda i: (i, 0)),
                  pl.BlockSpec(mod.shape, lambda i: (0, 0)),
                  pl.BlockSpec(ln2.shape, lambda i: (0, 0))],
        out_specs=pl.BlockSpec((tile, d), lambda i: (i, 0)),
        out_shape=jax.ShapeDtypeStruct((nt, d), F32),
        compiler_params=pltpu.CompilerParams(vmem_limit_bytes=VMEM_LIMIT),
        name="moe_combine_ln",
    )(yg, route, x1, mod, ln2)


def _route(route, counts):
    nt = route.shape[0]
    nk = nt * TOP_K
    bm = MOE_BM
    expert = route[:, 0:TOP_K].astype(jnp.int32)
    rank = route[:, 2 * TOP_K:3 * TOP_K].astype(jnp.int32)
    counts = counts[0, :N_EXPERTS].astype(jnp.int32)
    padded = (counts + bm - 1) // bm * bm
    pad_end = jnp.cumsum(padded)
    pad_start = pad_end - padded
    grp_start = jnp.cumsum(counts) - counts
    pair_slot = pad_start[expert] + rank
    tok = jnp.arange(nk, dtype=jnp.int32) // TOP_K
    _, sorted_tok = lax.sort((pair_slot.reshape(-1), tok), num_keys=1)
    n_blocks = nk // bm + N_EXPERTS
    blk_start = jnp.arange(n_blocks, dtype=jnp.int32) * bm
    block_e = jnp.minimum(jnp.sum((pad_end[None, :] <= blk_start[:, None]).astype(jnp.int32), axis=1),
                          N_EXPERTS - 1)
    slot_e = jnp.repeat(block_e, bm)
    slot = jnp.arange(n_blocks * bm, dtype=jnp.int32)
    j = slot - pad_start[slot_e]
    src = jnp.clip(grp_start[slot_e] + j, 0, nk - 1)
    slot_tok = jnp.where(j < counts[slot_e], sorted_tok[src], slot % nt)
    n_used = (pad_end[-1] // bm).astype(jnp.int32).reshape(1)
    return slot_tok, pair_slot, block_e.astype(jnp.int32), n_used


def _grid_pos_embed(t, d):
    rows = t // GRID_W
    r, col = jnp.meshgrid(jnp.arange(rows), jnp.arange(GRID_W), indexing="ij")
    r = r.reshape(-1).astype(F32)[:, None]
    col = col.reshape(-1).astype(F32)[:, None]
    n_freq = d // 4
    omega = 1.0 / (POS_BASE ** (jnp.arange(n_freq, dtype=F32) / n_freq))
    return jnp.concatenate([jnp.sin(r * omega), jnp.cos(r * omega),
                            jnp.sin(col * omega), jnp.cos(col * omega)], axis=-1)


def _pad_lanes(a, offset):
    return jnp.zeros((1, LANES), F32).at[0, offset:offset + a.shape[0]].set(a.astype(F32))


def _layer_params(l, w_in, conv_dw, conv_b, conv_ln_g, conv_ln_b, dn_conv, dn_a_log, dn_dt_bias, dn_norm_g,
                  gm_ln_g, gm_ln_b, gm_ws, gm_bs, w_out, ln1_g, ln1_b, ln2_g, ln2_b, w_router, b_router):
    wi = w_in[l]
    c_conv, c_qkv, c_z, c_ba = 0, 2 * CONV_W, 2 * CONV_W + 2 * QK_W + DN_W, 2 * CONV_W + 2 * QK_W + 2 * DN_W
    c_gm = c_ba + 4 * DN_HEADS
    w_in_r = jnp.concatenate([
        wi[:, c_qkv:c_z], wi[:, c_conv:c_qkv], wi[:, c_z:c_ba], wi[:, c_gm:],
        wi[:, c_ba:c_gm], jnp.zeros((D_MODEL, LANES - 4 * DN_HEADS), F32)], axis=1).astype(BF16)
    grp = jnp.arange(CONV_W) // GROUP_W
    gavg = (grp[:, None] == grp[None, :]).astype(BF16) * (1.0 / GROUP_W)
    gvec = jnp.concatenate([_pad_lanes(dn_a_log[l].reshape(-1), 2 * DN_HEADS),
                            _pad_lanes(dn_dt_bias[l].reshape(-1), 2 * DN_HEADS)], axis=0)
    wr = jnp.zeros((D_MODEL, LANES), F32).at[:, :N_EXPERTS].set(w_router[l])
    wr_hi = wr.astype(BF16)
    wr_lo = (wr - wr_hi.astype(F32)).astype(BF16)
    return {
        "w_in_r": w_in_r,
        "conv_dw": conv_dw[l],
        "conv_p": jnp.stack([conv_b[l], conv_ln_g[l], conv_ln_b[l]]),
        "dn_conv": dn_conv[l],
        "gvec": gvec,
        "gm_p": jnp.stack([gm_ln_g[l], gm_ln_b[l]]),
        "gm_ws": gm_ws[l],
        "gm_bsf": jnp.repeat(jnp.transpose(gm_bs[l]), GROUP_W, axis=1),
        "gavg": gavg,
        "w_out": w_out[l].astype(BF16),
        "ln1": jnp.stack([ln1_g[l], ln1_b[l]]),
        "ln2": jnp.stack([ln2_g[l], ln2_b[l]]),
        "dn_norm_g": dn_norm_g[l].reshape(1, DN_DV),
        "wr_hi": wr_hi,
        "wr_lo": wr_lo,
        "b_router": jnp.full((1, LANES), -1e30, F32).at[0, :N_EXPERTS].set(b_router[l]),
    }


def kernel(x_prompt, x_sample, state_delta, c, c_ctx, w_ada, b_ada, w_in, conv_dw, conv_b, conv_ln_g, conv_ln_b, dn_conv, dn_a_log, dn_dt_bias, dn_norm_g, gm_ln_g, gm_ln_b, gm_ws, gm_bs, w_out, ln1_g, ln1_b, ln2_g, ln2_b, w_router, b_router, w_gu, b_gu, w_down, b_down):
    batch, seq, d = x_prompt.shape
    dec_batch, dec_seq, _ = x_sample.shape
    n_ctx = batch * seq
    n_den = dec_batch * dec_seq
    depth = w_in.shape[0]

    groups = [(batch, 0, dec_batch)]
    pos = _grid_pos_embed(dec_seq, d)
    xp_flat = x_prompt.reshape(n_ctx, d)
    xs_flat = x_sample.reshape(n_den, d)
    b_gu_r = b_gu.reshape(depth, N_EXPERTS, 1, 2 * D_FF)
    b_down_r = b_down.reshape(depth, N_EXPERTS, 1, d)

    xs, mods = [], []
    for g_batch, g_first, g_count in groups:
        g_ctx = g_batch * seq
        cond = jnp.zeros((16, d), F32).at[0].set(c_ctx).at[1:1 + g_count].set(c[g_first:g_first + g_count])
        mods.append(_modulation(cond, w_ada, b_ada))
        xs.append(_assemble_tokens(xp_flat if g_ctx else xs_flat, xs_flat, pos, TOK_TILE,
                                   g_ctx, g_first * dec_seq, g_count * dec_seq))

    ctx_states = []
    for l in range(depth):
        lp = _layer_params(l, w_in, conv_dw, conv_b, conv_ln_g, conv_ln_b, dn_conv, dn_a_log, dn_dt_bias,
                           dn_norm_g, gm_ln_g, gm_ln_b, gm_ws, gm_bs, w_out, ln1_g, ln1_b, ln2_g, ln2_b,
                           w_router, b_router)
        for gi, (g_batch, g_first, g_count) in enumerate(groups):
            g_ctx = g_batch * seq
            x, mod = xs[gi], mods[gi][l]
            proj = _in_projection(x, mod, lp["w_in_r"], g_ctx, dec_seq)
            ya, yc, q, k, v, gates = _local_mixers(proj, lp, g_ctx // MIX_TILE, seq // MIX_TILE, dec_seq // MIX_TILE)
            dn_out = _deltanet(q, k, v, gates, state_delta, l, g_batch, seq, g_count, dec_seq, g_first)
            if g_batch:
                ctx_states.append(dn_out[2])
            x1, hffn, route, counts = _out_projection(ya, dn_out[0], dn_out[1], proj, yc, x, mod, lp, g_ctx, dec_seq)
            slot_tok, pair_slot, block_e, n_used = _route(route, counts)
            xg = hffn.at[slot_tok].get(mode="promise_in_bounds")
            yb = _experts(xg, block_e, n_used, w_gu, b_gu_r, w_down, b_down_r, l)
            yg = yb.at[jnp.transpose(pair_slot)].get(mode="promise_in_bounds")
            xs[gi] = _combine(yg, route, x1, mod, lp["ln2"], g_ctx, dec_seq)

    new_state = jnp.stack(ctx_states, axis=1).astype(x_prompt.dtype)
    y_prompt = xs[0][:n_ctx].reshape(batch, seq, d)
    y_sample = jnp.concatenate([xs[0][n_ctx:]] + xs[1:], axis=0).reshape(dec_batch, dec_seq, d)
    return (y_prompt, y_sample, new_state)
```

```python
import functools

import jax
import jax.numpy as jnp
from jax import lax
from jax.experimental import pallas as pl
from jax.experimental.pallas import tpu as pltpu

F32 = jnp.float32
BF16 = jnp.bfloat16

D_MODEL = 1024
DEPTH = 2
GRID_W = 64
POS_BASE = 10000.0
CONV_W = 256
CONV_K = 31
CONV_HALO = 16
DN_HEADS = 4
DN_DK = 128
DN_DV = 128
QK_W = DN_HEADS * DN_DK
DN_W = DN_HEADS * DN_DV
DN_SHORT_K = 5
DN_HALO = 8
DN_CHUNK = 64
GMLP_W = 256
GMLP_GROUPS = 4
GMLP_CHUNK = 128
GROUP_W = 64
N_EXPERTS = 32
TOP_K = 4
D_FF = D_MODEL
SWIGLU_ALPHA = 1.702
SWIGLU_LIMIT = 7.0
DEEPNORM_ALPHA = (2 * DEPTH) ** 0.25
EPS = 1e-6

LANES = 128
SUBLANES = 8
COL_QKV = 0
COL_CONV = 2 * QK_W + DN_W
COL_Z = COL_CONV + 2 * CONV_W
COL_GM = COL_Z + DN_W
COL_BA = COL_GM + 2 * GMLP_W
PROJ_W = COL_BA + LANES

MIX_TILE = 256
TOK_TILE = 512
OUTPROJ_PARTS = 2
MOE_BM = 512
VMEM_LIMIT = 56 * 1024 * 1024


def _ln(x):
    mu = jnp.mean(x, axis=-1, keepdims=True)
    xc = x - mu
    return xc * lax.rsqrt(jnp.mean(xc * xc, axis=-1, keepdims=True) + EPS)


def _sigmoid(x):
    return jax.nn.sigmoid(x)


def _split_bf16(x, parts):
    out = []
    r = x
    for _ in range(parts):
        p = r.astype(BF16)
        out.append(p)
        r = r - p.astype(F32)
    return out


def _dot_exact_rhs(x, m_bf16, parts=3):
    acc = None
    for p in _split_bf16(x, parts):
        t = jnp.dot(p, m_bf16, preferred_element_type=F32)
        acc = t if acc is None else acc + t
    return acc


def _dot_exact_lhs(m_bf16, x, parts=3):
    acc = None
    for p in _split_bf16(x, parts):
        t = jnp.dot(m_bf16, p, preferred_element_type=F32)
        acc = t if acc is None else acc + t
    return acc


def _mod_row(mod_ref, start, n_ctx, dec_seq):
    row = jnp.where(start < n_ctx, 0, 1 + (start - n_ctx) // dec_seq)
    return mod_ref[pl.ds(row, 1), :]


def _mod_kernel(cond_ref, w_ref, b_ref, out_ref):
    c = cond_ref[...]
    s = c * _sigmoid(c)
    out_ref[0] = jnp.dot(s, w_ref[0], preferred_element_type=F32) + b_ref[0]


def _modulation(cond, w_ada, b_ada):
    nl, d, n = w_ada.shape
    r = cond.shape[0]
    tn = 1024
    return pl.pallas_call(
        _mod_kernel,
        grid=(nl, n // tn),
        in_specs=[
            pl.BlockSpec((r, d), lambda l, j: (0, 0)),
            pl.BlockSpec((1, d, tn), lambda l, j: (l, 0, j)),
            pl.BlockSpec((1, 1, tn), lambda l, j: (l, 0, j)),
        ],
        out_specs=pl.BlockSpec((1, r, tn), lambda l, j: (l, 0, j)),
        out_shape=jax.ShapeDtypeStruct((nl, r, n), F32),
        name="adaln_modulation",
    )(cond, w_ada, b_ada.reshape(nl, 1, n))


def _token_specs(x_src, tile, n_ctx):
    d = x_src[0].shape[1]
    if len(x_src) == 1:
        return [pl.BlockSpec((tile, d), lambda i: (i, 0))]
    ct = n_ctx // tile
    pt = x_src[2].shape[0] // tile
    return [pl.BlockSpec((tile, d), lambda i: (jnp.minimum(i, max(ct - 1, 0)), 0)),
            pl.BlockSpec((tile, d), lambda i: (jnp.maximum(i - ct, 0), 0)),
            pl.BlockSpec((tile, d), lambda i: (jnp.maximum(i - ct, 0) % pt, 0))]


def _load_tokens(x_refs, rows, is_ctx_tile):
    if len(x_refs) == 1:
        return x_refs[0][rows, :]
    xp_ref, xs_ref, pos_ref = x_refs
    return jnp.where(is_ctx_tile, xp_ref[rows, :], xs_ref[rows, :] + pos_ref[rows, :])


def _inproj_kernel(*refs, tile, n_ctx, dec_seq):
    *x_refs, mod_ref, w_ref, out_ref = refs
    start = pl.program_id(0) * tile
    m = _mod_row(mod_ref, start, n_ctx, dec_seq)
    sh1 = m[:, 0:D_MODEL]
    sc1 = m[:, D_MODEL:2 * D_MODEL]
    h = _ln(_load_tokens(x_refs, slice(None), start < n_ctx)) * (1.0 + sc1) + sh1
    out_ref[...] = jnp.dot(h.astype(BF16), w_ref[...], preferred_element_type=F32)


def _in_projection(x_src, nt, mod, w_in_r, n_ctx, dec_seq):
    tile = TOK_TILE
    return pl.pallas_call(
        functools.partial(_inproj_kernel, tile=tile, n_ctx=n_ctx, dec_seq=dec_seq),
        grid=(nt // tile,),
        in_specs=_token_specs(x_src, tile, n_ctx) + [
            pl.BlockSpec(mod.shape, lambda i: (0, 0)),
            pl.BlockSpec(w_in_r.shape, lambda i: (0, 0)),
        ],
        out_specs=pl.BlockSpec((tile, PROJ_W), lambda i: (i, 0)),
        out_shape=jax.ShapeDtypeStruct((nt, PROJ_W), F32),
        compiler_params=pltpu.CompilerParams(vmem_limit_bytes=VMEM_LIMIT),
        name="in_projection",
    )(*x_src, mod, w_in_r)


def _group_norm(x, gavg):
    mean = _dot_exact_rhs(x, gavg, parts=2)
    xc = x - mean
    var = _dot_exact_rhs(xc * xc, gavg, parts=2)
    return xc * lax.rsqrt(var + EPS)


def _mixpre_kernel(qkv_ref, qkv_p_ref, qkv_n_ref, cv_ref, cv_p_ref, cv_n_ref, gm_ref, ba_ref,
                   convw_ref, convp_ref, dnw_ref, gvec_ref, gmp_ref, ws_ref, bsf_ref, gavg_ref,
                   ya_ref, yc_ref, q_ref, k_ref, v_ref, gates_ref, cbuf, cshift, qbuf,
                   *, tile, n_ctx_tiles, ctx_tps, den_tps):
    i = pl.program_id(0)
    pos = jnp.where(i < n_ctx_tiles, i % ctx_tps, (i - n_ctx_tiles) % den_tps)
    tps = jnp.where(i < n_ctx_tiles, ctx_tps, den_tps)
    first = pos == 0
    last = pos == tps - 1
    gavg = gavg_ref[...]

    def glu(p):
        return p[:, :CONV_W] * _sigmoid(p[:, CONV_W:])

    cbuf[0:CONV_HALO, :] = jnp.where(first, 0.0, glu(cv_p_ref[...]))
    cbuf[CONV_HALO:CONV_HALO + tile, :] = glu(cv_ref[...])
    cbuf[CONV_HALO + tile:2 * CONV_HALO + tile, :] = jnp.where(last, 0.0, glu(cv_n_ref[...]))
    conv_b = convp_ref[0:1, :]
    conv_g = convp_ref[1:2, :]
    conv_beta = convp_ref[2:3, :]
    rc = 64
    off = CONV_HALO - CONV_K // 2
    span = tile + 2 * CONV_HALO - SUBLANES
    for b in range(1, SUBLANES):
        cshift[b - 1] = cbuf[b:b + span, :]
    for c in range(tile // rc):
        acc = jnp.zeros((rc, CONV_W), F32)
        for k in range(CONV_K):
            a, b = divmod(off + k, SUBLANES)
            r0 = c * rc + a * SUBLANES
            win = cbuf[r0:r0 + rc, :] if b == 0 else cshift[b - 1, r0:r0 + rc, :]
            acc = acc + win * convw_ref[k:k + 1, :]
        y = _group_norm(acc + conv_b, gavg) * conv_g + conv_beta
        ya_ref[c * rc:(c + 1) * rc, :] = y * _sigmoid(y)

    qbuf[0:DN_HALO, :] = jnp.where(first, 0.0, qkv_p_ref[...])
    qbuf[DN_HALO:DN_HALO + tile, :] = qkv_ref[...]
    qbuf[DN_HALO + tile:2 * DN_HALO + tile, :] = jnp.where(last, 0.0, qkv_n_ref[...])
    rq = 32
    offq = DN_HALO - DN_SHORT_K // 2
    outs = (q_ref, k_ref, v_ref)
    for part in range(3):
        c0 = part * QK_W
        for c in range(tile // rq):
            acc = jnp.zeros((rq, QK_W), F32)
            for k in range(DN_SHORT_K):
                acc = acc + (qbuf[c * rq + offq + k:c * rq + offq + k + rq, c0:c0 + QK_W]
                             * dnw_ref[k:k + 1, c0:c0 + QK_W])
            a = acc * _sigmoid(acc)
            if part < 2:
                scale = DN_DK ** -0.5 if part == 0 else 1.0
                hs = []
                for h in range(DN_HEADS):
                    ah = a[:, h * DN_DK:(h + 1) * DN_DK]
                    nrm = lax.rsqrt(jnp.sum(ah * ah, axis=-1, keepdims=True) + EPS)
                    hs.append(ah * (nrm * scale))
                a = jnp.concatenate(hs, axis=-1)
            outs[part][c * rq:(c + 1) * rq, :] = a

    p = ba_ref[...]
    beta = _sigmoid(p)
    xg = p + gvec_ref[1:2, :]
    softplus = jnp.maximum(xg, 0.0) + jnp.log1p(jnp.exp(-jnp.abs(xg)))
    g = -jnp.exp(gvec_ref[0:1, :]) * softplus
    lane = lax.broadcasted_iota(jnp.int32, p.shape, 1)
    gates_ref[...] = jnp.where(lane < 2 * DN_HEADS, beta, g)

    pg = gm_ref[...]
    ge = pg * (0.5 * (1.0 + jnp.tanh(0.7978845608028654 * (pg + 0.044715 * (pg * pg * pg)))))
    u = ge[:, :GMLP_W]
    vn = _group_norm(ge[:, GMLP_W:], gavg) * gmp_ref[0:1, :] + gmp_ref[1:2, :]
    grp = lax.broadcasted_iota(jnp.int32, (GMLP_CHUNK, GMLP_W), 1) // GROUP_W
    for n in range(tile // GMLP_CHUNK):
        vchunk = vn[n * GMLP_CHUNK:(n + 1) * GMLP_CHUNK, :]
        sg = bsf_ref[...]
        for gi in range(GMLP_GROUPS):
            r = jnp.dot(ws_ref[gi], vchunk, preferred_element_type=F32)
            sg = sg + jnp.where(grp == gi, r, 0.0)
        yc_ref[n * GMLP_CHUNK:(n + 1) * GMLP_CHUNK, :] = u[n * GMLP_CHUNK:(n + 1) * GMLP_CHUNK, :] * sg


def _local_mixers(proj, lp, n_ctx_tiles, ctx_tps, den_tps):
    nt = proj.shape[0]
    tile = MIX_TILE
    n_tiles = nt // tile
    cpb = tile // CONV_HALO
    qpb = tile // DN_HALO
    n_cblk = nt // CONV_HALO
    n_qblk = nt // DN_HALO
    col = lambda c, w: c // w
    full = lambda a: pl.BlockSpec(a.shape, lambda i: (0,) * a.ndim)
    in_specs = [
        pl.BlockSpec((tile, 3 * QK_W), lambda i: (i, col(COL_QKV, 3 * QK_W))),
        pl.BlockSpec((DN_HALO, 3 * QK_W), lambda i: (jnp.maximum(i * qpb - 1, 0), 0)),
        pl.BlockSpec((DN_HALO, 3 * QK_W), lambda i: (jnp.minimum((i + 1) * qpb, n_qblk - 1), 0)),
        pl.BlockSpec((tile, 2 * CONV_W), lambda i: (i, col(COL_CONV, 2 * CONV_W))),
        pl.BlockSpec((CONV_HALO, 2 * CONV_W), lambda i: (jnp.maximum(i * cpb - 1, 0), col(COL_CONV, 2 * CONV_W))),
        pl.BlockSpec((CONV_HALO, 2 * CONV_W),
                     lambda i: (jnp.minimum((i + 1) * cpb, n_cblk - 1), col(COL_CONV, 2 * CONV_W))),
        pl.BlockSpec((tile, 2 * GMLP_W), lambda i: (i, col(COL_GM, 2 * GMLP_W))),
        pl.BlockSpec((tile, LANES), lambda i: (i, col(COL_BA, LANES))),
        full(lp["conv_dw"]), full(lp["conv_p"]), full(lp["dn_conv"]), full(lp["gvec"]),
        full(lp["gm_p"]), full(lp["gm_ws"]), full(lp["gm_bsf"]), full(lp["gavg"]),
    ]
    tok = lambda w: pl.BlockSpec((tile, w), lambda i: (i, 0))
    shp = lambda w: jax.ShapeDtypeStruct((nt, w), F32)
    return pl.pallas_call(
        functools.partial(_mixpre_kernel, tile=tile, n_ctx_tiles=n_ctx_tiles, ctx_tps=ctx_tps, den_tps=den_tps),
        grid=(n_tiles,),
        in_specs=in_specs,
        out_specs=[tok(CONV_W), tok(GMLP_W), tok(QK_W), tok(QK_W), tok(DN_W), tok(LANES)],
        out_shape=[shp(CONV_W), shp(GMLP_W), shp(QK_W), shp(QK_W), shp(DN_W), shp(LANES)],
        scratch_shapes=[pltpu.VMEM((tile + 2 * CONV_HALO, CONV_W), F32),
                        pltpu.VMEM((SUBLANES - 1, tile + 2 * CONV_HALO - SUBLANES, CONV_W), F32),
                        pltpu.VMEM((tile + 2 * DN_HALO, 3 * QK_W), F32)],
        compiler_params=pltpu.CompilerParams(vmem_limit_bytes=VMEM_LIMIT),
        name="local_mixers",
    )(proj, proj, proj, proj, proj, proj, proj, proj,
      lp["conv_dw"], lp["conv_p"], lp["dn_conv"], lp["gvec"], lp["gm_p"], lp["gm_ws"], lp["gm_bsf"], lp["gavg"])


DN_FIRST, DN_LAST_EMIT, DN_ZERO_INIT = 1, 2, 4


def _deltanet_kernel(tf_ref, tb_ref, flag_ref, s0i_ref, sfi_ref,
                     qf_ref, kf_ref, vf_ref, gf_ref, qb_ref, kb_ref, vb_ref, gb_ref, s0_ref,
                     of_ref, ob_ref, sfin_ref, s_ref, pq_scr, b_scr, o_scr, gt_scr, *, tile):
    flags = flag_ref[pl.program_id(0)]
    first = (flags & DN_FIRST) != 0
    zero_init = (flags & DN_ZERO_INIT) != 0
    c = DN_CHUNK
    n_chunks = tile // c
    upd = n_chunks * DN_HEADS

    @pl.when(first & zero_init)
    def _():
        s_ref[...] = jnp.zeros_like(s_ref)

    @pl.when(first & jnp.logical_not(zero_init))
    def _():
        for d in range(2):
            for h in range(DN_HEADS):
                s_ref[d * DN_HEADS + h] = s0_ref[0, 0, d, h]

    ti = lax.broadcasted_iota(jnp.int32, (tile, tile), 0)
    tj = lax.broadcasted_iota(jnp.int32, (tile, tile), 1)
    same_chunk = (ti // c) == (tj // c)
    nt_dims = (((1,), (1,)), ((), ()))
    tn_dims = (((0,), (0,)), ((), ()))
    dirs = (
        (qf_ref, kf_ref, vf_ref, gf_ref, of_ref, ti >= tj, c - 1),
        (qb_ref, kb_ref, vb_ref, gb_ref, ob_ref, ti <= tj, 0),
    )

    def gate_context(d):
        g_ref, tile_tri = dirs[d][3], dirs[d][5]
        gates = g_ref[...]
        blockcum = (same_chunk & tile_tri).astype(BF16)
        gc_t = _dot_exact_lhs(blockcum, gates)
        return gates, gc_t, gc_t.T

    gate_ctx = [gate_context(0), gate_context(1)]

    b16 = lambda x: x.astype(BF16)
    mm = lambda x, y: jnp.dot(x, y, preferred_element_type=F32)

    pi = lax.broadcasted_iota(jnp.int32, (c, 2 * c), 0)
    plane = lax.broadcasted_iota(jnp.int32, (c, 2 * c), 1)
    pj = plane & (c - 1)
    left = plane < c
    psame = lambda s: (pi >> s) == (pj >> s)
    pm8 = psame(3)
    pl16 = psame(4) & ~pm8
    pl32 = psame(5) & ~psame(4)
    pl64 = ~psame(5)
    ptri = ((pi >= pj, pi > pj), (pi <= pj, pi < pj))

    def blockdiag(y):
        return b16(jnp.concatenate([jnp.where(left, y, 0.0), jnp.where(left, 0.0, y)], axis=0))

    def blockdiag_wide(ya, yb):
        z = jnp.zeros_like(ya)
        return b16(jnp.concatenate([jnp.concatenate([ya, z], axis=1), jnp.concatenate([z, yb], axis=1)], axis=0))

    def prep():
        pairs = [(d, ci, p) for d in range(2) for ci in range(n_chunks) for p in range(DN_HEADS // 2)]
        chunk = {}
        for d in range(2):
            gc_t, last_row = gate_ctx[d][1], dirs[d][6]
            for ci in range(n_chunks):
                r0 = ci * c
                gc_c = gc_t[r0:r0 + c, :]
                glast = gc_t[r0 + last_row:r0 + last_row + 1, :]
                chunk[d, ci] = (gc_c, jnp.exp(gc_c), jnp.exp(glast - gc_c), jnp.exp(glast))
        lane_b = lambda d, h: d * DN_HEADS + h
        lane_g = lambda d, h: 2 * DN_HEADS + d * DN_HEADS + h
        col = lambda x, l: x[:, l:l + 1]
        n = range(len(pairs))
        heads = [(2 * p, 2 * p + 1) for _, _, p in pairs]

        def gj_pair(d, ci, ha, hb):
            gc_tt = gate_ctx[d][2]
            grow = lambda h: gc_tt[lane_g(d, h):lane_g(d, h) + 1, (ci // 2) * 2 * c:(ci // 2 + 1) * 2 * c]
            ra, rb = grow(ha), grow(hb)
            if ci % 2 == 0:
                rb = pltpu.roll(rb, c, axis=1)
            else:
                ra = pltpu.roll(ra, c, axis=1)
            return jnp.where(left[0:1, :], ra, rb)

        beta_h = [[col(gate_ctx[d][0][ci * c:(ci + 1) * c, :], lane_b(d, h)) for h in heads[i]]
                  for i, (d, ci, _) in enumerate(pairs)]
        eg_h = [[col(chunk[d, ci][1], lane_g(d, h)) for h in heads[i]] for i, (d, ci, _) in enumerate(pairs)]
        beta = [jnp.where(left, beta_h[i][0], beta_h[i][1]) for i in n]
        dmat = [jnp.where(ptri[d][0], jnp.exp(jnp.minimum(
            jnp.where(left, col(chunk[d, ci][0], lane_g(d, heads[i][0])), col(chunk[d, ci][0], lane_g(d, heads[i][1])))
            - gj_pair(d, ci, *heads[i]), 0.0)), 0.0) for i, (d, ci, _) in enumerate(pairs)]
        w2 = 2 * DN_DK
        q = [dirs[d][0][ci * c:(ci + 1) * c, p * w2:(p + 1) * w2] for d, ci, p in pairs]
        k = [dirs[d][1][ci * c:(ci + 1) * c, p * w2:(p + 1) * w2] for d, ci, p in pairs]
        v = [dirs[d][2][ci * c:(ci + 1) * c, p * w2:(p + 1) * w2] for d, ci, p in pairs]
        k16 = [b16(x) for x in k]
        kbd = [blockdiag_wide(x[:, :DN_DK], x[:, DN_DK:]) for x in k]
        kk = [lax.dot_general(k16[i], kbd[i], nt_dims, preferred_element_type=F32) for i in n]
        qk = [lax.dot_general(b16(q[i]), kbd[i], nt_dims, preferred_element_type=F32) for i in n]
        a = [jnp.where(ptri[pairs[i][0]][1], beta[i] * kk[i] * dmat[i], 0.0) for i in n]
        dd = [jnp.where(pm8, a[i], 0.0) for i in n]
        dd16 = [b16(x) for x in dd]
        d2 = [mm(dd16[i], blockdiag(dd[i])) for i in n]
        d2bd = [blockdiag(x) for x in d2]
        d3 = [mm(dd16[i], d2bd[i]) for i in n]
        d4 = [mm(b16(d2[i]), d2bd[i]) for i in n]
        e = [d2[i] - dd[i] - d3[i] for i in n]
        t = [mm(b16(e[i]), blockdiag(d4[i])) for i in n]
        e = [e[i] + d4[i] + t[i] for i in n]
        for sel in (pl16, pl32, pl64):
            l = [jnp.where(sel, a[i], 0.0) for i in n]
            ly = [l[i] + mm(b16(e[i]), blockdiag(l[i])) for i in n]
            z = [mm(b16(ly[i]), blockdiag(e[i])) for i in n]
            e = [e[i] - ly[i] - z[i] for i in n]
        half = lambda x, j: x[:, j * DN_DK:(j + 1) * DN_DK]
        r = [[jnp.concatenate([half(k[i], j) * (beta_h[i][j] * eg_h[i][j]), half(v[i], j) * beta_h[i][j]], axis=1)
              for j in range(2)] for i in n]
        wu = [jnp.concatenate(r[i], axis=1) + mm(b16(e[i]), blockdiag_wide(*r[i])) for i in n]
        wu_h = [[wu[i][:, j * w2:(j + 1) * w2] for j in range(2)] for i in n]
        qo = [mm(b16(qk[i] * dmat[i]), blockdiag_wide(*wu_h[i])) for i in n]
        units = [(i, j) for i in n for j in range(2)]
        kd = [b16(half(k[i], j) * col(chunk[pairs[i][0], pairs[i][1]][2], lane_g(pairs[i][0], heads[i][j])))
              for i, j in units]
        pb = [lax.dot_general(kd[m], b16(wu_h[i][j]), tn_dims, preferred_element_type=F32)
              for m, (i, j) in enumerate(units)]
        for m, (i, j) in enumerate(units):
            d, ci, h = pairs[i][0], pairs[i][1], heads[i][j]
            u = d * upd + ci * DN_HEADS + h
            qo_h = qo[i][:, j * w2:(j + 1) * w2]
            pq_scr[u, 0:DN_DK, :] = b16(pb[m][:, :DN_DK])
            pq_scr[u, DN_DK:DN_DK + c, :] = b16(half(q[i], j) * eg_h[i][j] - qo_h[:, :DN_DK])
            b_scr[u] = pb[m][:, DN_DK:]
            o_scr[u] = qo_h[:, DN_DK:]
            gt_scr[u] = jnp.broadcast_to(col(chunk[d, ci][3], lane_g(d, h)), (1, LANES))

    def scan(step):
        chains = [(0, step, h) for h in range(DN_HEADS)] + [(1, n_chunks - 1 - step, h) for h in range(DN_HEADS)]
        s = [s_ref[d * DN_HEADS + h] for d, ci, h in chains]
        ps = [mm(pq_scr[d * upd + ci * DN_HEADS + h], b16(s[i])) for i, (d, ci, h) in enumerate(chains)]
        for i, (d, ci, h) in enumerate(chains):
            u = d * upd + ci * DN_HEADS + h
            s_ref[d * DN_HEADS + h] = s[i] * gt_scr[u] - ps[i][0:DN_DK, :] + b_scr[u]
            dirs[d][4][ci * c:(ci + 1) * c, h * DN_DV:(h + 1) * DN_DV] = ps[i][DN_DK:DN_DK + c, :] + o_scr[u]

    prep()
    for step in range(n_chunks):
        scan(step)

    @pl.when((flags & DN_LAST_EMIT) != 0)
    def _():
        for d in range(2):
            for h in range(DN_HEADS):
                sfin_ref[0, d, h] = s_ref[d * DN_HEADS + h]


def _deltanet_schedule(batch, seq, dec_batch, dec_seq, tile):
    tf, tb, flags, s0i, sfi = [], [], [], [], []
    t0 = 0
    for n_seq, length, is_ctx in ((batch, seq, True), (dec_batch, dec_seq, False)):
        tps = length // tile
        for b in range(n_seq):
            for t in range(tps):
                tf.append(t0 + b * tps + t)
                tb.append(t0 + b * tps + tps - 1 - t)
                flags.append((DN_FIRST if t == 0 else 0)
                             | (DN_LAST_EMIT if (is_ctx and t == tps - 1) else 0)
                             | (DN_ZERO_INIT if is_ctx else 0))
                s0i.append(0 if is_ctx else b)
                sfi.append(b if is_ctx else batch - 1)
        t0 += n_seq * tps
    return [jnp.asarray(a, jnp.int32) for a in (tf, tb, flags, s0i, sfi)]


def _deltanet(q, k, v, gates, s0, layer, batch, seq, dec_batch, dec_seq):
    tile = MIX_TILE
    sched = _deltanet_schedule(batch, seq, dec_batch, dec_seq, tile)
    fwd = lambda w: pl.BlockSpec((tile, w), lambda s, tf, tb, fl, s0i, sfi: (tf[s], 0))
    bwd = lambda w: pl.BlockSpec((tile, w), lambda s, tf, tb, fl, s0i, sfi: (tb[s], 0))
    state_block = (2, DN_HEADS, DN_DK, DN_DV)
    units = 2 * (tile // DN_CHUNK) * DN_HEADS
    grid_spec = pltpu.PrefetchScalarGridSpec(
        num_scalar_prefetch=len(sched),
        grid=(sched[0].shape[0],),
        in_specs=[fwd(QK_W), fwd(QK_W), fwd(DN_W), fwd(LANES), bwd(QK_W), bwd(QK_W), bwd(DN_W), bwd(LANES),
                  pl.BlockSpec((1, 1) + state_block,
                               lambda s, tf, tb, fl, s0i, sfi: (s0i[s], layer, 0, 0, 0, 0))],
        out_specs=[fwd(DN_W), bwd(DN_W),
                   pl.BlockSpec((1,) + state_block, lambda s, tf, tb, fl, s0i, sfi: (sfi[s], 0, 0, 0, 0))],
        scratch_shapes=[pltpu.VMEM((2 * DN_HEADS, DN_DK, DN_DV), F32),
                        pltpu.VMEM((units, DN_DK + DN_CHUNK, DN_DV), BF16),
                        pltpu.VMEM((units, DN_DK, DN_DV), F32),
                        pltpu.VMEM((units, DN_CHUNK, DN_DV), F32),
                        pltpu.VMEM((units, 1, LANES), F32)],
    )
    return pl.pallas_call(
        functools.partial(_deltanet_kernel, tile=tile),
        grid_spec=grid_spec,
        out_shape=[jax.ShapeDtypeStruct((q.shape[0], DN_W), F32)] * 2
        + [jax.ShapeDtypeStruct((batch,) + state_block, F32)],
        compiler_params=pltpu.CompilerParams(dimension_semantics=("arbitrary",), vmem_limit_bytes=VMEM_LIMIT),
        name="deltanet_scan",
    )(*sched, q, k, v, gates, q, k, v, gates, s0)


def _outproj_kernel(*refs, tile, n_ctx, dec_seq):
    (*x_refs, ya_ref, of_ref, ob_ref, z_ref, yc_ref, mod_ref, wout_ref, vec_ref, dng_ref,
     wr_hi_ref, wr_lo_ref, br_ref, x1_ref, h_ref, route_ref, cnt_ref, run_ref) = refs
    start = pl.program_id(0) * tile
    m = _mod_row(mod_ref, start, n_ctx, dec_seq)
    gt1 = m[:, 2 * D_MODEL:3 * D_MODEL]
    sh2 = m[:, 3 * D_MODEL:4 * D_MODEL]
    sc2 = m[:, 4 * D_MODEL:5 * D_MODEL]

    @pl.when(pl.program_id(0) == 0)
    def _():
        run_ref[...] = jnp.zeros_like(run_ref)

    rows_per_part = tile // OUTPROJ_PARTS
    lane = lax.broadcasted_iota(jnp.int32, (rows_per_part, LANES), 1).astype(F32)
    ri = lax.broadcasted_iota(jnp.int32, (rows_per_part, rows_per_part), 0)
    rj = lax.broadcasted_iota(jnp.int32, (rows_per_part, rows_per_part), 1)
    earlier = (ri > rj).astype(BF16)
    parts = []
    for part in range(OUTPROJ_PARTS):
        rows = slice(part * rows_per_part, (part + 1) * rows_per_part)
        o = of_ref[rows, :] + ob_ref[rows, :]
        z = z_ref[rows, :]
        zz = z * _sigmoid(z)
        hs = []
        for h in range(DN_HEADS):
            oh = o[:, h * DN_DV:(h + 1) * DN_DV]
            on = oh * lax.rsqrt(jnp.mean(oh * oh, axis=-1, keepdims=True) + EPS) * dng_ref[...]
            hs.append(on * zz[:, h * DN_DV:(h + 1) * DN_DV])
        ycat = jnp.concatenate([ya_ref[rows, :]] + hs + [yc_ref[rows, :]], axis=-1).astype(BF16)
        y = jnp.dot(ycat, wout_ref[...], preferred_element_type=F32)
        x = _load_tokens(x_refs, rows, start < n_ctx)
        x1 = _ln(DEEPNORM_ALPHA * x + gt1 * y) * vec_ref[0:1, :] + vec_ref[1:2, :]
        x1_ref[rows, :] = x1
        hf = _ln(x1) * (1.0 + sc2) + sh2
        h_hi = hf.astype(BF16)
        h_ref[rows, :] = h_hi
        logits = (jnp.dot(h_hi, wr_hi_ref[...], preferred_element_type=F32)
                  + jnp.dot(h_hi, wr_lo_ref[...], preferred_element_type=F32)
                  + br_ref[...])
        vals, idxs = [], []
        for _ in range(TOP_K):
            top = jnp.max(logits, axis=-1, keepdims=True)
            idx = jnp.min(jnp.where(logits == top, lane, float(LANES)), axis=-1, keepdims=True)
            vals.append(top)
            idxs.append(idx)
            logits = jnp.where(lane == idx, -jnp.inf, logits)
        es = [jnp.exp(v - vals[0]) for v in vals]
        den = (es[0] + es[1]) + (es[2] + es[3])
        onehot = jnp.zeros((rows_per_part, LANES), F32)
        for idx in idxs:
            onehot = onehot + (lane == idx).astype(F32)
        before = jnp.dot(earlier, onehot.astype(BF16), preferred_element_type=F32)
        parts.append((rows, idxs, [e / den for e in es], before, jnp.sum(onehot, axis=0, keepdims=True)))

    run = run_ref[...]
    for rows, idxs, gates, before, count in parts:
        before = before + run
        run = run + count
        route = jnp.zeros((rows_per_part, LANES), F32)
        for j in range(TOP_K):
            rank = jnp.sum(jnp.where(lane == idxs[j], before, 0.0), axis=-1, keepdims=True)
            route = jnp.where(lane == float(j), idxs[j], route)
            route = jnp.where(lane == float(TOP_K + j), gates[j], route)
            route = jnp.where(lane == float(2 * TOP_K + j), rank, route)
        route_ref[rows, :] = route
    run_ref[...] = run
    cnt_ref[...] = jnp.broadcast_to(run, cnt_ref.shape)


def _out_projection(x_src, ya, o_f, o_b, proj, yc, mod, lp, n_ctx, dec_seq):
    nt, d = ya.shape[0], x_src[0].shape[1]
    tile = TOK_TILE
    tok = lambda w: pl.BlockSpec((tile, w), lambda i: (i, 0))
    full = lambda a: pl.BlockSpec(a.shape, lambda i: (0,) * a.ndim)
    return pl.pallas_call(
        functools.partial(_outproj_kernel, tile=tile, n_ctx=n_ctx, dec_seq=dec_seq),
        grid=(nt // tile,),
        in_specs=_token_specs(x_src, tile, n_ctx) + [
            tok(CONV_W), tok(DN_W), tok(DN_W),
            pl.BlockSpec((tile, DN_W), lambda i: (i, COL_Z // DN_W)),
            tok(GMLP_W), full(mod), full(lp["w_out"]), full(lp["ln1"]), full(lp["dn_norm_g"]),
            full(lp["wr_hi"]), full(lp["wr_lo"]), full(lp["b_router"])],
        out_specs=[tok(d), tok(d), tok(LANES), pl.BlockSpec((8, LANES), lambda i: (0, 0))],
        out_shape=[jax.ShapeDtypeStruct((nt, d), F32), jax.ShapeDtypeStruct((nt, d), BF16),
                   jax.ShapeDtypeStruct((nt, LANES), F32), jax.ShapeDtypeStruct((8, LANES), F32)],
        scratch_shapes=[pltpu.VMEM((1, LANES), F32)],
        compiler_params=pltpu.CompilerParams(dimension_semantics=("arbitrary",), vmem_limit_bytes=VMEM_LIMIT),
        name="out_projection_router",
    )(*x_src, ya, o_f, o_b, proj, yc, mod, lp["w_out"], lp["ln1"], lp["dn_norm_g"],
      lp["wr_hi"], lp["wr_lo"], lp["b_router"])


def _expert_kernel(be_ref, nu_ref, x_ref, wgu_ref, bgu_ref, wd_ref, bd_ref, out_ref, wgu16, wd16):
    i = pl.program_id(0)

    @pl.when((i == 0) | (be_ref[i] != be_ref[jnp.maximum(i - 1, 0)]))
    def _():
        wgu16[...] = wgu_ref[0, 0].astype(BF16)
        wd16[...] = wd_ref[0, 0].astype(BF16)

    @pl.when(i < nu_ref[0])
    def _():
        gu = jnp.dot(x_ref[...], wgu16[...], preferred_element_type=F32) + bgu_ref[0, 0]
        gate = jnp.minimum(gu[:, :D_FF], SWIGLU_LIMIT)
        up = jnp.clip(gu[:, D_FF:], -SWIGLU_LIMIT, SWIGLU_LIMIT)
        act = gate * _sigmoid(SWIGLU_ALPHA * gate)
        hmid = ((up + 1.0) * act).astype(BF16)
        y = jnp.dot(hmid, wd16[...], preferred_element_type=F32) + bd_ref[0, 0]
        out_ref[...] = y.astype(out_ref.dtype)

    @pl.when(i >= nu_ref[0])
    def _():
        out_ref[...] = jnp.zeros_like(out_ref)


def _experts(xg, block_e, n_used, w_gu, b_gu, w_down, b_down, layer):
    m_pad, d = xg.shape
    bm = MOE_BM
    grid_spec = pltpu.PrefetchScalarGridSpec(
        num_scalar_prefetch=2,
        grid=(m_pad // bm,),
        in_specs=[
            pl.BlockSpec((bm, d), lambda i, be, nu: (i, 0)),
            pl.BlockSpec((1, 1, d, 2 * D_FF), lambda i, be, nu: (layer, be[i], 0, 0)),
            pl.BlockSpec((1, 1, 1, 2 * D_FF), lambda i, be, nu: (layer, be[i], 0, 0)),
            pl.BlockSpec((1, 1, D_FF, d), lambda i, be, nu: (layer, be[i], 0, 0)),
            pl.BlockSpec((1, 1, 1, d), lambda i, be, nu: (layer, be[i], 0, 0)),
        ],
        out_specs=pl.BlockSpec((bm, d), lambda i, be, nu: (i, 0)),
        scratch_shapes=[pltpu.VMEM((d, 2 * D_FF), BF16), pltpu.VMEM((D_FF, d), BF16)],
    )
    return pl.pallas_call(
        _expert_kernel,
        grid_spec=grid_spec,
        out_shape=jax.ShapeDtypeStruct((m_pad, d), BF16),
        compiler_params=pltpu.CompilerParams(dimension_semantics=("arbitrary",), vmem_limit_bytes=VMEM_LIMIT),
        name="moe_experts",
    )(block_e, n_used, xg, w_gu, b_gu, w_down, b_down)


def _combine_kernel(yg_ref, route_ref, x1_ref, mod_ref, vec_ref, *out_refs, tile, n_ctx, dec_seq):
    start = pl.program_id(0) * tile
    m = _mod_row(mod_ref, start, n_ctx, dec_seq)
    gt2 = m[:, 5 * D_MODEL:6 * D_MODEL]
    gate = lambda j: route_ref[:, TOP_K + j:TOP_K + j + 1]
    ye = lambda j: yg_ref[j].astype(F32) * gate(j)
    y = (ye(0) + ye(1)) + (ye(2) + ye(3))
    res = _ln(DEEPNORM_ALPHA * x1_ref[...] + gt2 * y) * vec_ref[0:1, :] + vec_ref[1:2, :]
    if len(out_refs) == 1:
        out_refs[0][...] = res
    else:
        @pl.when(start < n_ctx)
        def _():
            out_refs[0][...] = res

        @pl.when(start >= n_ctx)
        def _():
            out_refs[1][...] = res


def _combine(yg, route, x1, mod, ln2, n_ctx, dec_seq, split_streams):
    nt, d = x1.shape
    tile = TOK_TILE
    ct = n_ctx // tile
    if split_streams:
        out_specs = [pl.BlockSpec((tile, d), lambda i: (jnp.minimum(i, ct - 1), 0)),
                     pl.BlockSpec((tile, d), lambda i: (jnp.maximum(i - ct, 0), 0))]
        out_shape = [jax.ShapeDtypeStruct((n_ctx, d), F32), jax.ShapeDtypeStruct((nt - n_ctx, d), F32)]
    else:
        out_specs = pl.BlockSpec((tile, d), lambda i: (i, 0))
        out_shape = jax.ShapeDtypeStruct((nt, d), F32)
    return pl.pallas_call(
        functools.partial(_combine_kernel, tile=tile, n_ctx=n_ctx, dec_seq=dec_seq),
        grid=(nt // tile,),
        in_specs=[pl.BlockSpec((TOP_K, tile, d), lambda i: (0, i, 0)),
                  pl.BlockSpec((tile, LANES), lambda i: (i, 0)),
                  pl.BlockSpec((tile, d), lambda i: (i, 0)),
                  pl.BlockSpec(mod.shape, lambda i: (0, 0)),
                  pl.BlockSpec(ln2.shape, lambda i: (0, 0))],
        out_specs=out_specs,
        out_shape=out_shape,
        compiler_params=pltpu.CompilerParams(dimension_semantics=("arbitrary",), vmem_limit_bytes=VMEM_LIMIT),
        name="moe_combine_ln",
    )(yg, route, x1, mod, ln2)


def _route(route, counts):
    nt = route.shape[0]
    nk = nt * TOP_K
    bm = MOE_BM
    expert = route[:, 0:TOP_K].astype(jnp.int32)
    rank = route[:, 2 * TOP_K:3 * TOP_K].astype(jnp.int32)
    counts = counts[0, :N_EXPERTS].astype(jnp.int32)
    padded = (counts + bm - 1) // bm * bm
    pad_end = jnp.cumsum(padded)
    pad_start = pad_end - padded
    grp_start = jnp.cumsum(counts) - counts
    pair_slot = pad_start[expert] + rank
    tok = jnp.arange(nk, dtype=jnp.int32) // TOP_K
    _, sorted_tok = lax.sort((pair_slot.reshape(-1), tok), num_keys=1)
    n_blocks = nk // bm + N_EXPERTS
    blk_start = jnp.arange(n_blocks, dtype=jnp.int32) * bm
    block_e = jnp.minimum(jnp.sum((pad_end[None, :] <= blk_start[:, None]).astype(jnp.int32), axis=1),
                          N_EXPERTS - 1)
    within = jnp.arange(bm, dtype=jnp.int32)[None, :]
    j = (blk_start - pad_start[block_e])[:, None] + within
    src = jnp.clip(grp_start[block_e][:, None] + j, 0, nk - 1)
    slot = blk_start[:, None] + within
    slot_tok = jnp.where(j < counts[block_e][:, None], sorted_tok[src], slot % nt).reshape(-1)
    n_used = (pad_end[-1] // bm).astype(jnp.int32).reshape(1)
    return slot_tok, pair_slot, block_e.astype(jnp.int32), n_used


def _grid_pos_embed(t, d):
    rows = t // GRID_W
    r, col = jnp.meshgrid(jnp.arange(rows), jnp.arange(GRID_W), indexing="ij")
    r = r.reshape(-1).astype(F32)[:, None]
    col = col.reshape(-1).astype(F32)[:, None]
    n_freq = d // 4
    omega = 1.0 / (POS_BASE ** (jnp.arange(n_freq, dtype=F32) / n_freq))
    return jnp.concatenate([jnp.sin(r * omega), jnp.cos(r * omega),
                            jnp.sin(col * omega), jnp.cos(col * omega)], axis=-1)


def _pad_lanes(a, offset):
    return jnp.zeros((1, LANES), F32).at[0, offset:offset + a.shape[0]].set(a.astype(F32))


def _layer_params(l, w_in, conv_dw, conv_b, conv_ln_g, conv_ln_b, dn_conv, dn_a_log, dn_dt_bias, dn_norm_g,
                  gm_ln_g, gm_ln_b, gm_ws, gm_bs, w_out, ln1_g, ln1_b, ln2_g, ln2_b, w_router, b_router):
    wi = w_in[l]
    c_conv, c_qkv, c_z, c_ba = 0, 2 * CONV_W, 2 * CONV_W + 2 * QK_W + DN_W, 2 * CONV_W + 2 * QK_W + 2 * DN_W
    c_gm = c_ba + 4 * DN_HEADS
    w_in_r = jnp.concatenate([
        wi[:, c_qkv:c_z], wi[:, c_conv:c_qkv], wi[:, c_z:c_ba], wi[:, c_gm:],
        wi[:, c_ba:c_gm], jnp.zeros((D_MODEL, LANES - 4 * DN_HEADS), F32)], axis=1).astype(BF16)
    grp = jnp.arange(CONV_W) // GROUP_W
    gavg = (grp[:, None] == grp[None, :]).astype(BF16) * (1.0 / GROUP_W)
    gvec = jnp.concatenate([_pad_lanes(dn_a_log[l].reshape(-1), 2 * DN_HEADS),
                            _pad_lanes(dn_dt_bias[l].reshape(-1), 2 * DN_HEADS)], axis=0)
    wr = jnp.zeros((D_MODEL, LANES), F32).at[:, :N_EXPERTS].set(w_router[l])
    wr_hi = wr.astype(BF16)
    wr_lo = (wr - wr_hi.astype(F32)).astype(BF16)
    return {
        "w_in_r": w_in_r,
        "conv_dw": conv_dw[l],
        "conv_p": jnp.stack([conv_b[l], conv_ln_g[l], conv_ln_b[l]]),
        "dn_conv": dn_conv[l],
        "gvec": gvec,
        "gm_p": jnp.stack([gm_ln_g[l], gm_ln_b[l]]),
        "gm_ws": gm_ws[l],
        "gm_bsf": jnp.repeat(jnp.transpose(gm_bs[l]), GROUP_W, axis=1),
        "gavg": gavg,
        "w_out": w_out[l].astype(BF16),
        "ln1": jnp.stack([ln1_g[l], ln1_b[l]]),
        "ln2": jnp.stack([ln2_g[l], ln2_b[l]]),
        "dn_norm_g": dn_norm_g[l].reshape(1, DN_DV),
        "wr_hi": wr_hi,
        "wr_lo": wr_lo,
        "b_router": jnp.full((1, LANES), -1e30, F32).at[0, :N_EXPERTS].set(b_router[l]),
    }


def kernel(x_prompt, x_sample, state_delta, c, c_ctx, w_ada, b_ada, w_in, conv_dw, conv_b, conv_ln_g, conv_ln_b, dn_conv, dn_a_log, dn_dt_bias, dn_norm_g, gm_ln_g, gm_ln_b, gm_ws, gm_bs, w_out, ln1_g, ln1_b, ln2_g, ln2_b, w_router, b_router, w_gu, b_gu, w_down, b_down):
    batch, seq, d = x_prompt.shape
    dec_batch, dec_seq, _ = x_sample.shape
    n_ctx = batch * seq
    n_den = dec_batch * dec_seq
    depth = w_in.shape[0]

    cond = jnp.zeros((16, d), F32).at[0].set(c_ctx).at[1:1 + dec_batch].set(c)
    mod_all = _modulation(cond, w_ada, b_ada)
    b_gu_r = b_gu.reshape(depth, N_EXPERTS, 1, 2 * D_FF)
    b_down_r = b_down.reshape(depth, N_EXPERTS, 1, d)

    nt = n_ctx + n_den
    x_src = (x_prompt.reshape(n_ctx, d), x_sample.reshape(n_den, d), _grid_pos_embed(dec_seq, d))
    ctx_states = []
    for l in range(depth):
        lp = _layer_params(l, w_in, conv_dw, conv_b, conv_ln_g, conv_ln_b, dn_conv, dn_a_log, dn_dt_bias,
                           dn_norm_g, gm_ln_g, gm_ln_b, gm_ws, gm_bs, w_out, ln1_g, ln1_b, ln2_g, ln2_b,
                           w_router, b_router)
        mod = mod_all[l]
        proj = _in_projection(x_src, nt, mod, lp["w_in_r"], n_ctx, dec_seq)
        ya, yc, q, k, v, gates = _local_mixers(proj, lp, n_ctx // MIX_TILE, seq // MIX_TILE, dec_seq // MIX_TILE)
        o_f, o_b, s_fin = _deltanet(q, k, v, gates, state_delta, l, batch, seq, dec_batch, dec_seq)
        ctx_states.append(s_fin)
        x1, hffn, route, counts = _out_projection(x_src, ya, o_f, o_b, proj, yc, mod, lp, n_ctx, dec_seq)
        slot_tok, pair_slot, block_e, n_used = _route(route, counts)
        xg = hffn.at[slot_tok].get(mode="promise_in_bounds")
        yb = _experts(xg, block_e, n_used, w_gu, b_gu_r, w_down, b_down_r, l)
        yg = yb.at[jnp.transpose(pair_slot)].get(mode="promise_in_bounds")
        out = _combine(yg, route, x1, mod, lp["ln2"], n_ctx, dec_seq, split_streams=(l == depth - 1))
        x_src = (out,)

    new_state = jnp.stack(ctx_states, axis=1).astype(x_prompt.dtype)
    return (out[0].reshape(batch, seq, d), out[1].reshape(dec_batch, dec_seq, d), new_state)
```

```python
import functools

import jax
import jax.numpy as jnp
from jax import lax
from jax.experimental import pallas as pl
from jax.experimental.pallas import tpu as pltpu

F32 = jnp.float32
BF16 = jnp.bfloat16

D_MODEL = 1024
DEPTH = 2
GRID_W = 64
POS_BASE = 10000.0
CONV_W = 256
CONV_K = 31
CONV_HALO = 16
DN_HEADS = 4
DN_DK = 128
DN_DV = 128
QK_W = DN_HEADS * DN_DK
DN_W = DN_HEADS * DN_DV
DN_SHORT_K = 5
DN_HALO = 8
DN_CHUNK = 64
GMLP_W = 256
GMLP_GROUPS = 4
GMLP_CHUNK = 128
GROUP_W = 64
N_EXPERTS = 32
TOP_K = 4
D_FF = D_MODEL
SWIGLU_ALPHA = 1.702
SWIGLU_LIMIT = 7.0
DEEPNORM_ALPHA = (2 * DEPTH) ** 0.25
EPS = 1e-6

LANES = 128
SUBLANES = 8
COL_QKV = 0
COL_CONV = 2 * QK_W + DN_W
COL_Z = COL_CONV + 2 * CONV_W
COL_GM = COL_Z + DN_W
COL_BA = COL_GM + 2 * GMLP_W
PROJ_W = COL_BA + LANES

MIX_TILE = 256
TOK_TILE = 512
OUTPROJ_PARTS = 2
MOE_BM = 512
TOKEN_BITS = 16
VMEM_LIMIT = 56 * 1024 * 1024


def _ln(x):
    mu = jnp.mean(x, axis=-1, keepdims=True)
    xc = x - mu
    return xc * lax.rsqrt(jnp.mean(xc * xc, axis=-1, keepdims=True) + EPS)


def _sigmoid(x):
    return jax.nn.sigmoid(x)


def _split_bf16(x, parts):
    out = []
    r = x
    for _ in range(parts):
        p = r.astype(BF16)
        out.append(p)
        r = r - p.astype(F32)
    return out


def _dot_exact_rhs(x, m_bf16, parts=3):
    acc = None
    for p in _split_bf16(x, parts):
        t = jnp.dot(p, m_bf16, preferred_element_type=F32)
        acc = t if acc is None else acc + t
    return acc


def _dot_exact_lhs(m_bf16, x, parts=3):
    acc = None
    for p in _split_bf16(x, parts):
        t = jnp.dot(m_bf16, p, preferred_element_type=F32)
        acc = t if acc is None else acc + t
    return acc


def _mod_row(mod_ref, start, n_ctx, dec_seq):
    row = jnp.where(start < n_ctx, 0, 1 + (start - n_ctx) // dec_seq)
    return mod_ref[pl.ds(row, 1), :]


def _mod_kernel(cond_ref, w_ref, b_ref, out_ref):
    c = cond_ref[...]
    s = c * _sigmoid(c)
    out_ref[0] = jnp.dot(s, w_ref[0], preferred_element_type=F32) + b_ref[0]


def _modulation(cond, w_ada, b_ada):
    nl, d, n = w_ada.shape
    r = cond.shape[0]
    tn = 1024
    return pl.pallas_call(
        _mod_kernel,
        grid=(nl, n // tn),
        in_specs=[
            pl.BlockSpec((r, d), lambda l, j: (0, 0)),
            pl.BlockSpec((1, d, tn), lambda l, j: (l, 0, j)),
            pl.BlockSpec((1, 1, tn), lambda l, j: (l, 0, j)),
        ],
        out_specs=pl.BlockSpec((1, r, tn), lambda l, j: (l, 0, j)),
        out_shape=jax.ShapeDtypeStruct((nl, r, n), F32),
        name="adaln_modulation",
    )(cond, w_ada, b_ada.reshape(nl, 1, n))


def _token_specs(x_src, tile, n_ctx):
    d = x_src[0].shape[1]
    if len(x_src) == 1:
        return [pl.BlockSpec((tile, d), lambda i: (i, 0))]
    ct = n_ctx // tile
    pt = x_src[2].shape[0] // tile
    return [pl.BlockSpec((tile, d), lambda i: (jnp.minimum(i, max(ct - 1, 0)), 0)),
            pl.BlockSpec((tile, d), lambda i: (jnp.maximum(i - ct, 0), 0)),
            pl.BlockSpec((tile, d), lambda i: (jnp.maximum(i - ct, 0) % pt, 0))]


def _load_tokens(x_refs, rows, is_ctx_tile):
    if len(x_refs) == 1:
        return x_refs[0][rows, :]
    xp_ref, xs_ref, pos_ref = x_refs
    return jnp.where(is_ctx_tile, xp_ref[rows, :], xs_ref[rows, :] + pos_ref[rows, :])


def _inproj_kernel(*refs, tile, n_ctx, dec_seq):
    *x_refs, mod_ref, w_ref, out_ref = refs
    start = pl.program_id(0) * tile
    m = _mod_row(mod_ref, start, n_ctx, dec_seq)
    sh1 = m[:, 0:D_MODEL]
    sc1 = m[:, D_MODEL:2 * D_MODEL]
    h = _ln(_load_tokens(x_refs, slice(None), start < n_ctx)) * (1.0 + sc1) + sh1
    out_ref[...] = jnp.dot(h.astype(BF16), w_ref[...], preferred_element_type=F32)


def _in_projection(x_src, nt, mod, w_in_r, n_ctx, dec_seq):
    tile = TOK_TILE
    return pl.pallas_call(
        functools.partial(_inproj_kernel, tile=tile, n_ctx=n_ctx, dec_seq=dec_seq),
        grid=(nt // tile,),
        in_specs=_token_specs(x_src, tile, n_ctx) + [
            pl.BlockSpec(mod.shape, lambda i: (0, 0)),
            pl.BlockSpec(w_in_r.shape, lambda i: (0, 0)),
        ],
        out_specs=pl.BlockSpec((tile, PROJ_W), lambda i: (i, 0)),
        out_shape=jax.ShapeDtypeStruct((nt, PROJ_W), F32),
        compiler_params=pltpu.CompilerParams(vmem_limit_bytes=VMEM_LIMIT),
        name="in_projection",
    )(*x_src, mod, w_in_r)


def _group_norm(x, gavg):
    mean = _dot_exact_rhs(x, gavg, parts=2)
    xc = x - mean
    var = _dot_exact_rhs(xc * xc, gavg, parts=2)
    return xc * lax.rsqrt(var + EPS)


def _mixpre_kernel(qkv_ref, qkv_p_ref, qkv_n_ref, cv_ref, cv_p_ref, cv_n_ref, gm_ref, ba_ref,
                   convw_ref, convp_ref, dnw_ref, gvec_ref, gmp_ref, ws_ref, bsf_ref, gavg_ref,
                   ya_ref, yc_ref, q_ref, k_ref, v_ref, gates_ref, cbuf, cshift, qbuf,
                   *, tile, n_ctx_tiles, ctx_tps, den_tps):
    i = pl.program_id(0)
    pos = jnp.where(i < n_ctx_tiles, i % ctx_tps, (i - n_ctx_tiles) % den_tps)
    tps = jnp.where(i < n_ctx_tiles, ctx_tps, den_tps)
    first = pos == 0
    last = pos == tps - 1
    gavg = gavg_ref[...]

    def glu(p):
        return p[:, :CONV_W] * _sigmoid(p[:, CONV_W:])

    cbuf[0:CONV_HALO, :] = jnp.where(first, 0.0, glu(cv_p_ref[...]))
    cbuf[CONV_HALO:CONV_HALO + tile, :] = glu(cv_ref[...])
    cbuf[CONV_HALO + tile:2 * CONV_HALO + tile, :] = jnp.where(last, 0.0, glu(cv_n_ref[...]))
    conv_b = convp_ref[0:1, :]
    conv_g = convp_ref[1:2, :]
    conv_beta = convp_ref[2:3, :]
    rc = 64
    off = CONV_HALO - CONV_K // 2
    span = tile + 2 * CONV_HALO - SUBLANES
    for b in range(1, SUBLANES):
        cshift[b - 1] = cbuf[b:b + span, :]
    for c in range(tile // rc):
        acc = jnp.zeros((rc, CONV_W), F32)
        for k in range(CONV_K):
            a, b = divmod(off + k, SUBLANES)
            r0 = c * rc + a * SUBLANES
            win = cbuf[r0:r0 + rc, :] if b == 0 else cshift[b - 1, r0:r0 + rc, :]
            acc = acc + win * convw_ref[k:k + 1, :]
        y = _group_norm(acc + conv_b, gavg) * conv_g + conv_beta
        ya_ref[c * rc:(c + 1) * rc, :] = y * _sigmoid(y)

    qbuf[0:DN_HALO, :] = jnp.where(first, 0.0, qkv_p_ref[...])
    qbuf[DN_HALO:DN_HALO + tile, :] = qkv_ref[...]
    qbuf[DN_HALO + tile:2 * DN_HALO + tile, :] = jnp.where(last, 0.0, qkv_n_ref[...])
    rq = 32
    offq = DN_HALO - DN_SHORT_K // 2
    outs = (q_ref, k_ref, v_ref)
    for part in range(3):
        c0 = part * QK_W
        for c in range(tile // rq):
            acc = jnp.zeros((rq, QK_W), F32)
            for k in range(DN_SHORT_K):
                acc = acc + (qbuf[c * rq + offq + k:c * rq + offq + k + rq, c0:c0 + QK_W]
                             * dnw_ref[k:k + 1, c0:c0 + QK_W])
            a = acc * _sigmoid(acc)
            if part < 2:
                scale = DN_DK ** -0.5 if part == 0 else 1.0
                hs = []
                for h in range(DN_HEADS):
                    ah = a[:, h * DN_DK:(h + 1) * DN_DK]
                    nrm = lax.rsqrt(jnp.sum(ah * ah, axis=-1, keepdims=True) + EPS)
                    hs.append(ah * (nrm * scale))
                a = jnp.concatenate(hs, axis=-1)
            outs[part][c * rq:(c + 1) * rq, :] = a

    p = ba_ref[...]
    beta = _sigmoid(p)
    xg = p + gvec_ref[1:2, :]
    softplus = jnp.maximum(xg, 0.0) + jnp.log1p(jnp.exp(-jnp.abs(xg)))
    g = -jnp.exp(gvec_ref[0:1, :]) * softplus
    lane = lax.broadcasted_iota(jnp.int32, p.shape, 1)
    gates_ref[...] = jnp.where(lane < 2 * DN_HEADS, beta, g)

    pg = gm_ref[...]
    ge = pg * (0.5 * (1.0 + jnp.tanh(0.7978845608028654 * (pg + 0.044715 * (pg * pg * pg)))))
    u = ge[:, :GMLP_W]
    vn = _group_norm(ge[:, GMLP_W:], gavg) * gmp_ref[0:1, :] + gmp_ref[1:2, :]
    grp = lax.broadcasted_iota(jnp.int32, (GMLP_CHUNK, GMLP_W), 1) // GROUP_W
    for n in range(tile // GMLP_CHUNK):
        vchunk = vn[n * GMLP_CHUNK:(n + 1) * GMLP_CHUNK, :]
        sg = bsf_ref[...]
        for gi in range(GMLP_GROUPS):
            r = jnp.dot(ws_ref[gi], vchunk, preferred_element_type=F32)
            sg = sg + jnp.where(grp == gi, r, 0.0)
        yc_ref[n * GMLP_CHUNK:(n + 1) * GMLP_CHUNK, :] = u[n * GMLP_CHUNK:(n + 1) * GMLP_CHUNK, :] * sg


def _local_mixers(proj, lp, n_ctx_tiles, ctx_tps, den_tps):
    nt = proj.shape[0]
    tile = MIX_TILE
    n_tiles = nt // tile
    cpb = tile // CONV_HALO
    qpb = tile // DN_HALO
    n_cblk = nt // CONV_HALO
    n_qblk = nt // DN_HALO
    col = lambda c, w: c // w
    full = lambda a: pl.BlockSpec(a.shape, lambda i: (0,) * a.ndim)
    in_specs = [
        pl.BlockSpec((tile, 3 * QK_W), lambda i: (i, col(COL_QKV, 3 * QK_W))),
        pl.BlockSpec((DN_HALO, 3 * QK_W), lambda i: (jnp.maximum(i * qpb - 1, 0), 0)),
        pl.BlockSpec((DN_HALO, 3 * QK_W), lambda i: (jnp.minimum((i + 1) * qpb, n_qblk - 1), 0)),
        pl.BlockSpec((tile, 2 * CONV_W), lambda i: (i, col(COL_CONV, 2 * CONV_W))),
        pl.BlockSpec((CONV_HALO, 2 * CONV_W), lambda i: (jnp.maximum(i * cpb - 1, 0), col(COL_CONV, 2 * CONV_W))),
        pl.BlockSpec((CONV_HALO, 2 * CONV_W),
                     lambda i: (jnp.minimum((i + 1) * cpb, n_cblk - 1), col(COL_CONV, 2 * CONV_W))),
        pl.BlockSpec((tile, 2 * GMLP_W), lambda i: (i, col(COL_GM, 2 * GMLP_W))),
        pl.BlockSpec((tile, LANES), lambda i: (i, col(COL_BA, LANES))),
        full(lp["conv_dw"]), full(lp["conv_p"]), full(lp["dn_conv"]), full(lp["gvec"]),
        full(lp["gm_p"]), full(lp["gm_ws"]), full(lp["gm_bsf"]), full(lp["gavg"]),
    ]
    tok = lambda w: pl.BlockSpec((tile, w), lambda i: (i, 0))
    shp = lambda w: jax.ShapeDtypeStruct((nt, w), F32)
    return pl.pallas_call(
        functools.partial(_mixpre_kernel, tile=tile, n_ctx_tiles=n_ctx_tiles, ctx_tps=ctx_tps, den_tps=den_tps),
        grid=(n_tiles,),
        in_specs=in_specs,
        out_specs=[tok(CONV_W), tok(GMLP_W), tok(QK_W), tok(QK_W), tok(DN_W), tok(LANES)],
        out_shape=[shp(CONV_W), shp(GMLP_W), shp(QK_W), shp(QK_W), shp(DN_W), shp(LANES)],
        scratch_shapes=[pltpu.VMEM((tile + 2 * CONV_HALO, CONV_W), F32),
                        pltpu.VMEM((SUBLANES - 1, tile + 2 * CONV_HALO - SUBLANES, CONV_W), F32),
                        pltpu.VMEM((tile + 2 * DN_HALO, 3 * QK_W), F32)],
        compiler_params=pltpu.CompilerParams(vmem_limit_bytes=VMEM_LIMIT),
        name="local_mixers",
    )(proj, proj, proj, proj, proj, proj, proj, proj,
      lp["conv_dw"], lp["conv_p"], lp["dn_conv"], lp["gvec"], lp["gm_p"], lp["gm_ws"], lp["gm_bsf"], lp["gavg"])


DN_FIRST, DN_LAST_EMIT, DN_ZERO_INIT = 1, 2, 4


def _deltanet_kernel(tf_ref, tb_ref, flag_ref, s0i_ref, sfi_ref,
                     qf_ref, kf_ref, vf_ref, gf_ref, qb_ref, kb_ref, vb_ref, gb_ref, s0_ref,
                     of_ref, ob_ref, sfin_ref, s_ref, pq_scr, b_scr, o_scr, gt_scr, *, tile):
    flags = flag_ref[pl.program_id(0)]
    first = (flags & DN_FIRST) != 0
    zero_init = (flags & DN_ZERO_INIT) != 0
    c = DN_CHUNK
    n_chunks = tile // c
    upd = n_chunks * DN_HEADS

    @pl.when(first & zero_init)
    def _():
        s_ref[...] = jnp.zeros_like(s_ref)

    @pl.when(first & jnp.logical_not(zero_init))
    def _():
        for d in range(2):
            for h in range(DN_HEADS):
                s_ref[d * DN_HEADS + h] = s0_ref[0, 0, d, h]

    ti = lax.broadcasted_iota(jnp.int32, (tile, tile), 0)
    tj = lax.broadcasted_iota(jnp.int32, (tile, tile), 1)
    same_chunk = (ti // c) == (tj // c)
    nt_dims = (((1,), (1,)), ((), ()))
    tn_dims = (((0,), (0,)), ((), ()))
    dirs = (
        (qf_ref, kf_ref, vf_ref, gf_ref, of_ref, ti >= tj, c - 1),
        (qb_ref, kb_ref, vb_ref, gb_ref, ob_ref, ti <= tj, 0),
    )

    def gate_context(d):
        g_ref, tile_tri = dirs[d][3], dirs[d][5]
        gates = g_ref[...]
        blockcum = (same_chunk & tile_tri).astype(BF16)
        gc_t = _dot_exact_lhs(blockcum, gates)
        return gates, gc_t, gc_t.T

    gate_ctx = [gate_context(0), gate_context(1)]

    b16 = lambda x: x.astype(BF16)
    mm = lambda x, y: jnp.dot(x, y, preferred_element_type=F32)

    pi = lax.broadcasted_iota(jnp.int32, (c, 2 * c), 0)
    plane = lax.broadcasted_iota(jnp.int32, (c, 2 * c), 1)
    pj = plane & (c - 1)
    left = plane < c
    psame = lambda s: (pi >> s) == (pj >> s)
    pm8 = psame(3)
    pl16 = psame(4) & ~pm8
    pl32 = psame(5) & ~psame(4)
    pl64 = ~psame(5)
    ptri = ((pi >= pj, pi > pj), (pi <= pj, pi < pj))

    def blockdiag(y):
        return b16(jnp.concatenate([jnp.where(left, y, 0.0), jnp.where(left, 0.0, y)], axis=0))

    def blockdiag_wide(ya, yb):
        z = jnp.zeros_like(ya)
        return b16(jnp.concatenate([jnp.concatenate([ya, z], axis=1), jnp.concatenate([z, yb], axis=1)], axis=0))

    def prep():
        pairs = [(d, ci, p) for d in range(2) for ci in range(n_chunks) for p in range(DN_HEADS // 2)]
        chunk = {}
        for d in range(2):
            gc_t, last_row = gate_ctx[d][1], dirs[d][6]
            for ci in range(n_chunks):
                r0 = ci * c
                gc_c = gc_t[r0:r0 + c, :]
                glast = gc_t[r0 + last_row:r0 + last_row + 1, :]
                chunk[d, ci] = (gc_c, jnp.exp(gc_c), jnp.exp(glast - gc_c), jnp.exp(glast))
        lane_b = lambda d, h: d * DN_HEADS + h
        lane_g = lambda d, h: 2 * DN_HEADS + d * DN_HEADS + h
        col = lambda x, l: x[:, l:l + 1]
        n = range(len(pairs))
        heads = [(2 * p, 2 * p + 1) for _, _, p in pairs]

        def gj_pair(d, ci, ha, hb):
            gc_tt = gate_ctx[d][2]
            grow = lambda h: gc_tt[lane_g(d, h):lane_g(d, h) + 1, (ci // 2) * 2 * c:(ci // 2 + 1) * 2 * c]
            ra, rb = grow(ha), grow(hb)
            if ci % 2 == 0:
                rb = pltpu.roll(rb, c, axis=1)
            else:
                ra = pltpu.roll(ra, c, axis=1)
            return jnp.where(left[0:1, :], ra, rb)

        beta_h = [[col(gate_ctx[d][0][ci * c:(ci + 1) * c, :], lane_b(d, h)) for h in heads[i]]
                  for i, (d, ci, _) in enumerate(pairs)]
        eg_h = [[col(chunk[d, ci][1], lane_g(d, h)) for h in heads[i]] for i, (d, ci, _) in enumerate(pairs)]
        beta = [jnp.where(left, beta_h[i][0], beta_h[i][1]) for i in n]
        dmat = [jnp.where(ptri[d][0], jnp.exp(jnp.minimum(
            jnp.where(left, col(chunk[d, ci][0], lane_g(d, heads[i][0])), col(chunk[d, ci][0], lane_g(d, heads[i][1])))
            - gj_pair(d, ci, *heads[i]), 0.0)), 0.0) for i, (d, ci, _) in enumerate(pairs)]
        w2 = 2 * DN_DK
        q = [dirs[d][0][ci * c:(ci + 1) * c, p * w2:(p + 1) * w2] for d, ci, p in pairs]
        k = [dirs[d][1][ci * c:(ci + 1) * c, p * w2:(p + 1) * w2] for d, ci, p in pairs]
        v = [dirs[d][2][ci * c:(ci + 1) * c, p * w2:(p + 1) * w2] for d, ci, p in pairs]
        k16 = [b16(x) for x in k]
        kbd = [blockdiag_wide(x[:, :DN_DK], x[:, DN_DK:]) for x in k]
        kk = [lax.dot_general(k16[i], kbd[i], nt_dims, preferred_element_type=F32) for i in n]
        qk = [lax.dot_general(b16(q[i]), kbd[i], nt_dims, preferred_element_type=F32) for i in n]
        a = [jnp.where(ptri[pairs[i][0]][1], beta[i] * kk[i] * dmat[i], 0.0) for i in n]
        dd = [jnp.where(pm8, a[i], 0.0) for i in n]
        dd16 = [b16(x) for x in dd]
        d2 = [mm(dd16[i], blockdiag(dd[i])) for i in n]
        d2bd = [blockdiag(x) for x in d2]
        d3 = [mm(dd16[i], d2bd[i]) for i in n]
        d4 = [mm(b16(d2[i]), d2bd[i]) for i in n]
        e = [d2[i] - dd[i] - d3[i] for i in n]
        t = [mm(b16(e[i]), blockdiag(d4[i])) for i in n]
        e = [e[i] + d4[i] + t[i] for i in n]
        for sel in (pl16, pl32, pl64):
            l = [jnp.where(sel, a[i], 0.0) for i in n]
            ly = [l[i] + mm(b16(e[i]), blockdiag(l[i])) for i in n]
            z = [mm(b16(ly[i]), blockdiag(e[i])) for i in n]
            e = [e[i] - ly[i] - z[i] for i in n]
        half = lambda x, j: x[:, j * DN_DK:(j + 1) * DN_DK]
        r = [[jnp.concatenate([half(k[i], j) * (beta_h[i][j] * eg_h[i][j]), half(v[i], j) * beta_h[i][j]], axis=1)
              for j in range(2)] for i in n]
        wu = [jnp.concatenate(r[i], axis=1) + mm(b16(e[i]), blockdiag_wide(*r[i])) for i in n]
        wu_h = [[wu[i][:, j * w2:(j + 1) * w2] for j in range(2)] for i in n]
        qo = [mm(b16(qk[i] * dmat[i]), blockdiag_wide(*wu_h[i])) for i in n]
        units = [(i, j) for i in n for j in range(2)]
        kd = [b16(half(k[i], j) * col(chunk[pairs[i][0], pairs[i][1]][2], lane_g(pairs[i][0], heads[i][j])))
              for i, j in units]
        pb = [lax.dot_general(kd[m], b16(wu_h[i][j]), tn_dims, preferred_element_type=F32)
              for m, (i, j) in enumerate(units)]
        for m, (i, j) in enumerate(units):
            d, ci, h = pairs[i][0], pairs[i][1], heads[i][j]
            u = d * upd + ci * DN_HEADS + h
            qo_h = qo[i][:, j * w2:(j + 1) * w2]
            pq_scr[u, 0:DN_DK, :] = b16(pb[m][:, :DN_DK])
            pq_scr[u, DN_DK:DN_DK + c, :] = b16(half(q[i], j) * eg_h[i][j] - qo_h[:, :DN_DK])
            b_scr[u] = pb[m][:, DN_DK:]
            o_scr[u] = qo_h[:, DN_DK:]
            gt_scr[u] = jnp.broadcast_to(col(chunk[d, ci][3], lane_g(d, h)), (1, LANES))

    def scan(step):
        chains = [(0, step, h) for h in range(DN_HEADS)] + [(1, n_chunks - 1 - step, h) for h in range(DN_HEADS)]
        s = [s_ref[d * DN_HEADS + h] for d, ci, h in chains]
        ps = [mm(pq_scr[d * upd + ci * DN_HEADS + h], b16(s[i])) for i, (d, ci, h) in enumerate(chains)]
        for i, (d, ci, h) in enumerate(chains):
            u = d * upd + ci * DN_HEADS + h
            s_ref[d * DN_HEADS + h] = s[i] * gt_scr[u] - ps[i][0:DN_DK, :] + b_scr[u]
            dirs[d][4][ci * c:(ci + 1) * c, h * DN_DV:(h + 1) * DN_DV] = ps[i][DN_DK:DN_DK + c, :] + o_scr[u]

    prep()
    for step in range(n_chunks):
        scan(step)

    @pl.when((flags & DN_LAST_EMIT) != 0)
    def _():
        for d in range(2):
            for h in range(DN_HEADS):
                sfin_ref[0, d, h] = s_ref[d * DN_HEADS + h]


def _deltanet_schedule(batch, seq, dec_batch, dec_seq, tile):
    tf, tb, flags, s0i, sfi = [], [], [], [], []
    t0 = 0
    for n_seq, length, is_ctx in ((batch, seq, True), (dec_batch, dec_seq, False)):
        tps = length // tile
        for b in range(n_seq):
            for t in range(tps):
                tf.append(t0 + b * tps + t)
                tb.append(t0 + b * tps + tps - 1 - t)
                flags.append((DN_FIRST if t == 0 else 0)
                             | (DN_LAST_EMIT if (is_ctx and t == tps - 1) else 0)
                             | (DN_ZERO_INIT if is_ctx else 0))
                s0i.append(0 if is_ctx else b)
                sfi.append(b if is_ctx else batch - 1)
        t0 += n_seq * tps
    return [jnp.asarray(a, jnp.int32) for a in (tf, tb, flags, s0i, sfi)]


def _deltanet(q, k, v, gates, s0, layer, batch, seq, dec_batch, dec_seq):
    tile = MIX_TILE
    sched = _deltanet_schedule(batch, seq, dec_batch, dec_seq, tile)
    fwd = lambda w: pl.BlockSpec((tile, w), lambda s, tf, tb, fl, s0i, sfi: (tf[s], 0))
    bwd = lambda w: pl.BlockSpec((tile, w), lambda s, tf, tb, fl, s0i, sfi: (tb[s], 0))
    state_block = (2, DN_HEADS, DN_DK, DN_DV)
    units = 2 * (tile // DN_CHUNK) * DN_HEADS
    grid_spec = pltpu.PrefetchScalarGridSpec(
        num_scalar_prefetch=len(sched),
        grid=(sched[0].shape[0],),
        in_specs=[fwd(QK_W), fwd(QK_W), fwd(DN_W), fwd(LANES), bwd(QK_W), bwd(QK_W), bwd(DN_W), bwd(LANES),
                  pl.BlockSpec((1, 1) + state_block,
                               lambda s, tf, tb, fl, s0i, sfi: (s0i[s], layer, 0, 0, 0, 0))],
        out_specs=[fwd(DN_W), bwd(DN_W),
                   pl.BlockSpec((1,) + state_block, lambda s, tf, tb, fl, s0i, sfi: (sfi[s], 0, 0, 0, 0))],
        scratch_shapes=[pltpu.VMEM((2 * DN_HEADS, DN_DK, DN_DV), F32),
                        pltpu.VMEM((units, DN_DK + DN_CHUNK, DN_DV), BF16),
                        pltpu.VMEM((units, DN_DK, DN_DV), F32),
                        pltpu.VMEM((units, DN_CHUNK, DN_DV), F32),
                        pltpu.VMEM((units, 1, LANES), F32)],
    )
    return pl.pallas_call(
        functools.partial(_deltanet_kernel, tile=tile),
        grid_spec=grid_spec,
        out_shape=[jax.ShapeDtypeStruct((q.shape[0], DN_W), F32)] * 2
        + [jax.ShapeDtypeStruct((batch,) + state_block, F32)],
        compiler_params=pltpu.CompilerParams(dimension_semantics=("arbitrary",), vmem_limit_bytes=VMEM_LIMIT),
        name="deltanet_scan",
    )(*sched, q, k, v, gates, q, k, v, gates, s0)


def _outproj_kernel(*refs, tile, n_ctx, dec_seq):
    (*x_refs, ya_ref, of_ref, ob_ref, z_ref, yc_ref, mod_ref, wout_ref, vec_ref, dng_ref,
     wr_hi_ref, wr_lo_ref, br_ref, x1_ref, h_ref, route_ref, cnt_ref, run_ref) = refs
    start = pl.program_id(0) * tile
    m = _mod_row(mod_ref, start, n_ctx, dec_seq)
    gt1 = m[:, 2 * D_MODEL:3 * D_MODEL]
    sh2 = m[:, 3 * D_MODEL:4 * D_MODEL]
    sc2 = m[:, 4 * D_MODEL:5 * D_MODEL]

    @pl.when(pl.program_id(0) == 0)
    def _():
        run_ref[...] = jnp.zeros_like(run_ref)

    rows_per_part = tile // OUTPROJ_PARTS
    lane = lax.broadcasted_iota(jnp.int32, (rows_per_part, LANES), 1).astype(F32)
    ri = lax.broadcasted_iota(jnp.int32, (rows_per_part, rows_per_part), 0)
    rj = lax.broadcasted_iota(jnp.int32, (rows_per_part, rows_per_part), 1)
    earlier = (ri > rj).astype(BF16)
    parts = []
    for part in range(OUTPROJ_PARTS):
        rows = slice(part * rows_per_part, (part + 1) * rows_per_part)
        o = of_ref[rows, :] + ob_ref[rows, :]
        z = z_ref[rows, :]
        zz = z * _sigmoid(z)
        hs = []
        for h in range(DN_HEADS):
            oh = o[:, h * DN_DV:(h + 1) * DN_DV]
            on = oh * lax.rsqrt(jnp.mean(oh * oh, axis=-1, keepdims=True) + EPS) * dng_ref[...]
            hs.append(on * zz[:, h * DN_DV:(h + 1) * DN_DV])
        ycat = jnp.concatenate([ya_ref[rows, :]] + hs + [yc_ref[rows, :]], axis=-1).astype(BF16)
        y = jnp.dot(ycat, wout_ref[...], preferred_element_type=F32)
        x = _load_tokens(x_refs, rows, start < n_ctx)
        x1 = _ln(DEEPNORM_ALPHA * x + gt1 * y) * vec_ref[0:1, :] + vec_ref[1:2, :]
        x1_ref[rows, :] = x1
        hf = _ln(x1) * (1.0 + sc2) + sh2
        h_hi = hf.astype(BF16)
        h_ref[rows, :] = h_hi
        logits = (jnp.dot(h_hi, wr_hi_ref[...], preferred_element_type=F32)
                  + jnp.dot(h_hi, wr_lo_ref[...], preferred_element_type=F32)
                  + br_ref[...])
        vals, idxs = [], []
        for _ in range(TOP_K):
            top = jnp.max(logits, axis=-1, keepdims=True)
            idx = jnp.min(jnp.where(logits == top, lane, float(LANES)), axis=-1, keepdims=True)
            vals.append(top)
            idxs.append(idx)
            logits = jnp.where(lane == idx, -jnp.inf, logits)
        es = [jnp.exp(v - vals[0]) for v in vals]
        den = (es[0] + es[1]) + (es[2] + es[3])
        onehot = jnp.zeros((rows_per_part, LANES), F32)
        for idx in idxs:
            onehot = onehot + (lane == idx).astype(F32)
        before = jnp.dot(earlier, onehot.astype(BF16), preferred_element_type=F32)
        parts.append((rows, idxs, [e / den for e in es], before, jnp.sum(onehot, axis=0, keepdims=True)))

    run = run_ref[...]
    for rows, idxs, gates, before, count in parts:
        before = before + run
        run = run + count
        route = jnp.zeros((rows_per_part, LANES), F32)
        for j in range(TOP_K):
            rank = jnp.sum(jnp.where(lane == idxs[j], before, 0.0), axis=-1, keepdims=True)
            route = jnp.where(lane == float(j), idxs[j], route)
            route = jnp.where(lane == float(TOP_K + j), gates[j], route)
            route = jnp.where(lane == float(2 * TOP_K + j), rank, route)
        route_ref[rows, :] = route
    run_ref[...] = run
    cnt_ref[...] = jnp.broadcast_to(run, cnt_ref.shape)


def _out_projection(x_src, ya, o_f, o_b, proj, yc, mod, lp, n_ctx, dec_seq):
    nt, d = ya.shape[0], x_src[0].shape[1]
    tile = TOK_TILE
    tok = lambda w: pl.BlockSpec((tile, w), lambda i: (i, 0))
    full = lambda a: pl.BlockSpec(a.shape, lambda i: (0,) * a.ndim)
    return pl.pallas_call(
        functools.partial(_outproj_kernel, tile=tile, n_ctx=n_ctx, dec_seq=dec_seq),
        grid=(nt // tile,),
        in_specs=_token_specs(x_src, tile, n_ctx) + [
            tok(CONV_W), tok(DN_W), tok(DN_W),
            pl.BlockSpec((tile, DN_W), lambda i: (i, COL_Z // DN_W)),
            tok(GMLP_W), full(mod), full(lp["w_out"]), full(lp["ln1"]), full(lp["dn_norm_g"]),
            full(lp["wr_hi"]), full(lp["wr_lo"]), full(lp["b_router"])],
        out_specs=[tok(d), tok(d), tok(LANES), pl.BlockSpec((8, LANES), lambda i: (0, 0))],
        out_shape=[jax.ShapeDtypeStruct((nt, d), F32), jax.ShapeDtypeStruct((nt, d), BF16),
                   jax.ShapeDtypeStruct((nt, LANES), F32), jax.ShapeDtypeStruct((8, LANES), F32)],
        scratch_shapes=[pltpu.VMEM((1, LANES), F32)],
        compiler_params=pltpu.CompilerParams(dimension_semantics=("arbitrary",), vmem_limit_bytes=VMEM_LIMIT),
        name="out_projection_router",
    )(*x_src, ya, o_f, o_b, proj, yc, mod, lp["w_out"], lp["ln1"], lp["dn_norm_g"],
      lp["wr_hi"], lp["wr_lo"], lp["b_router"])


def _expert_kernel(be_ref, nu_ref, x_ref, wgu_ref, bgu_ref, wd_ref, bd_ref, out_ref, wgu16, wd16):
    i = pl.program_id(0)

    @pl.when((i == 0) | (be_ref[i] != be_ref[jnp.maximum(i - 1, 0)]))
    def _():
        wgu16[...] = wgu_ref[0, 0].astype(BF16)
        wd16[...] = wd_ref[0, 0].astype(BF16)

    @pl.when(i < nu_ref[0])
    def _():
        gu = jnp.dot(x_ref[...], wgu16[...], preferred_element_type=F32) + bgu_ref[0, 0]
        gate = jnp.minimum(gu[:, :D_FF], SWIGLU_LIMIT)
        up = jnp.clip(gu[:, D_FF:], -SWIGLU_LIMIT, SWIGLU_LIMIT)
        act = gate * _sigmoid(SWIGLU_ALPHA * gate)
        hmid = ((up + 1.0) * act).astype(BF16)
        y = jnp.dot(hmid, wd16[...], preferred_element_type=F32) + bd_ref[0, 0]
        out_ref[...] = y.astype(out_ref.dtype)

    @pl.when(i >= nu_ref[0])
    def _():
        out_ref[...] = jnp.zeros_like(out_ref)


def _experts(xg, block_e, n_used, w_gu, b_gu, w_down, b_down, layer):
    m_pad, d = xg.shape
    bm = MOE_BM
    grid_spec = pltpu.PrefetchScalarGridSpec(
        num_scalar_prefetch=2,
        grid=(m_pad // bm,),
        in_specs=[
            pl.BlockSpec((bm, d), lambda i, be, nu: (i, 0)),
            pl.BlockSpec((1, 1, d, 2 * D_FF), lambda i, be, nu: (layer, be[i], 0, 0)),
            pl.BlockSpec((1, 1, 1, 2 * D_FF), lambda i, be, nu: (layer, be[i], 0, 0)),
            pl.BlockSpec((1, 1, D_FF, d), lambda i, be, nu: (layer, be[i], 0, 0)),
            pl.BlockSpec((1, 1, 1, d), lambda i, be, nu: (layer, be[i], 0, 0)),
        ],
        out_specs=pl.BlockSpec((bm, d), lambda i, be, nu: (i, 0)),
        scratch_shapes=[pltpu.VMEM((d, 2 * D_FF), BF16), pltpu.VMEM((D_FF, d), BF16)],
    )
    return pl.pallas_call(
        _expert_kernel,
        grid_spec=grid_spec,
        out_shape=jax.ShapeDtypeStruct((m_pad, d), BF16),
        compiler_params=pltpu.CompilerParams(dimension_semantics=("arbitrary",), vmem_limit_bytes=VMEM_LIMIT),
        name="moe_experts",
    )(block_e, n_used, xg, w_gu, b_gu, w_down, b_down)


def _combine_kernel(yg_ref, route_ref, x1_ref, mod_ref, vec_ref, *out_refs, tile, n_ctx, dec_seq):
    start = pl.program_id(0) * tile
    m = _mod_row(mod_ref, start, n_ctx, dec_seq)
    gt2 = m[:, 5 * D_MODEL:6 * D_MODEL]
    gate = lambda j: route_ref[:, TOP_K + j:TOP_K + j + 1]
    ye = lambda j: yg_ref[j].astype(F32) * gate(j)
    y = (ye(0) + ye(1)) + (ye(2) + ye(3))
    res = _ln(DEEPNORM_ALPHA * x1_ref[...] + gt2 * y) * vec_ref[0:1, :] + vec_ref[1:2, :]
    if len(out_refs) == 1:
        out_refs[0][...] = res
    else:
        @pl.when(start < n_ctx)
        def _():
            out_refs[0][...] = res

        @pl.when(start >= n_ctx)
        def _():
            out_refs[1][...] = res


def _combine(yg, route, x1, mod, ln2, n_ctx, dec_seq, split_streams):
    nt, d = x1.shape
    tile = TOK_TILE
    ct = n_ctx // tile
    if split_streams:
        out_specs = [pl.BlockSpec((tile, d), lambda i: (jnp.minimum(i, ct - 1), 0)),
                     pl.BlockSpec((tile, d), lambda i: (jnp.maximum(i - ct, 0), 0))]
        out_shape = [jax.ShapeDtypeStruct((n_ctx, d), F32), jax.ShapeDtypeStruct((nt - n_ctx, d), F32)]
    else:
        out_specs = pl.BlockSpec((tile, d), lambda i: (i, 0))
        out_shape = jax.ShapeDtypeStruct((nt, d), F32)
    return pl.pallas_call(
        functools.partial(_combine_kernel, tile=tile, n_ctx=n_ctx, dec_seq=dec_seq),
        grid=(nt // tile,),
        in_specs=[pl.BlockSpec((TOP_K, tile, d), lambda i: (0, i, 0)),
                  pl.BlockSpec((tile, LANES), lambda i: (i, 0)),
                  pl.BlockSpec((tile, d), lambda i: (i, 0)),
                  pl.BlockSpec(mod.shape, lambda i: (0, 0)),
                  pl.BlockSpec(ln2.shape, lambda i: (0, 0))],
        out_specs=out_specs,
        out_shape=out_shape,
        compiler_params=pltpu.CompilerParams(dimension_semantics=("arbitrary",), vmem_limit_bytes=VMEM_LIMIT),
        name="moe_combine_ln",
    )(yg, route, x1, mod, ln2)


def _route(route, counts):
    nt = route.shape[0]
    nk = nt * TOP_K
    bm = MOE_BM
    expert = route[:, 0:TOP_K].astype(jnp.int32)
    rank = route[:, 2 * TOP_K:3 * TOP_K].astype(jnp.int32)
    counts = counts[0, :N_EXPERTS].astype(jnp.int32)
    padded = (counts + bm - 1) // bm * bm
    pad_end = jnp.cumsum(padded)
    pad_start = pad_end - padded
    grp_start = jnp.cumsum(counts) - counts
    pair_slot = pad_start[expert] + rank
    assert nt <= 1 << TOKEN_BITS
    tok = jnp.arange(nt, dtype=jnp.int32)[:, None]
    sorted_tok = jnp.sort((expert * (1 << TOKEN_BITS) + tok).reshape(-1)) & ((1 << TOKEN_BITS) - 1)
    n_blocks = nk // bm + N_EXPERTS
    blk_start = jnp.arange(n_blocks, dtype=jnp.int32) * bm
    block_e = jnp.minimum(jnp.sum((pad_end[None, :] <= blk_start[:, None]).astype(jnp.int32), axis=1),
                          N_EXPERTS - 1)
    within = jnp.arange(bm, dtype=jnp.int32)[None, :]
    j = (blk_start - pad_start[block_e])[:, None] + within
    src = jnp.clip(grp_start[block_e][:, None] + j, 0, nk - 1)
    slot = blk_start[:, None] + within
    slot_tok = jnp.where(j < counts[block_e][:, None], sorted_tok[src], slot % nt).reshape(-1)
    n_used = (pad_end[-1] // bm).astype(jnp.int32).reshape(1)
    return slot_tok, pair_slot, block_e.astype(jnp.int32), n_used


def _grid_pos_embed(t, d):
    rows = t // GRID_W
    r, col = jnp.meshgrid(jnp.arange(rows), jnp.arange(GRID_W), indexing="ij")
    r = r.reshape(-1).astype(F32)[:, None]
    col = col.reshape(-1).astype(F32)[:, None]
    n_freq = d // 4
    omega = 1.0 / (POS_BASE ** (jnp.arange(n_freq, dtype=F32) / n_freq))
    return jnp.concatenate([jnp.sin(r * omega), jnp.cos(r * omega),
                            jnp.sin(col * omega), jnp.cos(col * omega)], axis=-1)


def _pad_lanes(a, offset):
    return jnp.zeros((1, LANES), F32).at[0, offset:offset + a.shape[0]].set(a.astype(F32))


def _layer_params(l, w_in, conv_dw, conv_b, conv_ln_g, conv_ln_b, dn_conv, dn_a_log, dn_dt_bias, dn_norm_g,
                  gm_ln_g, gm_ln_b, gm_ws, gm_bs, w_out, ln1_g, ln1_b, ln2_g, ln2_b, w_router, b_router):
    wi = w_in[l]
    c_conv, c_qkv, c_z, c_ba = 0, 2 * CONV_W, 2 * CONV_W + 2 * QK_W + DN_W, 2 * CONV_W + 2 * QK_W + 2 * DN_W
    c_gm = c_ba + 4 * DN_HEADS
    w_in_r = jnp.concatenate([
        wi[:, c_qkv:c_z], wi[:, c_conv:c_qkv], wi[:, c_z:c_ba], wi[:, c_gm:],
        wi[:, c_ba:c_gm], jnp.zeros((D_MODEL, LANES - 4 * DN_HEADS), F32)], axis=1).astype(BF16)
    grp = jnp.arange(CONV_W) // GROUP_W
    gavg = (grp[:, None] == grp[None, :]).astype(BF16) * (1.0 / GROUP_W)
    gvec = jnp.concatenate([_pad_lanes(dn_a_log[l].reshape(-1), 2 * DN_HEADS),
                            _pad_lanes(dn_dt_bias[l].reshape(-1), 2 * DN_HEADS)], axis=0)
    wr = jnp.zeros((D_MODEL, LANES), F32).at[:, :N_EXPERTS].set(w_router[l])
    wr_hi = wr.astype(BF16)
    wr_lo = (wr - wr_hi.astype(F32)).astype(BF16)
    return {
        "w_in_r": w_in_r,
        "conv_dw": conv_dw[l],
        "conv_p": jnp.stack([conv_b[l], conv_ln_g[l], conv_ln_b[l]]),
        "dn_conv": dn_conv[l],
        "gvec": gvec,
        "gm_p": jnp.stack([gm_ln_g[l], gm_ln_b[l]]),
        "gm_ws": gm_ws[l],
        "gm_bsf": jnp.repeat(jnp.transpose(gm_bs[l]), GROUP_W, axis=1),
        "gavg": gavg,
        "w_out": w_out[l].astype(BF16),
        "ln1": jnp.stack([ln1_g[l], ln1_b[l]]),
        "ln2": jnp.stack([ln2_g[l], ln2_b[l]]),
        "dn_norm_g": dn_norm_g[l].reshape(1, DN_DV),
        "wr_hi": wr_hi,
        "wr_lo": wr_lo,
        "b_router": jnp.full((1, LANES), -1e30, F32).at[0, :N_EXPERTS].set(b_router[l]),
    }


def kernel(x_prompt, x_sample, state_delta, c, c_ctx, w_ada, b_ada, w_in, conv_dw, conv_b, conv_ln_g, conv_ln_b, dn_conv, dn_a_log, dn_dt_bias, dn_norm_g, gm_ln_g, gm_ln_b, gm_ws, gm_bs, w_out, ln1_g, ln1_b, ln2_g, ln2_b, w_router, b_router, w_gu, b_gu, w_down, b_down):
    batch, seq, d = x_prompt.shape
    dec_batch, dec_seq, _ = x_sample.shape
    n_ctx = batch * seq
    n_den = dec_batch * dec_seq
    depth = w_in.shape[0]

    cond = jnp.zeros((16, d), F32).at[0].set(c_ctx).at[1:1 + dec_batch].set(c)
    mod_all = _modulation(cond, w_ada, b_ada)
    b_gu_r = b_gu.reshape(depth, N_EXPERTS, 1, 2 * D_FF)
    b_down_r = b_down.reshape(depth, N_EXPERTS, 1, d)

    nt = n_ctx + n_den
    x_src = (x_prompt.reshape(n_ctx, d), x_sample.reshape(n_den, d), _grid_pos_embed(dec_seq, d))
    ctx_states = []
    for l in range(depth):
        lp = _layer_params(l, w_in, conv_dw, conv_b, conv_ln_g, conv_ln_b, dn_conv, dn_a_log, dn_dt_bias,
                           dn_norm_g, gm_ln_g, gm_ln_b, gm_ws, gm_bs, w_out, ln1_g, ln1_b, ln2_g, ln2_b,
                           w_router, b_router)
        mod = mod_all[l]
        proj = _in_projection(x_src, nt, mod, lp["w_in_r"], n_ctx, dec_seq)
        ya, yc, q, k, v, gates = _local_mixers(proj, lp, n_ctx // MIX_TILE, seq // MIX_TILE, dec_seq // MIX_TILE)
        o_f, o_b, s_fin = _deltanet(q, k, v, gates, state_delta, l, batch, seq, dec_batch, dec_seq)
        ctx_states.append(s_fin)
        x1, hffn, route, counts = _out_projection(x_src, ya, o_f, o_b, proj, yc, mod, lp, n_ctx, dec_seq)
        slot_tok, pair_slot, block_e, n_used = _route(route, counts)
        xg = hffn.at[slot_tok].get(mode="promise_in_bounds")
        yb = _experts(xg, block_e, n_used, w_gu, b_gu_r, w_down, b_down_r, l)
        yg = yb.at[jnp.transpose(pair_slot)].get(mode="promise_in_bounds")
        out = _combine(yg, route, x1, mod, lp["ln2"], n_ctx, dec_seq, split_streams=(l == depth - 1))
        x_src = (out,)

    new_state = jnp.stack(ctx_states, axis=1).astype(x_prompt.dtype)
    return (out[0].reshape(batch, seq, d), out[1].reshape(dec_batch, dec_seq, d), new_state)
```

```python
import functools

import jax
import jax.numpy as jnp
from jax import lax
from jax.experimental import pallas as pl
from jax.experimental.pallas import tpu as pltpu

F32 = jnp.float32
BF16 = jnp.bfloat16

D_MODEL = 1024
DEPTH = 2
GRID_W = 64
POS_BASE = 10000.0
CONV_W = 256
CONV_K = 31
CONV_HALO = 16
DN_HEADS = 4
DN_DK = 128
DN_DV = 128
QK_W = DN_HEADS * DN_DK
DN_W = DN_HEADS * DN_DV
DN_SHORT_K = 5
DN_HALO = 8
DN_CHUNK = 64
GMLP_W = 256
GMLP_GROUPS = 4
GMLP_CHUNK = 128
GROUP_W = 64
N_EXPERTS = 32
TOP_K = 4
D_FF = D_MODEL
SWIGLU_ALPHA = 1.702
SWIGLU_LIMIT = 7.0
DEEPNORM_ALPHA = (2 * DEPTH) ** 0.25
EPS = 1e-6

LANES = 128
SUBLANES = 8
COL_QKV = 0
COL_CONV = 2 * QK_W + DN_W
COL_Z = COL_CONV + 2 * CONV_W
COL_GM = COL_Z + DN_W
COL_BA = COL_GM + 2 * GMLP_W
PROJ_W = COL_BA + LANES

MIX_TILE = 256
TOK_TILE = 512
OUTPROJ_PARTS = 1
MOE_BM = 512
TOKEN_BITS = 16
MOD_ROWS = 16
MOD_TILE_N = 1024
CONV_ROWS = 64
QKV_ROWS = 32
V7X_VMEM_BYTES = 64 * 1024 * 1024
VMEM_LIMIT = V7X_VMEM_BYTES * 7 // 8


def _ln(x):
    mu = jnp.mean(x, axis=-1, keepdims=True)
    xc = x - mu
    return xc * lax.rsqrt(jnp.mean(xc * xc, axis=-1, keepdims=True) + EPS)


def _sigmoid(x):
    return jax.nn.sigmoid(x)


def _split_bf16(x, parts):
    out = []
    r = x
    for _ in range(parts):
        p = r.astype(BF16)
        out.append(p)
        r = r - p.astype(F32)
    return out


def _dot_exact_rhs(x, m_bf16, parts=3):
    acc = None
    for p in _split_bf16(x, parts):
        t = jnp.dot(p, m_bf16, preferred_element_type=F32)
        acc = t if acc is None else acc + t
    return acc


def _dot_exact_lhs(m_bf16, x, parts=3):
    acc = None
    for p in _split_bf16(x, parts):
        t = jnp.dot(m_bf16, p, preferred_element_type=F32)
        acc = t if acc is None else acc + t
    return acc


def _mod_row(mod_ref, start, n_ctx, dec_seq):
    row = jnp.where(start < n_ctx, 0, 1 + (start - n_ctx) // dec_seq)
    return mod_ref[pl.ds(row, 1), :]


def _mod_kernel(cond_ref, w_ref, b_ref, out_ref):
    c = cond_ref[...]
    s = c * _sigmoid(c)
    out_ref[0] = jnp.dot(s, w_ref[0], preferred_element_type=F32) + b_ref[0]


def _modulation(cond, w_ada, b_ada):
    nl, d, n = w_ada.shape
    r = cond.shape[0]
    tn = MOD_TILE_N
    return pl.pallas_call(
        _mod_kernel,
        grid=(nl, n // tn),
        in_specs=[
            pl.BlockSpec((r, d), lambda l, j: (0, 0)),
            pl.BlockSpec((1, d, tn), lambda l, j: (l, 0, j)),
            pl.BlockSpec((1, 1, tn), lambda l, j: (l, 0, j)),
        ],
        out_specs=pl.BlockSpec((1, r, tn), lambda l, j: (l, 0, j)),
        out_shape=jax.ShapeDtypeStruct((nl, r, n), F32),
        name="adaln_modulation",
    )(cond, w_ada, b_ada.reshape(nl, 1, n))


def _token_specs(x_src, tile, n_ctx):
    d = x_src[0].shape[1]
    if len(x_src) == 1:
        return [pl.BlockSpec((tile, d), lambda i: (i, 0))]
    ct = n_ctx // tile
    pt = x_src[2].shape[0] // tile
    return [pl.BlockSpec((tile, d), lambda i: (jnp.minimum(i, max(ct - 1, 0)), 0)),
            pl.BlockSpec((tile, d), lambda i: (jnp.maximum(i - ct, 0), 0)),
            pl.BlockSpec((tile, d), lambda i: (jnp.maximum(i - ct, 0) % pt, 0))]


def _load_tokens(x_refs, rows, is_ctx_tile):
    if len(x_refs) == 1:
        return x_refs[0][rows, :]
    xp_ref, xs_ref, pos_ref = x_refs
    return jnp.where(is_ctx_tile, xp_ref[rows, :], xs_ref[rows, :] + pos_ref[rows, :])


def _inproj_kernel(*refs, tile, n_ctx, dec_seq):
    *x_refs, mod_ref, w_ref, out_ref = refs
    start = pl.program_id(0) * tile
    m = _mod_row(mod_ref, start, n_ctx, dec_seq)
    sh1 = m[:, 0:D_MODEL]
    sc1 = m[:, D_MODEL:2 * D_MODEL]
    h = _ln(_load_tokens(x_refs, slice(None), start < n_ctx)) * (1.0 + sc1) + sh1
    out_ref[...] = jnp.dot(h.astype(BF16), w_ref[...], preferred_element_type=F32)


def _in_projection(x_src, nt, mod, w_in_r, n_ctx, dec_seq):
    tile = TOK_TILE
    return pl.pallas_call(
        functools.partial(_inproj_kernel, tile=tile, n_ctx=n_ctx, dec_seq=dec_seq),
        grid=(nt // tile,),
        in_specs=_token_specs(x_src, tile, n_ctx) + [
            pl.BlockSpec(mod.shape, lambda i: (0, 0)),
            pl.BlockSpec(w_in_r.shape, lambda i: (0, 0)),
        ],
        out_specs=pl.BlockSpec((tile, PROJ_W), lambda i: (i, 0)),
        out_shape=jax.ShapeDtypeStruct((nt, PROJ_W), F32),
        compiler_params=pltpu.CompilerParams(vmem_limit_bytes=VMEM_LIMIT),
        name="in_projection",
    )(*x_src, mod, w_in_r)


def _group_norm(x, gavg):
    mean = _dot_exact_rhs(x, gavg, parts=2)
    xc = x - mean
    var = _dot_exact_rhs(xc * xc, gavg, parts=2)
    return xc * lax.rsqrt(var + EPS)


def _mixpre_kernel(qkv_ref, qkv_p_ref, qkv_n_ref, cv_ref, cv_p_ref, cv_n_ref, gm_ref, ba_ref,
                   convw_ref, convp_ref, dnw_ref, gvec_ref, gmp_ref, ws_ref, bsf_ref, gavg_ref,
                   ya_ref, yc_ref, q_ref, k_ref, v_ref, gates_ref, cbuf, cshift, qbuf,
                   *, tile, n_ctx_tiles, ctx_tps, den_tps):
    i = pl.program_id(0)
    pos = jnp.where(i < n_ctx_tiles, i % ctx_tps, (i - n_ctx_tiles) % den_tps)
    tps = jnp.where(i < n_ctx_tiles, ctx_tps, den_tps)
    first = pos == 0
    last = pos == tps - 1
    gavg = gavg_ref[...]

    def glu(p):
        return p[:, :CONV_W] * _sigmoid(p[:, CONV_W:])

    cbuf[0:CONV_HALO, :] = jnp.where(first, 0.0, glu(cv_p_ref[...]))
    cbuf[CONV_HALO:CONV_HALO + tile, :] = glu(cv_ref[...])
    cbuf[CONV_HALO + tile:2 * CONV_HALO + tile, :] = jnp.where(last, 0.0, glu(cv_n_ref[...]))
    conv_b = convp_ref[0:1, :]
    conv_g = convp_ref[1:2, :]
    conv_beta = convp_ref[2:3, :]
    rc = CONV_ROWS
    off = CONV_HALO - CONV_K // 2
    span = tile + 2 * CONV_HALO - SUBLANES
    for b in range(1, SUBLANES):
        cshift[b - 1] = cbuf[b:b + span, :]
    for c in range(tile // rc):
        acc = jnp.zeros((rc, CONV_W), F32)
        for k in range(CONV_K):
            a, b = divmod(off + k, SUBLANES)
            r0 = c * rc + a * SUBLANES
            win = cbuf[r0:r0 + rc, :] if b == 0 else cshift[b - 1, r0:r0 + rc, :]
            acc = acc + win * convw_ref[k:k + 1, :]
        y = _group_norm(acc + conv_b, gavg) * conv_g + conv_beta
        ya_ref[c * rc:(c + 1) * rc, :] = y * _sigmoid(y)

    qbuf[0:DN_HALO, :] = jnp.where(first, 0.0, qkv_p_ref[...])
    qbuf[DN_HALO:DN_HALO + tile, :] = qkv_ref[...]
    qbuf[DN_HALO + tile:2 * DN_HALO + tile, :] = jnp.where(last, 0.0, qkv_n_ref[...])
    rq = QKV_ROWS
    offq = DN_HALO - DN_SHORT_K // 2
    outs = (q_ref, k_ref, v_ref)
    for part in range(3):
        c0 = part * QK_W
        for c in range(tile // rq):
            acc = jnp.zeros((rq, QK_W), F32)
            for k in range(DN_SHORT_K):
                acc = acc + (qbuf[c * rq + offq + k:c * rq + offq + k + rq, c0:c0 + QK_W]
                             * dnw_ref[k:k + 1, c0:c0 + QK_W])
            a = acc * _sigmoid(acc)
            if part < 2:
                scale = DN_DK ** -0.5 if part == 0 else 1.0
                hs = []
                for h in range(DN_HEADS):
                    ah = a[:, h * DN_DK:(h + 1) * DN_DK]
                    nrm = lax.rsqrt(jnp.sum(ah * ah, axis=-1, keepdims=True) + EPS)
                    hs.append(ah * (nrm * scale))
                a = jnp.concatenate(hs, axis=-1)
            outs[part][c * rq:(c + 1) * rq, :] = a

    p = ba_ref[...]
    beta = _sigmoid(p)
    xg = p + gvec_ref[1:2, :]
    softplus = jnp.maximum(xg, 0.0) + jnp.log1p(jnp.exp(-jnp.abs(xg)))
    g = -jnp.exp(gvec_ref[0:1, :]) * softplus
    lane = lax.broadcasted_iota(jnp.int32, p.shape, 1)
    gates_ref[...] = jnp.where(lane < 2 * DN_HEADS, beta, g)

    pg = gm_ref[...]
    ge = pg * (0.5 * (1.0 + jnp.tanh(0.7978845608028654 * (pg + 0.044715 * (pg * pg * pg)))))
    u = ge[:, :GMLP_W]
    vn = _group_norm(ge[:, GMLP_W:], gavg) * gmp_ref[0:1, :] + gmp_ref[1:2, :]
    grp = lax.broadcasted_iota(jnp.int32, (GMLP_CHUNK, GMLP_W), 1) // GROUP_W
    for n in range(tile // GMLP_CHUNK):
        vchunk = vn[n * GMLP_CHUNK:(n + 1) * GMLP_CHUNK, :]
        sg = bsf_ref[...]
        for gi in range(GMLP_GROUPS):
            r = jnp.dot(ws_ref[gi], vchunk, preferred_element_type=F32)
            sg = sg + jnp.where(grp == gi, r, 0.0)
        yc_ref[n * GMLP_CHUNK:(n + 1) * GMLP_CHUNK, :] = u[n * GMLP_CHUNK:(n + 1) * GMLP_CHUNK, :] * sg


def _local_mixers(proj, lp, n_ctx_tiles, ctx_tps, den_tps):
    nt = proj.shape[0]
    tile = MIX_TILE
    n_tiles = nt // tile
    cpb = tile // CONV_HALO
    qpb = tile // DN_HALO
    n_cblk = nt // CONV_HALO
    n_qblk = nt // DN_HALO
    col = lambda c, w: c // w
    full = lambda a: pl.BlockSpec(a.shape, lambda i: (0,) * a.ndim)
    in_specs = [
        pl.BlockSpec((tile, 3 * QK_W), lambda i: (i, col(COL_QKV, 3 * QK_W))),
        pl.BlockSpec((DN_HALO, 3 * QK_W), lambda i: (jnp.maximum(i * qpb - 1, 0), 0)),
        pl.BlockSpec((DN_HALO, 3 * QK_W), lambda i: (jnp.minimum((i + 1) * qpb, n_qblk - 1), 0)),
        pl.BlockSpec((tile, 2 * CONV_W), lambda i: (i, col(COL_CONV, 2 * CONV_W))),
        pl.BlockSpec((CONV_HALO, 2 * CONV_W), lambda i: (jnp.maximum(i * cpb - 1, 0), col(COL_CONV, 2 * CONV_W))),
        pl.BlockSpec((CONV_HALO, 2 * CONV_W),
                     lambda i: (jnp.minimum((i + 1) * cpb, n_cblk - 1), col(COL_CONV, 2 * CONV_W))),
        pl.BlockSpec((tile, 2 * GMLP_W), lambda i: (i, col(COL_GM, 2 * GMLP_W))),
        pl.BlockSpec((tile, LANES), lambda i: (i, col(COL_BA, LANES))),
        full(lp["conv_dw"]), full(lp["conv_p"]), full(lp["dn_conv"]), full(lp["gvec"]),
        full(lp["gm_p"]), full(lp["gm_ws"]), full(lp["gm_bsf"]), full(lp["gavg"]),
    ]
    tok = lambda w: pl.BlockSpec((tile, w), lambda i: (i, 0))
    shp = lambda w: jax.ShapeDtypeStruct((nt, w), F32)
    return pl.pallas_call(
        functools.partial(_mixpre_kernel, tile=tile, n_ctx_tiles=n_ctx_tiles, ctx_tps=ctx_tps, den_tps=den_tps),
        grid=(n_tiles,),
        in_specs=in_specs,
        out_specs=[tok(CONV_W), tok(GMLP_W), tok(QK_W), tok(QK_W), tok(DN_W), tok(LANES)],
        out_shape=[shp(CONV_W), shp(GMLP_W), shp(QK_W), shp(QK_W), shp(DN_W), shp(LANES)],
        scratch_shapes=[pltpu.VMEM((tile + 2 * CONV_HALO, CONV_W), F32),
                        pltpu.VMEM((SUBLANES - 1, tile + 2 * CONV_HALO - SUBLANES, CONV_W), F32),
                        pltpu.VMEM((tile + 2 * DN_HALO, 3 * QK_W), F32)],
        compiler_params=pltpu.CompilerParams(vmem_limit_bytes=VMEM_LIMIT),
        name="local_mixers",
    )(proj, proj, proj, proj, proj, proj, proj, proj,
      lp["conv_dw"], lp["conv_p"], lp["dn_conv"], lp["gvec"], lp["gm_p"], lp["gm_ws"], lp["gm_bsf"], lp["gavg"])


DN_FIRST, DN_LAST_EMIT, DN_ZERO_INIT = 1, 2, 4


def _deltanet_kernel(tf_ref, tb_ref, flag_ref, s0i_ref, sfi_ref,
                     qf_ref, kf_ref, vf_ref, gf_ref, qb_ref, kb_ref, vb_ref, gb_ref, s0_ref,
                     of_ref, ob_ref, sfin_ref, s_ref, pq_scr, b_scr, o_scr, gt_scr, *, tile):
    flags = flag_ref[pl.program_id(0)]
    first = (flags & DN_FIRST) != 0
    zero_init = (flags & DN_ZERO_INIT) != 0
    c = DN_CHUNK
    n_chunks = tile // c
    upd = n_chunks * DN_HEADS

    @pl.when(first & zero_init)
    def _():
        s_ref[...] = jnp.zeros_like(s_ref)

    @pl.when(first & jnp.logical_not(zero_init))
    def _():
        for d in range(2):
            for h in range(DN_HEADS):
                s_ref[d * DN_HEADS + h] = s0_ref[0, 0, d, h]

    ti = lax.broadcasted_iota(jnp.int32, (tile, tile), 0)
    tj = lax.broadcasted_iota(jnp.int32, (tile, tile), 1)
    same_chunk = (ti // c) == (tj // c)
    nt_dims = (((1,), (1,)), ((), ()))
    tn_dims = (((0,), (0,)), ((), ()))
    dirs = (
        (qf_ref, kf_ref, vf_ref, gf_ref, of_ref, ti >= tj, c - 1),
        (qb_ref, kb_ref, vb_ref, gb_ref, ob_ref, ti <= tj, 0),
    )

    def gate_context(d):
        g_ref, tile_tri = dirs[d][3], dirs[d][5]
        gates = g_ref[...]
        blockcum = (same_chunk & tile_tri).astype(BF16)
        gc_t = _dot_exact_lhs(blockcum, gates)
        return gates, gc_t, gc_t.T

    gate_ctx = [gate_context(0), gate_context(1)]

    b16 = lambda x: x.astype(BF16)
    mm = lambda x, y: jnp.dot(x, y, preferred_element_type=F32)

    pi = lax.broadcasted_iota(jnp.int32, (c, 2 * c), 0)
    plane = lax.broadcasted_iota(jnp.int32, (c, 2 * c), 1)
    pj = plane & (c - 1)
    left = plane < c
    psame = lambda s: (pi >> s) == (pj >> s)
    pm8 = psame(3)
    pl16 = psame(4) & ~pm8
    pl32 = psame(5) & ~psame(4)
    pl64 = ~psame(5)
    ptri = ((pi >= pj, pi > pj), (pi <= pj, pi < pj))

    def blockdiag(y):
        return b16(jnp.concatenate([jnp.where(left, y, 0.0), jnp.where(left, 0.0, y)], axis=0))

    def blockdiag_wide(ya, yb):
        z = jnp.zeros_like(ya)
        return b16(jnp.concatenate([jnp.concatenate([ya, z], axis=1), jnp.concatenate([z, yb], axis=1)], axis=0))

    def prep():
        pairs = [(d, ci, p) for d in range(2) for ci in range(n_chunks) for p in range(DN_HEADS // 2)]
        chunk = {}
        for d in range(2):
            gc_t, last_row = gate_ctx[d][1], dirs[d][6]
            for ci in range(n_chunks):
                r0 = ci * c
                gc_c = gc_t[r0:r0 + c, :]
                glast = gc_t[r0 + last_row:r0 + last_row + 1, :]
                chunk[d, ci] = (gc_c, jnp.exp(gc_c), jnp.exp(glast - gc_c), jnp.exp(glast))
        lane_b = lambda d, h: d * DN_HEADS + h
        lane_g = lambda d, h: 2 * DN_HEADS + d * DN_HEADS + h
        col = lambda x, l: x[:, l:l + 1]
        n = range(len(pairs))
        heads = [(2 * p, 2 * p + 1) for _, _, p in pairs]

        def gj_pair(d, ci, ha, hb):
            gc_tt = gate_ctx[d][2]
            grow = lambda h: gc_tt[lane_g(d, h):lane_g(d, h) + 1, (ci // 2) * 2 * c:(ci // 2 + 1) * 2 * c]
            ra, rb = grow(ha), grow(hb)
            if ci % 2 == 0:
                rb = pltpu.roll(rb, c, axis=1)
            else:
                ra = pltpu.roll(ra, c, axis=1)
            return jnp.where(left[0:1, :], ra, rb)

        beta_h = [[col(gate_ctx[d][0][ci * c:(ci + 1) * c, :], lane_b(d, h)) for h in heads[i]]
                  for i, (d, ci, _) in enumerate(pairs)]
        eg_h = [[col(chunk[d, ci][1], lane_g(d, h)) for h in heads[i]] for i, (d, ci, _) in enumerate(pairs)]
        beta = [jnp.where(left, beta_h[i][0], beta_h[i][1]) for i in n]
        dmat = [jnp.where(ptri[d][0], jnp.exp(jnp.minimum(
            jnp.where(left, col(chunk[d, ci][0], lane_g(d, heads[i][0])), col(chunk[d, ci][0], lane_g(d, heads[i][1])))
            - gj_pair(d, ci, *heads[i]), 0.0)), 0.0) for i, (d, ci, _) in enumerate(pairs)]
        w2 = 2 * DN_DK
        q = [dirs[d][0][ci * c:(ci + 1) * c, p * w2:(p + 1) * w2] for d, ci, p in pairs]
        k = [dirs[d][1][ci * c:(ci + 1) * c, p * w2:(p + 1) * w2] for d, ci, p in pairs]
        v = [dirs[d][2][ci * c:(ci + 1) * c, p * w2:(p + 1) * w2] for d, ci, p in pairs]
        k16 = [b16(x) for x in k]
        kbd = [blockdiag_wide(x[:, :DN_DK], x[:, DN_DK:]) for x in k]
        kk = [lax.dot_general(k16[i], kbd[i], nt_dims, preferred_element_type=F32) for i in n]
        qk = [lax.dot_general(b16(q[i]), kbd[i], nt_dims, preferred_element_type=F32) for i in n]
        a = [jnp.where(ptri[pairs[i][0]][1], beta[i] * kk[i] * dmat[i], 0.0) for i in n]
        dd = [jnp.where(pm8, a[i], 0.0) for i in n]
        dd16 = [b16(x) for x in dd]
        d2 = [mm(dd16[i], blockdiag(dd[i])) for i in n]
        d2bd = [blockdiag(x) for x in d2]
        d3 = [mm(dd16[i], d2bd[i]) for i in n]
        d4 = [mm(b16(d2[i]), d2bd[i]) for i in n]
        e = [d2[i] - dd[i] - d3[i] for i in n]
        t = [mm(b16(e[i]), blockdiag(d4[i])) for i in n]
        e = [e[i] + d4[i] + t[i] for i in n]
        for sel in (pl16, pl32, pl64):
            l = [jnp.where(sel, a[i], 0.0) for i in n]
            ly = [l[i] + mm(b16(e[i]), blockdiag(l[i])) for i in n]
            z = [mm(b16(ly[i]), blockdiag(e[i])) for i in n]
            e = [e[i] - ly[i] - z[i] for i in n]
        half = lambda x, j: x[:, j * DN_DK:(j + 1) * DN_DK]
        r = [[jnp.concatenate([half(k[i], j) * (beta_h[i][j] * eg_h[i][j]), half(v[i], j) * beta_h[i][j]], axis=1)
              for j in range(2)] for i in n]
        wu = [jnp.concatenate(r[i], axis=1) + mm(b16(e[i]), blockdiag_wide(*r[i])) for i in n]
        wu_h = [[wu[i][:, j * w2:(j + 1) * w2] for j in range(2)] for i in n]
        qo = [mm(b16(qk[i] * dmat[i]), blockdiag_wide(*wu_h[i])) for i in n]
        units = [(i, j) for i in n for j in range(2)]
        kd = [b16(half(k[i], j) * col(chunk[pairs[i][0], pairs[i][1]][2], lane_g(pairs[i][0], heads[i][j])))
              for i, j in units]
        pb = [lax.dot_general(kd[m], b16(wu_h[i][j]), tn_dims, preferred_element_type=F32)
              for m, (i, j) in enumerate(units)]
        for m, (i, j) in enumerate(units):
            d, ci, h = pairs[i][0], pairs[i][1], heads[i][j]
            u = d * upd + ci * DN_HEADS + h
            qo_h = qo[i][:, j * w2:(j + 1) * w2]
            pq_scr[u, 0:DN_DK, :] = b16(pb[m][:, :DN_DK])
            pq_scr[u, DN_DK:DN_DK + c, :] = b16(half(q[i], j) * eg_h[i][j] - qo_h[:, :DN_DK])
            b_scr[u] = pb[m][:, DN_DK:]
            o_scr[u] = qo_h[:, DN_DK:]
            gt_scr[u] = jnp.broadcast_to(col(chunk[d, ci][3], lane_g(d, h)), (1, LANES))

    def scan(step):
        chains = [(0, step, h) for h in range(DN_HEADS)] + [(1, n_chunks - 1 - step, h) for h in range(DN_HEADS)]
        s = [s_ref[d * DN_HEADS + h] for d, ci, h in chains]
        ps = [mm(pq_scr[d * upd + ci * DN_HEADS + h], b16(s[i])) for i, (d, ci, h) in enumerate(chains)]
        for i, (d, ci, h) in enumerate(chains):
            u = d * upd + ci * DN_HEADS + h
            s_ref[d * DN_HEADS + h] = s[i] * gt_scr[u] - ps[i][0:DN_DK, :] + b_scr[u]
            dirs[d][4][ci * c:(ci + 1) * c, h * DN_DV:(h + 1) * DN_DV] = ps[i][DN_DK:DN_DK + c, :] + o_scr[u]

    prep()
    for step in range(n_chunks):
        scan(step)

    @pl.when((flags & DN_LAST_EMIT) != 0)
    def _():
        for d in range(2):
            for h in range(DN_HEADS):
                sfin_ref[0, d, h] = s_ref[d * DN_HEADS + h]


def _deltanet_schedule(batch, seq, dec_batch, dec_seq, tile):
    tf, tb, flags, s0i, sfi = [], [], [], [], []
    t0 = 0
    for n_seq, length, is_ctx in ((batch, seq, True), (dec_batch, dec_seq, False)):
        tps = length // tile
        for b in range(n_seq):
            for t in range(tps):
                tf.append(t0 + b * tps + t)
                tb.append(t0 + b * tps + tps - 1 - t)
                flags.append((DN_FIRST if t == 0 else 0)
                             | (DN_LAST_EMIT if (is_ctx and t == tps - 1) else 0)
                             | (DN_ZERO_INIT if is_ctx else 0))
                s0i.append(0 if is_ctx else b)
                sfi.append(b if is_ctx else batch - 1)
        t0 += n_seq * tps
    return [jnp.asarray(a, jnp.int32) for a in (tf, tb, flags, s0i, sfi)]


def _deltanet(q, k, v, gates, s0, layer, batch, seq, dec_batch, dec_seq):
    tile = MIX_TILE
    sched = _deltanet_schedule(batch, seq, dec_batch, dec_seq, tile)
    fwd = lambda w: pl.BlockSpec((tile, w), lambda s, tf, tb, fl, s0i, sfi: (tf[s], 0))
    bwd = lambda w: pl.BlockSpec((tile, w), lambda s, tf, tb, fl, s0i, sfi: (tb[s], 0))
    state_block = (2, DN_HEADS, DN_DK, DN_DV)
    units = 2 * (tile // DN_CHUNK) * DN_HEADS
    grid_spec = pltpu.PrefetchScalarGridSpec(
        num_scalar_prefetch=len(sched),
        grid=(sched[0].shape[0],),
        in_specs=[fwd(QK_W), fwd(QK_W), fwd(DN_W), fwd(LANES), bwd(QK_W), bwd(QK_W), bwd(DN_W), bwd(LANES),
                  pl.BlockSpec((1, 1) + state_block,
                               lambda s, tf, tb, fl, s0i, sfi: (s0i[s], layer, 0, 0, 0, 0))],
        out_specs=[fwd(DN_W), bwd(DN_W),
                   pl.BlockSpec((1,) + state_block, lambda s, tf, tb, fl, s0i, sfi: (sfi[s], 0, 0, 0, 0))],
        scratch_shapes=[pltpu.VMEM((2 * DN_HEADS, DN_DK, DN_DV), F32),
                        pltpu.VMEM((units, DN_DK + DN_CHUNK, DN_DV), BF16),
                        pltpu.VMEM((units, DN_DK, DN_DV), F32),
                        pltpu.VMEM((units, DN_CHUNK, DN_DV), F32),
                        pltpu.VMEM((units, 1, LANES), F32)],
    )
    return pl.pallas_call(
        functools.partial(_deltanet_kernel, tile=tile),
        grid_spec=grid_spec,
        out_shape=[jax.ShapeDtypeStruct((q.shape[0], DN_W), F32)] * 2
        + [jax.ShapeDtypeStruct((batch,) + state_block, F32)],
        compiler_params=pltpu.CompilerParams(dimension_semantics=("arbitrary",), vmem_limit_bytes=VMEM_LIMIT),
        name="deltanet_scan",
    )(*sched, q, k, v, gates, q, k, v, gates, s0)


def _outproj_kernel(*refs, tile, n_ctx, dec_seq):
    (*x_refs, ya_ref, of_ref, ob_ref, z_ref, yc_ref, mod_ref, wout_ref, vec_ref, dng_ref,
     wr_hi_ref, wr_lo_ref, br_ref, x1_ref, h_ref, route_ref, cnt_ref, run_ref) = refs
    start = pl.program_id(0) * tile
    m = _mod_row(mod_ref, start, n_ctx, dec_seq)
    gt1 = m[:, 2 * D_MODEL:3 * D_MODEL]
    sh2 = m[:, 3 * D_MODEL:4 * D_MODEL]
    sc2 = m[:, 4 * D_MODEL:5 * D_MODEL]

    @pl.when(pl.program_id(0) == 0)
    def _():
        run_ref[...] = jnp.zeros_like(run_ref)

    rows_per_part = tile // OUTPROJ_PARTS
    lane = lax.broadcasted_iota(jnp.int32, (rows_per_part, LANES), 1).astype(F32)
    ri = lax.broadcasted_iota(jnp.int32, (rows_per_part, rows_per_part), 0)
    rj = lax.broadcasted_iota(jnp.int32, (rows_per_part, rows_per_part), 1)
    earlier = (ri > rj).astype(BF16)
    parts = []
    for part in range(OUTPROJ_PARTS):
        rows = slice(part * rows_per_part, (part + 1) * rows_per_part)
        o = of_ref[rows, :] + ob_ref[rows, :]
        z = z_ref[rows, :]
        zz = z * _sigmoid(z)
        hs = []
        for h in range(DN_HEADS):
            oh = o[:, h * DN_DV:(h + 1) * DN_DV]
            on = oh * lax.rsqrt(jnp.mean(oh * oh, axis=-1, keepdims=True) + EPS) * dng_ref[...]
            hs.append(on * zz[:, h * DN_DV:(h + 1) * DN_DV])
        ycat = jnp.concatenate([ya_ref[rows, :]] + hs + [yc_ref[rows, :]], axis=-1).astype(BF16)
        y = jnp.dot(ycat, wout_ref[...], preferred_element_type=F32)
        x = _load_tokens(x_refs, rows, start < n_ctx)
        x1 = _ln(DEEPNORM_ALPHA * x + gt1 * y) * vec_ref[0:1, :] + vec_ref[1:2, :]
        x1_ref[rows, :] = x1
        hf = _ln(x1) * (1.0 + sc2) + sh2
        h_hi = hf.astype(BF16)
        h_ref[rows, :] = h_hi
        logits = (jnp.dot(h_hi, wr_hi_ref[...], preferred_element_type=F32)
                  + jnp.dot(h_hi, wr_lo_ref[...], preferred_element_type=F32)
                  + br_ref[...])
        vals, idxs = [], []
        for _ in range(TOP_K):
            top = jnp.max(logits, axis=-1, keepdims=True)
            idx = jnp.min(jnp.where(logits == top, lane, float(LANES)), axis=-1, keepdims=True)
            vals.append(top)
            idxs.append(idx)
            logits = jnp.where(lane == idx, -jnp.inf, logits)
        es = [jnp.exp(v - vals[0]) for v in vals]
        den = (es[0] + es[1]) + (es[2] + es[3])
        onehot = jnp.zeros((rows_per_part, LANES), F32)
        for idx in idxs:
            onehot = onehot + (lane == idx).astype(F32)
        before = jnp.dot(earlier, onehot.astype(BF16), preferred_element_type=F32)
        parts.append((rows, idxs, [e / den for e in es], before, jnp.sum(onehot, axis=0, keepdims=True)))

    run = run_ref[...]
    for rows, idxs, gates, before, count in parts:
        before = before + run
        run = run + count
        route = jnp.zeros((rows_per_part, LANES), F32)
        for j in range(TOP_K):
            rank = jnp.sum(jnp.where(lane == idxs[j], before, 0.0), axis=-1, keepdims=True)
            route = jnp.where(lane == float(j), idxs[j], route)
            route = jnp.where(lane == float(TOP_K + j), gates[j], route)
            route = jnp.where(lane == float(2 * TOP_K + j), rank, route)
        route_ref[rows, :] = route
    run_ref[...] = run
    cnt_ref[...] = jnp.broadcast_to(run, cnt_ref.shape)


def _out_projection(x_src, ya, o_f, o_b, proj, yc, mod, lp, n_ctx, dec_seq):
    nt, d = ya.shape[0], x_src[0].shape[1]
    tile = TOK_TILE
    tok = lambda w: pl.BlockSpec((tile, w), lambda i: (i, 0))
    full = lambda a: pl.BlockSpec(a.shape, lambda i: (0,) * a.ndim)
    return pl.pallas_call(
        functools.partial(_outproj_kernel, tile=tile, n_ctx=n_ctx, dec_seq=dec_seq),
        grid=(nt // tile,),
        in_specs=_token_specs(x_src, tile, n_ctx) + [
            tok(CONV_W), tok(DN_W), tok(DN_W),
            pl.BlockSpec((tile, DN_W), lambda i: (i, COL_Z // DN_W)),
            tok(GMLP_W), full(mod), full(lp["w_out"]), full(lp["ln1"]), full(lp["dn_norm_g"]),
            full(lp["wr_hi"]), full(lp["wr_lo"]), full(lp["b_router"])],
        out_specs=[tok(d), tok(d), tok(LANES), pl.BlockSpec((SUBLANES, LANES), lambda i: (0, 0))],
        out_shape=[jax.ShapeDtypeStruct((nt, d), F32), jax.ShapeDtypeStruct((nt, d), BF16),
                   jax.ShapeDtypeStruct((nt, LANES), F32), jax.ShapeDtypeStruct((SUBLANES, LANES), F32)],
        scratch_shapes=[pltpu.VMEM((1, LANES), F32)],
        compiler_params=pltpu.CompilerParams(dimension_semantics=("arbitrary",), vmem_limit_bytes=VMEM_LIMIT),
        name="out_projection_router",
    )(*x_src, ya, o_f, o_b, proj, yc, mod, lp["w_out"], lp["ln1"], lp["dn_norm_g"],
      lp["wr_hi"], lp["wr_lo"], lp["b_router"])


def _expert_kernel(be_ref, nu_ref, x_ref, wgu_ref, bgu_ref, wd_ref, bd_ref, out_ref, wgu16, wd16):
    i = pl.program_id(0)

    @pl.when((i == 0) | (be_ref[i] != be_ref[jnp.maximum(i - 1, 0)]))
    def _():
        wgu16[...] = wgu_ref[0, 0].astype(BF16)
        wd16[...] = wd_ref[0, 0].astype(BF16)

    @pl.when(i < nu_ref[0])
    def _():
        gu = jnp.dot(x_ref[...], wgu16[...], preferred_element_type=F32) + bgu_ref[0, 0]
        gate = jnp.minimum(gu[:, :D_FF], SWIGLU_LIMIT)
        up = jnp.clip(gu[:, D_FF:], -SWIGLU_LIMIT, SWIGLU_LIMIT)
        act = gate * _sigmoid(SWIGLU_ALPHA * gate)
        hmid = ((up + 1.0) * act).astype(BF16)
        y = jnp.dot(hmid, wd16[...], preferred_element_type=F32) + bd_ref[0, 0]
        out_ref[...] = y.astype(out_ref.dtype)

    @pl.when(i >= nu_ref[0])
    def _():
        out_ref[...] = jnp.zeros_like(out_ref)


def _experts(xg, block_e, n_used, w_gu, b_gu, w_down, b_down, layer):
    m_pad, d = xg.shape
    bm = MOE_BM
    grid_spec = pltpu.PrefetchScalarGridSpec(
        num_scalar_prefetch=2,
        grid=(m_pad // bm,),
        in_specs=[
            pl.BlockSpec((bm, d), lambda i, be, nu: (i, 0)),
            pl.BlockSpec((1, 1, d, 2 * D_FF), lambda i, be, nu: (layer, be[i], 0, 0)),
            pl.BlockSpec((1, 1, 1, 2 * D_FF), lambda i, be, nu: (layer, be[i], 0, 0)),
            pl.BlockSpec((1, 1, D_FF, d), lambda i, be, nu: (layer, be[i], 0, 0)),
            pl.BlockSpec((1, 1, 1, d), lambda i, be, nu: (layer, be[i], 0, 0)),
        ],
        out_specs=pl.BlockSpec((bm, d), lambda i, be, nu: (i, 0)),
        scratch_shapes=[pltpu.VMEM((d, 2 * D_FF), BF16), pltpu.VMEM((D_FF, d), BF16)],
    )
    return pl.pallas_call(
        _expert_kernel,
        grid_spec=grid_spec,
        out_shape=jax.ShapeDtypeStruct((m_pad, d), BF16),
        compiler_params=pltpu.CompilerParams(dimension_semantics=("arbitrary",), vmem_limit_bytes=VMEM_LIMIT),
        name="moe_experts",
    )(block_e, n_used, xg, w_gu, b_gu, w_down, b_down)


def _combine_kernel(yg_ref, route_ref, x1_ref, mod_ref, vec_ref, *out_refs, tile, n_ctx, dec_seq):
    start = pl.program_id(0) * tile
    m = _mod_row(mod_ref, start, n_ctx, dec_seq)
    gt2 = m[:, 5 * D_MODEL:6 * D_MODEL]
    gate = lambda j: route_ref[:, TOP_K + j:TOP_K + j + 1]
    ye = lambda j: yg_ref[j].astype(F32) * gate(j)
    y = (ye(0) + ye(1)) + (ye(2) + ye(3))
    res = _ln(DEEPNORM_ALPHA * x1_ref[...] + gt2 * y) * vec_ref[0:1, :] + vec_ref[1:2, :]
    if len(out_refs) == 1:
        out_refs[0][...] = res
    else:
        @pl.when(start < n_ctx)
        def _():
            out_refs[0][...] = res

        @pl.when(start >= n_ctx)
        def _():
            out_refs[1][...] = res


def _combine(yg, route, x1, mod, ln2, n_ctx, dec_seq, split_streams):
    nt, d = x1.shape
    tile = TOK_TILE
    ct = n_ctx // tile
    if split_streams:
        out_specs = [pl.BlockSpec((tile, d), lambda i: (jnp.minimum(i, ct - 1), 0)),
                     pl.BlockSpec((tile, d), lambda i: (jnp.maximum(i - ct, 0), 0))]
        out_shape = [jax.ShapeDtypeStruct((n_ctx, d), F32), jax.ShapeDtypeStruct((nt - n_ctx, d), F32)]
    else:
        out_specs = pl.BlockSpec((tile, d), lambda i: (i, 0))
        out_shape = jax.ShapeDtypeStruct((nt, d), F32)
    return pl.pallas_call(
        functools.partial(_combine_kernel, tile=tile, n_ctx=n_ctx, dec_seq=dec_seq),
        grid=(nt // tile,),
        in_specs=[pl.BlockSpec((TOP_K, tile, d), lambda i: (0, i, 0)),
                  pl.BlockSpec((tile, LANES), lambda i: (i, 0)),
                  pl.BlockSpec((tile, d), lambda i: (i, 0)),
                  pl.BlockSpec(mod.shape, lambda i: (0, 0)),
                  pl.BlockSpec(ln2.shape, lambda i: (0, 0))],
        out_specs=out_specs,
        out_shape=out_shape,
        compiler_params=pltpu.CompilerParams(dimension_semantics=("arbitrary",), vmem_limit_bytes=VMEM_LIMIT),
        name="moe_combine_ln",
    )(yg, route, x1, mod, ln2)


def _route(route, counts):
    nt = route.shape[0]
    nk = nt * TOP_K
    bm = MOE_BM
    expert = route[:, 0:TOP_K].astype(jnp.int32)
    rank = route[:, 2 * TOP_K:3 * TOP_K].astype(jnp.int32)
    counts = counts[0, :N_EXPERTS].astype(jnp.int32)
    padded = (counts + bm - 1) // bm * bm
    pad_end = jnp.cumsum(padded)
    pad_start = pad_end - padded
    pair_slot = pad_start[expert] + rank
    n_blocks = nk // bm + N_EXPERTS
    m_pad = n_blocks * bm
    pad_tok = (1 << TOKEN_BITS) - 1
    assert nt <= pad_tok
    tok = jnp.arange(nt, dtype=jnp.int32)[:, None]
    pad_cum = jnp.cumsum(padded - counts)
    pad_id = jnp.arange(m_pad - nk, dtype=jnp.int32)
    pad_expert = jnp.sum((pad_cum[None, :] <= pad_id[:, None]).astype(jnp.int32), axis=1)
    keys = jnp.concatenate([(expert * (1 << TOKEN_BITS) + tok).reshape(-1), pad_expert * (1 << TOKEN_BITS) + pad_tok])
    slot_key = jnp.sort(keys) & pad_tok
    slot_tok = jnp.where(slot_key == pad_tok, jnp.arange(m_pad, dtype=jnp.int32) % nt, slot_key)
    blk_start = jnp.arange(n_blocks, dtype=jnp.int32) * bm
    block_e = jnp.minimum(jnp.sum((pad_end[None, :] <= blk_start[:, None]).astype(jnp.int32), axis=1),
                          N_EXPERTS - 1)
    n_used = (pad_end[-1] // bm).astype(jnp.int32).reshape(1)
    return slot_tok, pair_slot, block_e.astype(jnp.int32), n_used


def _grid_pos_embed(t, d):
    rows = t // GRID_W
    r, col = jnp.meshgrid(jnp.arange(rows), jnp.arange(GRID_W), indexing="ij")
    r = r.reshape(-1).astype(F32)[:, None]
    col = col.reshape(-1).astype(F32)[:, None]
    n_freq = d // 4
    omega = 1.0 / (POS_BASE ** (jnp.arange(n_freq, dtype=F32) / n_freq))
    return jnp.concatenate([jnp.sin(r * omega), jnp.cos(r * omega),
                            jnp.sin(col * omega), jnp.cos(col * omega)], axis=-1)


def _pad_lanes(a, offset):
    return jnp.zeros((1, LANES), F32).at[0, offset:offset + a.shape[0]].set(a.astype(F32))


def _layer_params(l, w_in, conv_dw, conv_b, conv_ln_g, conv_ln_b, dn_conv, dn_a_log, dn_dt_bias, dn_norm_g,
                  gm_ln_g, gm_ln_b, gm_ws, gm_bs, w_out, ln1_g, ln1_b, ln2_g, ln2_b, w_router, b_router):
    wi = w_in[l]
    c_conv, c_qkv, c_z, c_ba = 0, 2 * CONV_W, 2 * CONV_W + 2 * QK_W + DN_W, 2 * CONV_W + 2 * QK_W + 2 * DN_W
    c_gm = c_ba + 4 * DN_HEADS
    w_in_r = jnp.concatenate([
        wi[:, c_qkv:c_z], wi[:, c_conv:c_qkv], wi[:, c_z:c_ba], wi[:, c_gm:],
        wi[:, c_ba:c_gm], jnp.zeros((D_MODEL, LANES - 4 * DN_HEADS), F32)], axis=1).astype(BF16)
    grp = jnp.arange(CONV_W) // GROUP_W
    gavg = (grp[:, None] == grp[None, :]).astype(BF16) * (1.0 / GROUP_W)
    gvec = jnp.concatenate([_pad_lanes(dn_a_log[l].reshape(-1), 2 * DN_HEADS),
                            _pad_lanes(dn_dt_bias[l].reshape(-1), 2 * DN_HEADS)], axis=0)
    wr = jnp.zeros((D_MODEL, LANES), F32).at[:, :N_EXPERTS].set(w_router[l])
    wr_hi = wr.astype(BF16)
    wr_lo = (wr - wr_hi.astype(F32)).astype(BF16)
    return {
        "w_in_r": w_in_r,
        "conv_dw": conv_dw[l],
        "conv_p": jnp.stack([conv_b[l], conv_ln_g[l], conv_ln_b[l]]),
        "dn_conv": dn_conv[l],
        "gvec": gvec,
        "gm_p": jnp.stack([gm_ln_g[l], gm_ln_b[l]]),
        "gm_ws": gm_ws[l],
        "gm_bsf": jnp.repeat(jnp.transpose(gm_bs[l]), GROUP_W, axis=1),
        "gavg": gavg,
        "w_out": w_out[l].astype(BF16),
        "ln1": jnp.stack([ln1_g[l], ln1_b[l]]),
        "ln2": jnp.stack([ln2_g[l], ln2_b[l]]),
        "dn_norm_g": dn_norm_g[l].reshape(1, DN_DV),
        "wr_hi": wr_hi,
        "wr_lo": wr_lo,
        "b_router": jnp.full((1, LANES), -1e30, F32).at[0, :N_EXPERTS].set(b_router[l]),
    }


def kernel(x_prompt, x_sample, state_delta, c, c_ctx, w_ada, b_ada, w_in, conv_dw, conv_b, conv_ln_g, conv_ln_b, dn_conv, dn_a_log, dn_dt_bias, dn_norm_g, gm_ln_g, gm_ln_b, gm_ws, gm_bs, w_out, ln1_g, ln1_b, ln2_g, ln2_b, w_router, b_router, w_gu, b_gu, w_down, b_down):
    batch, seq, d = x_prompt.shape
    dec_batch, dec_seq, _ = x_sample.shape
    n_ctx = batch * seq
    n_den = dec_batch * dec_seq
    depth = w_in.shape[0]

    assert 1 + dec_batch <= MOD_ROWS and seq % MIX_TILE == 0 and dec_seq % TOK_TILE == 0 and n_ctx % TOK_TILE == 0
    cond = jnp.zeros((MOD_ROWS, d), F32).at[0].set(c_ctx).at[1:1 + dec_batch].set(c)
    mod_all = _modulation(cond, w_ada, b_ada)
    b_gu_r = b_gu.reshape(depth, N_EXPERTS, 1, 2 * D_FF)
    b_down_r = b_down.reshape(depth, N_EXPERTS, 1, d)

    nt = n_ctx + n_den
    x_src = (x_prompt.reshape(n_ctx, d), x_sample.reshape(n_den, d), _grid_pos_embed(dec_seq, d))
    ctx_states = []
    for l in range(depth):
        lp = _layer_params(l, w_in, conv_dw, conv_b, conv_ln_g, conv_ln_b, dn_conv, dn_a_log, dn_dt_bias,
                           dn_norm_g, gm_ln_g, gm_ln_b, gm_ws, gm_bs, w_out, ln1_g, ln1_b, ln2_g, ln2_b,
                           w_router, b_router)
        mod = mod_all[l]
        proj = _in_projection(x_src, nt, mod, lp["w_in_r"], n_ctx, dec_seq)
        ya, yc, q, k, v, gates = _local_mixers(proj, lp, n_ctx // MIX_TILE, seq // MIX_TILE, dec_seq // MIX_TILE)
        o_f, o_b, s_fin = _deltanet(q, k, v, gates, state_delta, l, batch, seq, dec_batch, dec_seq)
        ctx_states.append(s_fin)
        x1, hffn, route, counts = _out_projection(x_src, ya, o_f, o_b, proj, yc, mod, lp, n_ctx, dec_seq)
        slot_tok, pair_slot, block_e, n_used = _route(route, counts)
        xg = hffn.at[slot_tok].get(mode="promise_in_bounds")
        yb = _experts(xg, block_e, n_used, w_gu, b_gu_r, w_down, b_down_r, l)
        yg = yb.at[jnp.transpose(pair_slot)].get(mode="promise_in_bounds")
        out = _combine(yg, route, x1, mod, lp["ln2"], n_ctx, dec_seq, split_streams=(l == depth - 1))
        x_src = (out,)

    new_state = jnp.stack(ctx_states, axis=1).astype(x_prompt.dtype)
    return (out[0].reshape(batch, seq, d), out[1].reshape(dec_batch, dec_seq, d), new_state)
```

```python
import functools

import jax
import jax.numpy as jnp
from jax import lax
from jax.experimental import pallas as pl
from jax.experimental.pallas import tpu as pltpu

F32 = jnp.float32
BF16 = jnp.bfloat16

D_MODEL = 1024
DEPTH = 2
GRID_W = 64
POS_BASE = 10000.0
CONV_W = 256
CONV_K = 31
CONV_HALO = 16
DN_HEADS = 4
DN_DK = 128
DN_DV = 128
QK_W = DN_HEADS * DN_DK
DN_W = DN_HEADS * DN_DV
DN_SHORT_K = 5
DN_HALO = 8
DN_CHUNK = 64
GMLP_W = 256
GMLP_GROUPS = 4
GMLP_CHUNK = 128
GROUP_W = 64
N_EXPERTS = 32
TOP_K = 4
D_FF = D_MODEL
SWIGLU_ALPHA = 1.702
SWIGLU_LIMIT = 7.0
DEEPNORM_ALPHA = (2 * DEPTH) ** 0.25
EPS = 1e-6

LANES = 128
SUBLANES = 8
COL_QKV = 0
COL_CONV = 2 * QK_W + DN_W
COL_Z = COL_CONV + 2 * CONV_W
COL_GM = COL_Z + DN_W
COL_BA = COL_GM + 2 * GMLP_W
PROJ_W = COL_BA + LANES

MIX_TILE = 256
TOK_TILE = 512
OUTPROJ_PARTS = 1
MOE_BM = 512
TOKEN_BITS = 16
MOD_ROWS = 16
MOD_TILE_N = 1024
CONV_ROWS = 64
QKV_ROWS = 32
V7X_VMEM_BYTES = 64 * 1024 * 1024
VMEM_LIMIT = V7X_VMEM_BYTES * 7 // 8


def _ln(x):
    mu = jnp.mean(x, axis=-1, keepdims=True)
    xc = x - mu
    return xc * lax.rsqrt(jnp.mean(xc * xc, axis=-1, keepdims=True) + EPS)


def _sigmoid(x):
    return jax.nn.sigmoid(x)


def _split_bf16(x, parts):
    out = []
    r = x
    for _ in range(parts):
        p = r.astype(BF16)
        out.append(p)
        r = r - p.astype(F32)
    return out


def _dot_exact_rhs(x, m_bf16, parts=3):
    acc = None
    for p in _split_bf16(x, parts):
        t = jnp.dot(p, m_bf16, preferred_element_type=F32)
        acc = t if acc is None else acc + t
    return acc


def _dot_exact_lhs(m_bf16, x, parts=3):
    acc = None
    for p in _split_bf16(x, parts):
        t = jnp.dot(m_bf16, p, preferred_element_type=F32)
        acc = t if acc is None else acc + t
    return acc


def _mod_row(mod_ref, start, n_ctx, dec_seq):
    row = jnp.where(start < n_ctx, 0, 1 + (start - n_ctx) // dec_seq)
    return mod_ref[pl.ds(row, 1), :]


def _mod_kernel(cond_ref, w_ref, b_ref, out_ref):
    c = cond_ref[...]
    s = c * _sigmoid(c)
    out_ref[0] = jnp.dot(s, w_ref[0], preferred_element_type=F32) + b_ref[0]


def _modulation(cond, w_ada, b_ada):
    nl, d, n = w_ada.shape
    r = cond.shape[0]
    tn = MOD_TILE_N
    return pl.pallas_call(
        _mod_kernel,
        grid=(nl, n // tn),
        in_specs=[
            pl.BlockSpec((r, d), lambda l, j: (0, 0)),
            pl.BlockSpec((1, d, tn), lambda l, j: (l, 0, j)),
            pl.BlockSpec((1, 1, tn), lambda l, j: (l, 0, j)),
        ],
        out_specs=pl.BlockSpec((1, r, tn), lambda l, j: (l, 0, j)),
        out_shape=jax.ShapeDtypeStruct((nl, r, n), F32),
        name="adaln_modulation",
    )(cond, w_ada, b_ada.reshape(nl, 1, n))


def _token_specs(x_src, tile, n_ctx):
    d = x_src[0].shape[1]
    if len(x_src) == 1:
        return [pl.BlockSpec((tile, d), lambda i: (i, 0))]
    ct = n_ctx // tile
    pt = x_src[2].shape[0] // tile
    return [pl.BlockSpec((tile, d), lambda i: (jnp.minimum(i, max(ct - 1, 0)), 0)),
            pl.BlockSpec((tile, d), lambda i: (jnp.maximum(i - ct, 0), 0)),
            pl.BlockSpec((tile, d), lambda i: (jnp.maximum(i - ct, 0) % pt, 0))]


def _load_tokens(x_refs, rows, is_ctx_tile):
    if len(x_refs) == 1:
        return x_refs[0][rows, :]
    xp_ref, xs_ref, pos_ref = x_refs
    return jnp.where(is_ctx_tile, xp_ref[rows, :], xs_ref[rows, :] + pos_ref[rows, :])


def _inproj_kernel(*refs, tile, n_ctx, dec_seq):
    *x_refs, mod_ref, w_ref, out_ref = refs
    start = pl.program_id(0) * tile
    m = _mod_row(mod_ref, start, n_ctx, dec_seq)
    sh1 = m[:, 0:D_MODEL]
    sc1 = m[:, D_MODEL:2 * D_MODEL]
    h = _ln(_load_tokens(x_refs, slice(None), start < n_ctx)) * (1.0 + sc1) + sh1
    out_ref[...] = jnp.dot(h.astype(BF16), w_ref[...], preferred_element_type=F32)


def _in_projection(x_src, nt, mod, w_in_r, n_ctx, dec_seq):
    tile = TOK_TILE
    return pl.pallas_call(
        functools.partial(_inproj_kernel, tile=tile, n_ctx=n_ctx, dec_seq=dec_seq),
        grid=(nt // tile,),
        in_specs=_token_specs(x_src, tile, n_ctx) + [
            pl.BlockSpec(mod.shape, lambda i: (0, 0)),
            pl.BlockSpec(w_in_r.shape, lambda i: (0, 0)),
        ],
        out_specs=pl.BlockSpec((tile, PROJ_W), lambda i: (i, 0)),
        out_shape=jax.ShapeDtypeStruct((nt, PROJ_W), F32),
        compiler_params=pltpu.CompilerParams(vmem_limit_bytes=VMEM_LIMIT),
        name="in_projection",
    )(*x_src, mod, w_in_r)


def _group_norm(x, gavg):
    mean = _dot_exact_rhs(x, gavg, parts=2)
    xc = x - mean
    var = _dot_exact_rhs(xc * xc, gavg, parts=2)
    return xc * lax.rsqrt(var + EPS)


def _mixpre_kernel(qkv_ref, qkv_p_ref, qkv_n_ref, cv_ref, cv_p_ref, cv_n_ref, gm_ref, ba_ref,
                   convw_ref, convp_ref, dnw_ref, gvec_ref, gmp_ref, ws_ref, bsf_ref, gavg_ref,
                   ya_ref, yc_ref, q_ref, k_ref, v_ref, gates_ref, cbuf, cshift, qbuf,
                   *, tile, n_ctx_tiles, ctx_tps, den_tps):
    i = pl.program_id(0)
    pos = jnp.where(i < n_ctx_tiles, i % ctx_tps, (i - n_ctx_tiles) % den_tps)
    tps = jnp.where(i < n_ctx_tiles, ctx_tps, den_tps)
    first = pos == 0
    last = pos == tps - 1
    gavg = gavg_ref[...]

    def glu(p):
        return p[:, :CONV_W] * _sigmoid(p[:, CONV_W:])

    cbuf[0:CONV_HALO, :] = jnp.where(first, 0.0, glu(cv_p_ref[...]))
    cbuf[CONV_HALO:CONV_HALO + tile, :] = glu(cv_ref[...])
    cbuf[CONV_HALO + tile:2 * CONV_HALO + tile, :] = jnp.where(last, 0.0, glu(cv_n_ref[...]))
    conv_b = convp_ref[0:1, :]
    conv_g = convp_ref[1:2, :]
    conv_beta = convp_ref[2:3, :]
    rc = CONV_ROWS
    off = CONV_HALO - CONV_K // 2
    span = tile + 2 * CONV_HALO - SUBLANES
    for b in range(1, SUBLANES):
        cshift[b - 1] = cbuf[b:b + span, :]
    for c in range(tile // rc):
        acc = jnp.zeros((rc, CONV_W), F32)
        for k in range(CONV_K):
            a, b = divmod(off + k, SUBLANES)
            r0 = c * rc + a * SUBLANES
            win = cbuf[r0:r0 + rc, :] if b == 0 else cshift[b - 1, r0:r0 + rc, :]
            acc = acc + win * convw_ref[k:k + 1, :]
        y = _group_norm(acc + conv_b, gavg) * conv_g + conv_beta
        ya_ref[c * rc:(c + 1) * rc, :] = y * _sigmoid(y)

    qbuf[0:DN_HALO, :] = jnp.where(first, 0.0, qkv_p_ref[...])
    qbuf[DN_HALO:DN_HALO + tile, :] = qkv_ref[...]
    qbuf[DN_HALO + tile:2 * DN_HALO + tile, :] = jnp.where(last, 0.0, qkv_n_ref[...])
    rq = QKV_ROWS
    offq = DN_HALO - DN_SHORT_K // 2
    outs = (q_ref, k_ref, v_ref)
    for part in range(3):
        c0 = part * QK_W
        for c in range(tile // rq):
            acc = jnp.zeros((rq, QK_W), F32)
            for k in range(DN_SHORT_K):
                acc = acc + (qbuf[c * rq + offq + k:c * rq + offq + k + rq, c0:c0 + QK_W]
                             * dnw_ref[k:k + 1, c0:c0 + QK_W])
            a = acc * _sigmoid(acc)
            if part < 2:
                scale = DN_DK ** -0.5 if part == 0 else 1.0
                hs = []
                for h in range(DN_HEADS):
                    ah = a[:, h * DN_DK:(h + 1) * DN_DK]
                    nrm = lax.rsqrt(jnp.sum(ah * ah, axis=-1, keepdims=True) + EPS)
                    hs.append(ah * (nrm * scale))
                a = jnp.concatenate(hs, axis=-1)
            outs[part][c * rq:(c + 1) * rq, :] = a

    p = ba_ref[...]
    beta = _sigmoid(p)
    xg = p + gvec_ref[1:2, :]
    softplus = jnp.maximum(xg, 0.0) + jnp.log1p(jnp.exp(-jnp.abs(xg)))
    g = -jnp.exp(gvec_ref[0:1, :]) * softplus
    lane = lax.broadcasted_iota(jnp.int32, p.shape, 1)
    gates_ref[...] = jnp.where(lane < 2 * DN_HEADS, beta, g)

    pg = gm_ref[...]
    ge = pg * (0.5 * (1.0 + jnp.tanh(0.7978845608028654 * (pg + 0.044715 * (pg * pg * pg)))))
    u = ge[:, :GMLP_W]
    vn = _group_norm(ge[:, GMLP_W:], gavg) * gmp_ref[0:1, :] + gmp_ref[1:2, :]
    grp = lax.broadcasted_iota(jnp.int32, (GMLP_CHUNK, GMLP_W), 1) // GROUP_W
    for n in range(tile // GMLP_CHUNK):
        vchunk = vn[n * GMLP_CHUNK:(n + 1) * GMLP_CHUNK, :]
        sg = bsf_ref[...]
        for gi in range(GMLP_GROUPS):
            r = jnp.dot(ws_ref[gi], vchunk, preferred_element_type=F32)
            sg = sg + jnp.where(grp == gi, r, 0.0)
        yc_ref[n * GMLP_CHUNK:(n + 1) * GMLP_CHUNK, :] = u[n * GMLP_CHUNK:(n + 1) * GMLP_CHUNK, :] * sg


def _local_mixers(proj, lp, n_ctx_tiles, ctx_tps, den_tps):
    nt = proj.shape[0]
    tile = MIX_TILE
    n_tiles = nt // tile
    cpb = tile // CONV_HALO
    qpb = tile // DN_HALO
    n_cblk = nt // CONV_HALO
    n_qblk = nt // DN_HALO
    col = lambda c, w: c // w
    full = lambda a: pl.BlockSpec(a.shape, lambda i: (0,) * a.ndim)
    in_specs = [
        pl.BlockSpec((tile, 3 * QK_W), lambda i: (i, col(COL_QKV, 3 * QK_W))),
        pl.BlockSpec((DN_HALO, 3 * QK_W), lambda i: (jnp.maximum(i * qpb - 1, 0), 0)),
        pl.BlockSpec((DN_HALO, 3 * QK_W), lambda i: (jnp.minimum((i + 1) * qpb, n_qblk - 1), 0)),
        pl.BlockSpec((tile, 2 * CONV_W), lambda i: (i, col(COL_CONV, 2 * CONV_W))),
        pl.BlockSpec((CONV_HALO, 2 * CONV_W), lambda i: (jnp.maximum(i * cpb - 1, 0), col(COL_CONV, 2 * CONV_W))),
        pl.BlockSpec((CONV_HALO, 2 * CONV_W),
                     lambda i: (jnp.minimum((i + 1) * cpb, n_cblk - 1), col(COL_CONV, 2 * CONV_W))),
        pl.BlockSpec((tile, 2 * GMLP_W), lambda i: (i, col(COL_GM, 2 * GMLP_W))),
        pl.BlockSpec((tile, LANES), lambda i: (i, col(COL_BA, LANES))),
        full(lp["conv_dw"]), full(lp["conv_p"]), full(lp["dn_conv"]), full(lp["gvec"]),
        full(lp["gm_p"]), full(lp["gm_ws"]), full(lp["gm_bsf"]), full(lp["gavg"]),
    ]
    tok = lambda w: pl.BlockSpec((tile, w), lambda i: (i, 0))
    shp = lambda w: jax.ShapeDtypeStruct((nt, w), F32)
    return pl.pallas_call(
        functools.partial(_mixpre_kernel, tile=tile, n_ctx_tiles=n_ctx_tiles, ctx_tps=ctx_tps, den_tps=den_tps),
        grid=(n_tiles,),
        in_specs=in_specs,
        out_specs=[tok(CONV_W), tok(GMLP_W), tok(QK_W), tok(QK_W), tok(DN_W), tok(LANES)],
        out_shape=[shp(CONV_W), shp(GMLP_W), shp(QK_W), shp(QK_W), shp(DN_W), shp(LANES)],
        scratch_shapes=[pltpu.VMEM((tile + 2 * CONV_HALO, CONV_W), F32),
                        pltpu.VMEM((SUBLANES - 1, tile + 2 * CONV_HALO - SUBLANES, CONV_W), F32),
                        pltpu.VMEM((tile + 2 * DN_HALO, 3 * QK_W), F32)],
        compiler_params=pltpu.CompilerParams(vmem_limit_bytes=VMEM_LIMIT),
        name="local_mixers",
    )(proj, proj, proj, proj, proj, proj, proj, proj,
      lp["conv_dw"], lp["conv_p"], lp["dn_conv"], lp["gvec"], lp["gm_p"], lp["gm_ws"], lp["gm_bsf"], lp["gavg"])


DN_FIRST, DN_LAST_EMIT, DN_ZERO_INIT = 1, 2, 4


def _deltanet_kernel(tf_ref, tb_ref, flag_ref, s0i_ref, sfi_ref,
                     qf_ref, kf_ref, vf_ref, gf_ref, qb_ref, kb_ref, vb_ref, gb_ref, s0_ref,
                     of_ref, ob_ref, sfin_ref, s_ref, pq_scr, b_scr, o_scr, gt_scr, *, tile):
    flags = flag_ref[pl.program_id(0)]
    first = (flags & DN_FIRST) != 0
    zero_init = (flags & DN_ZERO_INIT) != 0
    c = DN_CHUNK
    n_chunks = tile // c
    upd = n_chunks * DN_HEADS

    @pl.when(first & zero_init)
    def _():
        s_ref[...] = jnp.zeros_like(s_ref)

    @pl.when(first & jnp.logical_not(zero_init))
    def _():
        for d in range(2):
            for h in range(DN_HEADS):
                s_ref[d * DN_HEADS + h] = s0_ref[0, 0, d, h]

    ti = lax.broadcasted_iota(jnp.int32, (tile, tile), 0)
    tj = lax.broadcasted_iota(jnp.int32, (tile, tile), 1)
    same_chunk = (ti // c) == (tj // c)
    nt_dims = (((1,), (1,)), ((), ()))
    tn_dims = (((0,), (0,)), ((), ()))
    dirs = (
        (qf_ref, kf_ref, vf_ref, gf_ref, of_ref, ti >= tj, c - 1),
        (qb_ref, kb_ref, vb_ref, gb_ref, ob_ref, ti <= tj, 0),
    )

    def gate_context(d):
        g_ref, tile_tri = dirs[d][3], dirs[d][5]
        gates = g_ref[...]
        blockcum = (same_chunk & tile_tri).astype(BF16)
        gc_t = _dot_exact_lhs(blockcum, gates)
        return gates, gc_t, gc_t.T

    gate_ctx = [gate_context(0), gate_context(1)]

    b16 = lambda x: x.astype(BF16)
    mm = lambda x, y: jnp.dot(x, y, preferred_element_type=F32)

    pi = lax.broadcasted_iota(jnp.int32, (c, 2 * c), 0)
    plane = lax.broadcasted_iota(jnp.int32, (c, 2 * c), 1)
    pj = plane & (c - 1)
    left = plane < c
    psame = lambda s: (pi >> s) == (pj >> s)
    pm8 = psame(3)
    pl16 = psame(4) & ~pm8
    pl32 = psame(5) & ~psame(4)
    pl64 = ~psame(5)
    ptri = ((pi >= pj, pi > pj), (pi <= pj, pi < pj))

    def blockdiag(y):
        return b16(jnp.concatenate([jnp.where(left, y, 0.0), jnp.where(left, 0.0, y)], axis=0))

    def blockdiag_wide(ya, yb):
        z = jnp.zeros_like(ya)
        return b16(jnp.concatenate([jnp.concatenate([ya, z], axis=1), jnp.concatenate([z, yb], axis=1)], axis=0))

    def prep():
        pairs = [(d, ci, p) for d in range(2) for ci in range(n_chunks) for p in range(DN_HEADS // 2)]
        chunk = {}
        for d in range(2):
            gc_t, last_row = gate_ctx[d][1], dirs[d][6]
            for ci in range(n_chunks):
                r0 = ci * c
                gc_c = gc_t[r0:r0 + c, :]
                glast = gc_t[r0 + last_row:r0 + last_row + 1, :]
                chunk[d, ci] = (gc_c, jnp.exp(gc_c), jnp.exp(glast - gc_c), jnp.exp(glast))
        lane_b = lambda d, h: d * DN_HEADS + h
        lane_g = lambda d, h: 2 * DN_HEADS + d * DN_HEADS + h
        col = lambda x, l: x[:, l:l + 1]
        n = range(len(pairs))
        heads = [(2 * p, 2 * p + 1) for _, _, p in pairs]

        def gj_pair(d, ci, ha, hb):
            gc_tt = gate_ctx[d][2]
            grow = lambda h: gc_tt[lane_g(d, h):lane_g(d, h) + 1, (ci // 2) * 2 * c:(ci // 2 + 1) * 2 * c]
            ra, rb = grow(ha), grow(hb)
            if ci % 2 == 0:
                rb = pltpu.roll(rb, c, axis=1)
            else:
                ra = pltpu.roll(ra, c, axis=1)
            return jnp.where(left[0:1, :], ra, rb)

        beta_h = [[col(gate_ctx[d][0][ci * c:(ci + 1) * c, :], lane_b(d, h)) for h in heads[i]]
                  for i, (d, ci, _) in enumerate(pairs)]
        eg_h = [[col(chunk[d, ci][1], lane_g(d, h)) for h in heads[i]] for i, (d, ci, _) in enumerate(pairs)]
        beta = [jnp.where(left, beta_h[i][0], beta_h[i][1]) for i in n]
        dmat = [jnp.where(ptri[d][0], jnp.exp(jnp.minimum(
            jnp.where(left, col(chunk[d, ci][0], lane_g(d, heads[i][0])), col(chunk[d, ci][0], lane_g(d, heads[i][1])))
            - gj_pair(d, ci, *heads[i]), 0.0)), 0.0) for i, (d, ci, _) in enumerate(pairs)]
        w2 = 2 * DN_DK
        q = [dirs[d][0][ci * c:(ci + 1) * c, p * w2:(p + 1) * w2] for d, ci, p in pairs]
        k = [dirs[d][1][ci * c:(ci + 1) * c, p * w2:(p + 1) * w2] for d, ci, p in pairs]
        v = [dirs[d][2][ci * c:(ci + 1) * c, p * w2:(p + 1) * w2] for d, ci, p in pairs]
        k16 = [b16(x) for x in k]
        kbd = [blockdiag_wide(x[:, :DN_DK], x[:, DN_DK:]) for x in k]
        kk = [lax.dot_general(k16[i], kbd[i], nt_dims, preferred_element_type=F32) for i in n]
        qk = [lax.dot_general(b16(q[i]), kbd[i], nt_dims, preferred_element_type=F32) for i in n]
        a = [jnp.where(ptri[pairs[i][0]][1], beta[i] * kk[i] * dmat[i], 0.0) for i in n]
        dd = [jnp.where(pm8, a[i], 0.0) for i in n]
        dd16 = [b16(x) for x in dd]
        d2 = [mm(dd16[i], blockdiag(dd[i])) for i in n]
        d2bd = [blockdiag(x) for x in d2]
        d3 = [mm(dd16[i], d2bd[i]) for i in n]
        d4 = [mm(b16(d2[i]), d2bd[i]) for i in n]
        e = [d2[i] - dd[i] - d3[i] for i in n]
        t = [mm(b16(e[i]), blockdiag(d4[i])) for i in n]
        e = [e[i] + d4[i] + t[i] for i in n]
        for sel in (pl16, pl32, pl64):
            l = [jnp.where(sel, a[i], 0.0) for i in n]
            ly = [l[i] + mm(b16(e[i]), blockdiag(l[i])) for i in n]
            z = [mm(b16(ly[i]), blockdiag(e[i])) for i in n]
            e = [e[i] - ly[i] - z[i] for i in n]
        half = lambda x, j: x[:, j * DN_DK:(j + 1) * DN_DK]
        r = [[jnp.concatenate([half(k[i], j) * (beta_h[i][j] * eg_h[i][j]), half(v[i], j) * beta_h[i][j]], axis=1)
              for j in range(2)] for i in n]
        wu = [jnp.concatenate(r[i], axis=1) + mm(b16(e[i]), blockdiag_wide(*r[i])) for i in n]
        wu_h = [[wu[i][:, j * w2:(j + 1) * w2] for j in range(2)] for i in n]
        qo = [mm(b16(qk[i] * dmat[i]), blockdiag_wide(*wu_h[i])) for i in n]
        units = [(i, j) for i in n for j in range(2)]
        kd = [b16(half(k[i], j) * col(chunk[pairs[i][0], pairs[i][1]][2], lane_g(pairs[i][0], heads[i][j])))
              for i, j in units]
        pb = [lax.dot_general(kd[m], b16(wu_h[i][j]), tn_dims, preferred_element_type=F32)
              for m, (i, j) in enumerate(units)]
        for m, (i, j) in enumerate(units):
            d, ci, h = pairs[i][0], pairs[i][1], heads[i][j]
            u = d * upd + ci * DN_HEADS + h
            qo_h = qo[i][:, j * w2:(j + 1) * w2]
            pq_scr[u, 0:DN_DK, :] = b16(pb[m][:, :DN_DK])
            pq_scr[u, DN_DK:DN_DK + c, :] = b16(half(q[i], j) * eg_h[i][j] - qo_h[:, :DN_DK])
            b_scr[u] = pb[m][:, DN_DK:]
            o_scr[u] = qo_h[:, DN_DK:]
            gt_scr[u] = jnp.broadcast_to(col(chunk[d, ci][3], lane_g(d, h)), (1, LANES))

    def scan(step):
        chains = [(0, step, h) for h in range(DN_HEADS)] + [(1, n_chunks - 1 - step, h) for h in range(DN_HEADS)]
        s = [s_ref[d * DN_HEADS + h] for d, ci, h in chains]
        ps = [mm(pq_scr[d * upd + ci * DN_HEADS + h], b16(s[i])) for i, (d, ci, h) in enumerate(chains)]
        for i, (d, ci, h) in enumerate(chains):
            u = d * upd + ci * DN_HEADS + h
            s_ref[d * DN_HEADS + h] = s[i] * gt_scr[u] - ps[i][0:DN_DK, :] + b_scr[u]
            dirs[d][4][ci * c:(ci + 1) * c, h * DN_DV:(h + 1) * DN_DV] = ps[i][DN_DK:DN_DK + c, :] + o_scr[u]

    prep()
    for step in range(n_chunks):
        scan(step)

    @pl.when((flags & DN_LAST_EMIT) != 0)
    def _():
        for d in range(2):
            for h in range(DN_HEADS):
                sfin_ref[0, d, h] = s_ref[d * DN_HEADS + h]


def _deltanet_schedule(batch, seq, dec_batch, dec_seq, tile):
    tf, tb, flags, s0i, sfi = [], [], [], [], []
    t0 = 0
    for n_seq, length, is_ctx in ((batch, seq, True), (dec_batch, dec_seq, False)):
        tps = length // tile
        for b in range(n_seq):
            for t in range(tps):
                tf.append(t0 + b * tps + t)
                tb.append(t0 + b * tps + tps - 1 - t)
                flags.append((DN_FIRST if t == 0 else 0)
                             | (DN_LAST_EMIT if (is_ctx and t == tps - 1) else 0)
                             | (DN_ZERO_INIT if is_ctx else 0))
                s0i.append(0 if is_ctx else b)
                sfi.append(b if is_ctx else batch - 1)
        t0 += n_seq * tps
    return [jnp.asarray(a, jnp.int32) for a in (tf, tb, flags, s0i, sfi)]


def _deltanet(q, k, v, gates, s0, layer, batch, seq, dec_batch, dec_seq):
    tile = MIX_TILE
    sched = _deltanet_schedule(batch, seq, dec_batch, dec_seq, tile)
    fwd = lambda w: pl.BlockSpec((tile, w), lambda s, tf, tb, fl, s0i, sfi: (tf[s], 0))
    bwd = lambda w: pl.BlockSpec((tile, w), lambda s, tf, tb, fl, s0i, sfi: (tb[s], 0))
    state_block = (2, DN_HEADS, DN_DK, DN_DV)
    units = 2 * (tile // DN_CHUNK) * DN_HEADS
    grid_spec = pltpu.PrefetchScalarGridSpec(
        num_scalar_prefetch=len(sched),
        grid=(sched[0].shape[0],),
        in_specs=[fwd(QK_W), fwd(QK_W), fwd(DN_W), fwd(LANES), bwd(QK_W), bwd(QK_W), bwd(DN_W), bwd(LANES),
                  pl.BlockSpec((1, 1) + state_block,
                               lambda s, tf, tb, fl, s0i, sfi: (s0i[s], layer, 0, 0, 0, 0))],
        out_specs=[fwd(DN_W), bwd(DN_W),
                   pl.BlockSpec((1,) + state_block, lambda s, tf, tb, fl, s0i, sfi: (sfi[s], 0, 0, 0, 0))],
        scratch_shapes=[pltpu.VMEM((2 * DN_HEADS, DN_DK, DN_DV), F32),
                        pltpu.VMEM((units, DN_DK + DN_CHUNK, DN_DV), BF16),
                        pltpu.VMEM((units, DN_DK, DN_DV), F32),
                        pltpu.VMEM((units, DN_CHUNK, DN_DV), F32),
                        pltpu.VMEM((units, 1, LANES), F32)],
    )
    return pl.pallas_call(
        functools.partial(_deltanet_kernel, tile=tile),
        grid_spec=grid_spec,
        out_shape=[jax.ShapeDtypeStruct((q.shape[0], DN_W), F32)] * 2
        + [jax.ShapeDtypeStruct((batch,) + state_block, F32)],
        compiler_params=pltpu.CompilerParams(dimension_semantics=("arbitrary",), vmem_limit_bytes=VMEM_LIMIT),
        name="deltanet_scan",
    )(*sched, q, k, v, gates, q, k, v, gates, s0)


def _outproj_kernel(*refs, tile, n_ctx, dec_seq):
    (*x_refs, ya_ref, of_ref, ob_ref, z_ref, yc_ref, mod_ref, wout_ref, vec_ref, dng_ref,
     wr_hi_ref, wr_lo_ref, br_ref, x1_ref, h_ref, route_ref, cnt_ref, run_ref) = refs
    start = pl.program_id(0) * tile
    m = _mod_row(mod_ref, start, n_ctx, dec_seq)
    gt1 = m[:, 2 * D_MODEL:3 * D_MODEL]
    sh2 = m[:, 3 * D_MODEL:4 * D_MODEL]
    sc2 = m[:, 4 * D_MODEL:5 * D_MODEL]

    @pl.when(pl.program_id(0) == 0)
    def _():
        run_ref[...] = jnp.zeros_like(run_ref)

    rows_per_part = tile // OUTPROJ_PARTS
    lane = lax.broadcasted_iota(jnp.int32, (rows_per_part, LANES), 1).astype(F32)
    ri = lax.broadcasted_iota(jnp.int32, (rows_per_part, rows_per_part), 0)
    rj = lax.broadcasted_iota(jnp.int32, (rows_per_part, rows_per_part), 1)
    earlier = (ri > rj).astype(BF16)
    parts = []
    for part in range(OUTPROJ_PARTS):
        rows = slice(part * rows_per_part, (part + 1) * rows_per_part)
        o = of_ref[rows, :] + ob_ref[rows, :]
        z = z_ref[rows, :]
        zz = z * _sigmoid(z)
        hs = []
        for h in range(DN_HEADS):
            oh = o[:, h * DN_DV:(h + 1) * DN_DV]
            on = oh * lax.rsqrt(jnp.mean(oh * oh, axis=-1, keepdims=True) + EPS) * dng_ref[...]
            hs.append(on * zz[:, h * DN_DV:(h + 1) * DN_DV])
        ycat = jnp.concatenate([ya_ref[rows, :]] + hs + [yc_ref[rows, :]], axis=-1).astype(BF16)
        y = jnp.dot(ycat, wout_ref[...], preferred_element_type=F32)
        x = _load_tokens(x_refs, rows, start < n_ctx)
        x1 = _ln(DEEPNORM_ALPHA * x + gt1 * y) * vec_ref[0:1, :] + vec_ref[1:2, :]
        x1_ref[rows, :] = x1
        hf = _ln(x1) * (1.0 + sc2) + sh2
        h_hi = hf.astype(BF16)
        h_ref[rows, :] = h_hi
        logits = (jnp.dot(h_hi, wr_hi_ref[...], preferred_element_type=F32)
                  + jnp.dot(h_hi, wr_lo_ref[...], preferred_element_type=F32)
                  + br_ref[...])
        vals, idxs = [], []
        for _ in range(TOP_K):
            top = jnp.max(logits, axis=-1, keepdims=True)
            idx = jnp.min(jnp.where(logits == top, lane, float(LANES)), axis=-1, keepdims=True)
            vals.append(top)
            idxs.append(idx)
            logits = jnp.where(lane == idx, -jnp.inf, logits)
        es = [jnp.exp(v - vals[0]) for v in vals]
        den = (es[0] + es[1]) + (es[2] + es[3])
        onehot = jnp.zeros((rows_per_part, LANES), F32)
        for idx in idxs:
            onehot = onehot + (lane == idx).astype(F32)
        before = jnp.dot(earlier, onehot.astype(BF16), preferred_element_type=F32)
        parts.append((rows, idxs, [e / den for e in es], before, jnp.sum(onehot, axis=0, keepdims=True)))

    run = run_ref[...]
    for rows, idxs, gates, before, count in parts:
        before = before + run
        run = run + count
        route = jnp.zeros((rows_per_part, LANES), F32)
        for j in range(TOP_K):
            rank = jnp.sum(jnp.where(lane == idxs[j], before, 0.0), axis=-1, keepdims=True)
            route = jnp.where(lane == float(j), idxs[j], route)
            route = jnp.where(lane == float(TOP_K + j), gates[j], route)
            route = jnp.where(lane == float(2 * TOP_K + j), rank, route)
        route_ref[rows, :] = route
    run_ref[...] = run
    cnt_ref[...] = jnp.broadcast_to(run, cnt_ref.shape)


def _out_projection(x_src, ya, o_f, o_b, proj, yc, mod, lp, n_ctx, dec_seq):
    nt, d = ya.shape[0], x_src[0].shape[1]
    tile = TOK_TILE
    tok = lambda w: pl.BlockSpec((tile, w), lambda i: (i, 0))
    full = lambda a: pl.BlockSpec(a.shape, lambda i: (0,) * a.ndim)
    return pl.pallas_call(
        functools.partial(_outproj_kernel, tile=tile, n_ctx=n_ctx, dec_seq=dec_seq),
        grid=(nt // tile,),
        in_specs=_token_specs(x_src, tile, n_ctx) + [
            tok(CONV_W), tok(DN_W), tok(DN_W),
            pl.BlockSpec((tile, DN_W), lambda i: (i, COL_Z // DN_W)),
            tok(GMLP_W), full(mod), full(lp["w_out"]), full(lp["ln1"]), full(lp["dn_norm_g"]),
            full(lp["wr_hi"]), full(lp["wr_lo"]), full(lp["b_router"])],
        out_specs=[tok(d), tok(d), tok(LANES), pl.BlockSpec((SUBLANES, LANES), lambda i: (0, 0))],
        out_shape=[jax.ShapeDtypeStruct((nt, d), F32), jax.ShapeDtypeStruct((nt, d), BF16),
                   jax.ShapeDtypeStruct((nt, LANES), F32), jax.ShapeDtypeStruct((SUBLANES, LANES), F32)],
        scratch_shapes=[pltpu.VMEM((1, LANES), F32)],
        compiler_params=pltpu.CompilerParams(dimension_semantics=("arbitrary",), vmem_limit_bytes=VMEM_LIMIT),
        name="out_projection_router",
    )(*x_src, ya, o_f, o_b, proj, yc, mod, lp["w_out"], lp["ln1"], lp["dn_norm_g"],
      lp["wr_hi"], lp["wr_lo"], lp["b_router"])


def _expert_kernel(be_ref, nu_ref, first_ref, next_ref, slot_ref, x_ref, wgu_hbm, bgu_ref, wd_hbm, bd_ref,
                   out_ref, wgu_buf, wd_buf, wgu16, wd16, sem, *, layer):
    i = pl.program_id(0)

    def weight_copies(expert, slot):
        return (pltpu.make_async_copy(wgu_hbm.at[layer, expert], wgu_buf.at[slot], sem.at[0, slot]),
                pltpu.make_async_copy(wd_hbm.at[layer, expert], wd_buf.at[slot], sem.at[1, slot]))

    @pl.when(first_ref[i] == 1)
    def _():
        slot = slot_ref[i]

        @pl.when(i == 0)
        def _():
            for cp in weight_copies(be_ref[0], slot):
                cp.start()

        for cp in weight_copies(be_ref[i], slot):
            cp.wait()

        @pl.when(next_ref[i] >= 0)
        def _():
            for cp in weight_copies(next_ref[i], 1 - slot):
                cp.start()

        wgu16[...] = wgu_buf[slot].astype(BF16)
        wd16[...] = wd_buf[slot].astype(BF16)

    @pl.when(i < nu_ref[0])
    def _():
        gu = jnp.dot(x_ref[...], wgu16[...], preferred_element_type=F32) + bgu_ref[0, 0]
        gate = jnp.minimum(gu[:, :D_FF], SWIGLU_LIMIT)
        up = jnp.clip(gu[:, D_FF:], -SWIGLU_LIMIT, SWIGLU_LIMIT)
        act = gate * _sigmoid(SWIGLU_ALPHA * gate)
        hmid = ((up + 1.0) * act).astype(BF16)
        y = jnp.dot(hmid, wd16[...], preferred_element_type=F32) + bd_ref[0, 0]
        out_ref[...] = y.astype(out_ref.dtype)

    @pl.when(i >= nu_ref[0])
    def _():
        out_ref[...] = jnp.zeros_like(out_ref)


def _expert_runs(block_e, n_used):
    nb = block_e.shape[0]
    idx = jnp.arange(nb, dtype=jnp.int32)
    prev_e = jnp.concatenate([jnp.full((1,), -1, jnp.int32), block_e[:-1]])
    first = ((idx < n_used[0]) & (block_e != prev_e)).astype(jnp.int32)
    slot = (jnp.cumsum(first) - 1) % 2
    first_pos = jnp.where(first == 1, idx, nb)
    later = jnp.concatenate([first_pos[1:], jnp.full((1,), nb, jnp.int32)])
    next_pos = lax.cummin(later, axis=0, reverse=True)
    next_e = jnp.where(next_pos < nb, block_e[jnp.minimum(next_pos, nb - 1)], -1)
    return first, next_e.astype(jnp.int32), slot.astype(jnp.int32)


def _experts(xg, block_e, n_used, w_gu, b_gu, w_down, b_down, layer):
    m_pad, d = xg.shape
    bm = MOE_BM
    first, next_e, slot = _expert_runs(block_e, n_used)
    pre = lambda f: (lambda i, be, nu, fi, ne, sl: f(i, be))
    grid_spec = pltpu.PrefetchScalarGridSpec(
        num_scalar_prefetch=5,
        grid=(m_pad // bm,),
        in_specs=[
            pl.BlockSpec((bm, d), pre(lambda i, be: (i, 0))),
            pl.BlockSpec(memory_space=pl.ANY),
            pl.BlockSpec((1, 1, 1, 2 * D_FF), pre(lambda i, be: (layer, be[i], 0, 0))),
            pl.BlockSpec(memory_space=pl.ANY),
            pl.BlockSpec((1, 1, 1, d), pre(lambda i, be: (layer, be[i], 0, 0))),
        ],
        out_specs=pl.BlockSpec((bm, d), pre(lambda i, be: (i, 0))),
        scratch_shapes=[pltpu.VMEM((2, d, 2 * D_FF), F32), pltpu.VMEM((2, D_FF, d), F32),
                        pltpu.VMEM((d, 2 * D_FF), BF16), pltpu.VMEM((D_FF, d), BF16),
                        pltpu.SemaphoreType.DMA((2, 2))],
    )
    return pl.pallas_call(
        functools.partial(_expert_kernel, layer=layer),
        grid_spec=grid_spec,
        out_shape=jax.ShapeDtypeStruct((m_pad, d), BF16),
        compiler_params=pltpu.CompilerParams(dimension_semantics=("arbitrary",), vmem_limit_bytes=VMEM_LIMIT),
        name="moe_experts",
    )(block_e, n_used, first, next_e, slot, xg, w_gu, b_gu, w_down, b_down)


def _combine_kernel(yg_ref, route_ref, x1_ref, mod_ref, vec_ref, *out_refs, tile, n_ctx, dec_seq):
    start = pl.program_id(0) * tile
    m = _mod_row(mod_ref, start, n_ctx, dec_seq)
    gt2 = m[:, 5 * D_MODEL:6 * D_MODEL]
    gate = lambda j: route_ref[:, TOP_K + j:TOP_K + j + 1]
    ye = lambda j: yg_ref[j].astype(F32) * gate(j)
    y = (ye(0) + ye(1)) + (ye(2) + ye(3))
    res = _ln(DEEPNORM_ALPHA * x1_ref[...] + gt2 * y) * vec_ref[0:1, :] + vec_ref[1:2, :]
    if len(out_refs) == 1:
        out_refs[0][...] = res
    else:
        @pl.when(start < n_ctx)
        def _():
            out_refs[0][...] = res

        @pl.when(start >= n_ctx)
        def _():
            out_refs[1][...] = res


def _combine(yg, route, x1, mod, ln2, n_ctx, dec_seq, split_streams):
    nt, d = x1.shape
    tile = TOK_TILE
    ct = n_ctx // tile
    if split_streams:
        out_specs = [pl.BlockSpec((tile, d), lambda i: (jnp.minimum(i, ct - 1), 0)),
                     pl.BlockSpec((tile, d), lambda i: (jnp.maximum(i - ct, 0), 0))]
        out_shape = [jax.ShapeDtypeStruct((n_ctx, d), F32), jax.ShapeDtypeStruct((nt - n_ctx, d), F32)]
    else:
        out_specs = pl.BlockSpec((tile, d), lambda i: (i, 0))
        out_shape = jax.ShapeDtypeStruct((nt, d), F32)
    return pl.pallas_call(
        functools.partial(_combine_kernel, tile=tile, n_ctx=n_ctx, dec_seq=dec_seq),
        grid=(nt // tile,),
        in_specs=[pl.BlockSpec((TOP_K, tile, d), lambda i: (0, i, 0)),
                  pl.BlockSpec((tile, LANES), lambda i: (i, 0)),
                  pl.BlockSpec((tile, d), lambda i: (i, 0)),
                  pl.BlockSpec(mod.shape, lambda i: (0, 0)),
                  pl.BlockSpec(ln2.shape, lambda i: (0, 0))],
        out_specs=out_specs,
        out_shape=out_shape,
        compiler_params=pltpu.CompilerParams(dimension_semantics=("arbitrary",), vmem_limit_bytes=VMEM_LIMIT),
        name="moe_combine_ln",
    )(yg, route, x1, mod, ln2)


def _route(route, counts):
    nt = route.shape[0]
    nk = nt * TOP_K
    bm = MOE_BM
    expert = route[:, 0:TOP_K].astype(jnp.int32)
    rank = route[:, 2 * TOP_K:3 * TOP_K].astype(jnp.int32)
    counts = counts[0, :N_EXPERTS].astype(jnp.int32)
    padded = (counts + bm - 1) // bm * bm
    pad_end = jnp.cumsum(padded)
    pad_start = pad_end - padded
    pair_slot = pad_start[expert] + rank
    n_blocks = nk // bm + N_EXPERTS
    m_pad = n_blocks * bm
    pad_tok = (1 << TOKEN_BITS) - 1
    assert nt <= pad_tok
    tok = jnp.arange(nt, dtype=jnp.int32)[:, None]
    pad_cum = jnp.cumsum(padded - counts)
    pad_id = jnp.arange(m_pad - nk, dtype=jnp.int32)
    pad_expert = jnp.sum((pad_cum[None, :] <= pad_id[:, None]).astype(jnp.int32), axis=1)
    keys = jnp.concatenate([(expert * (1 << TOKEN_BITS) + tok).reshape(-1), pad_expert * (1 << TOKEN_BITS) + pad_tok])
    slot_key = jnp.sort(keys) & pad_tok
    slot_tok = jnp.where(slot_key == pad_tok, jnp.arange(m_pad, dtype=jnp.int32) % nt, slot_key)
    blk_start = jnp.arange(n_blocks, dtype=jnp.int32) * bm
    block_e = jnp.minimum(jnp.sum((pad_end[None, :] <= blk_start[:, None]).astype(jnp.int32), axis=1),
                          N_EXPERTS - 1)
    n_used = (pad_end[-1] // bm).astype(jnp.int32).reshape(1)
    return slot_tok, pair_slot, block_e.astype(jnp.int32), n_used


def _grid_pos_embed(t, d):
    rows = t // GRID_W
    r, col = jnp.meshgrid(jnp.arange(rows), jnp.arange(GRID_W), indexing="ij")
    r = r.reshape(-1).astype(F32)[:, None]
    col = col.reshape(-1).astype(F32)[:, None]
    n_freq = d // 4
    omega = 1.0 / (POS_BASE ** (jnp.arange(n_freq, dtype=F32) / n_freq))
    return jnp.concatenate([jnp.sin(r * omega), jnp.cos(r * omega),
                            jnp.sin(col * omega), jnp.cos(col * omega)], axis=-1)


def _pad_lanes(a, offset):
    return jnp.zeros((1, LANES), F32).at[0, offset:offset + a.shape[0]].set(a.astype(F32))


def _layer_params(l, w_in, conv_dw, conv_b, conv_ln_g, conv_ln_b, dn_conv, dn_a_log, dn_dt_bias, dn_norm_g,
                  gm_ln_g, gm_ln_b, gm_ws, gm_bs, w_out, ln1_g, ln1_b, ln2_g, ln2_b, w_router, b_router):
    wi = w_in[l]
    c_conv, c_qkv, c_z, c_ba = 0, 2 * CONV_W, 2 * CONV_W + 2 * QK_W + DN_W, 2 * CONV_W + 2 * QK_W + 2 * DN_W
    c_gm = c_ba + 4 * DN_HEADS
    w_in_r = jnp.concatenate([
        wi[:, c_qkv:c_z], wi[:, c_conv:c_qkv], wi[:, c_z:c_ba], wi[:, c_gm:],
        wi[:, c_ba:c_gm], jnp.zeros((D_MODEL, LANES - 4 * DN_HEADS), F32)], axis=1).astype(BF16)
    grp = jnp.arange(CONV_W) // GROUP_W
    gavg = (grp[:, None] == grp[None, :]).astype(BF16) * (1.0 / GROUP_W)
    gvec = jnp.concatenate([_pad_lanes(dn_a_log[l].reshape(-1), 2 * DN_HEADS),
                            _pad_lanes(dn_dt_bias[l].reshape(-1), 2 * DN_HEADS)], axis=0)
    wr = jnp.zeros((D_MODEL, LANES), F32).at[:, :N_EXPERTS].set(w_router[l])
    wr_hi = wr.astype(BF16)
    wr_lo = (wr - wr_hi.astype(F32)).astype(BF16)
    return {
        "w_in_r": w_in_r,
        "conv_dw": conv_dw[l],
        "conv_p": jnp.stack([conv_b[l], conv_ln_g[l], conv_ln_b[l]]),
        "dn_conv": dn_conv[l],
        "gvec": gvec,
        "gm_p": jnp.stack([gm_ln_g[l], gm_ln_b[l]]),
        "gm_ws": gm_ws[l],
        "gm_bsf": jnp.repeat(jnp.transpose(gm_bs[l]), GROUP_W, axis=1),
        "gavg": gavg,
        "w_out": w_out[l].astype(BF16),
        "ln1": jnp.stack([ln1_g[l], ln1_b[l]]),
        "ln2": jnp.stack([ln2_g[l], ln2_b[l]]),
        "dn_norm_g": dn_norm_g[l].reshape(1, DN_DV),
        "wr_hi": wr_hi,
        "wr_lo": wr_lo,
        "b_router": jnp.full((1, LANES), -1e30, F32).at[0, :N_EXPERTS].set(b_router[l]),
    }


def kernel(x_prompt, x_sample, state_delta, c, c_ctx, w_ada, b_ada, w_in, conv_dw, conv_b, conv_ln_g, conv_ln_b, dn_conv, dn_a_log, dn_dt_bias, dn_norm_g, gm_ln_g, gm_ln_b, gm_ws, gm_bs, w_out, ln1_g, ln1_b, ln2_g, ln2_b, w_router, b_router, w_gu, b_gu, w_down, b_down):
    batch, seq, d = x_prompt.shape
    dec_batch, dec_seq, _ = x_sample.shape
    n_ctx = batch * seq
    n_den = dec_batch * dec_seq
    depth = w_in.shape[0]

    assert 1 + dec_batch <= MOD_ROWS and seq % MIX_TILE == 0 and dec_seq % TOK_TILE == 0 and n_ctx % TOK_TILE == 0
    cond = jnp.zeros((MOD_ROWS, d), F32).at[0].set(c_ctx).at[1:1 + dec_batch].set(c)
    mod_all = _modulation(cond, w_ada, b_ada)
    b_gu_r = b_gu.reshape(depth, N_EXPERTS, 1, 2 * D_FF)
    b_down_r = b_down.reshape(depth, N_EXPERTS, 1, d)

    nt = n_ctx + n_den
    x_src = (x_prompt.reshape(n_ctx, d), x_sample.reshape(n_den, d), _grid_pos_embed(dec_seq, d))
    ctx_states = []
    for l in range(depth):
        lp = _layer_params(l, w_in, conv_dw, conv_b, conv_ln_g, conv_ln_b, dn_conv, dn_a_log, dn_dt_bias,
                           dn_norm_g, gm_ln_g, gm_ln_b, gm_ws, gm_bs, w_out, ln1_g, ln1_b, ln2_g, ln2_b,
                           w_router, b_router)
        mod = mod_all[l]
        proj = _in_projection(x_src, nt, mod, lp["w_in_r"], n_ctx, dec_seq)
        ya, yc, q, k, v, gates = _local_mixers(proj, lp, n_ctx // MIX_TILE, seq // MIX_TILE, dec_seq // MIX_TILE)
        o_f, o_b, s_fin = _deltanet(q, k, v, gates, state_delta, l, batch, seq, dec_batch, dec_seq)
        ctx_states.append(s_fin)
        x1, hffn, route, counts = _out_projection(x_src, ya, o_f, o_b, proj, yc, mod, lp, n_ctx, dec_seq)
        slot_tok, pair_slot, block_e, n_used = _route(route, counts)
        xg = hffn.at[slot_tok].get(mode="promise_in_bounds")
        yb = _experts(xg, block_e, n_used, w_gu, b_gu_r, w_down, b_down_r, l)
        yg = yb.at[jnp.transpose(pair_slot)].get(mode="promise_in_bounds")
        out = _combine(yg, route, x1, mod, lp["ln2"], n_ctx, dec_seq, split_streams=(l == depth - 1))
        x_src = (out,)

    new_state = jnp.stack(ctx_states, axis=1).astype(x_prompt.dtype)
    return (out[0].reshape(batch, seq, d), out[1].reshape(dec_batch, dec_seq, d), new_state)
```

```python
import functools

import jax
import jax.numpy as jnp
from jax import lax
from jax.experimental import pallas as pl
from jax.experimental.pallas import tpu as pltpu

F32 = jnp.float32
BF16 = jnp.bfloat16

D_MODEL = 1024
DEPTH = 2
GRID_W = 64
POS_BASE = 10000.0
CONV_W = 256
CONV_K = 31
CONV_HALO = 16
DN_HEADS = 4
DN_DK = 128
DN_DV = 128
QK_W = DN_HEADS * DN_DK
DN_W = DN_HEADS * DN_DV
DN_SHORT_K = 5
DN_HALO = 8
DN_CHUNK = 64
GMLP_W = 256
GMLP_GROUPS = 4
GMLP_CHUNK = 128
GROUP_W = 64
N_EXPERTS = 32
TOP_K = 4
D_FF = D_MODEL
SWIGLU_ALPHA = 1.702
SWIGLU_LIMIT = 7.0
DEEPNORM_ALPHA = (2 * DEPTH) ** 0.25
EPS = 1e-6

LANES = 128
SUBLANES = 8
COL_QKV = 0
COL_CONV = 2 * QK_W + DN_W
COL_Z = COL_CONV + 2 * CONV_W
COL_GM = COL_Z + DN_W
COL_BA = COL_GM + 2 * GMLP_W
PROJ_W = COL_BA + LANES

MIX_TILE = 256
TOK_TILE = 512
OUTPROJ_PARTS = 1
MOE_BM = 512
TOKEN_BITS = 16
MOD_ROWS = 16
MOD_TILE_N = 1024
CONV_ROWS = 64
QKV_ROWS = 32
V7X_VMEM_BYTES = 64 * 1024 * 1024
VMEM_LIMIT = V7X_VMEM_BYTES * 7 // 8


def _ln(x):
    mu = jnp.mean(x, axis=-1, keepdims=True)
    xc = x - mu
    return xc * lax.rsqrt(jnp.mean(xc * xc, axis=-1, keepdims=True) + EPS)


def _sigmoid(x):
    return jax.nn.sigmoid(x)


def _split_bf16(x, parts):
    out = []
    r = x
    for _ in range(parts):
        p = r.astype(BF16)
        out.append(p)
        r = r - p.astype(F32)
    return out


def _dot_exact_rhs(x, m_bf16, parts=3):
    acc = None
    for p in _split_bf16(x, parts):
        t = jnp.dot(p, m_bf16, preferred_element_type=F32)
        acc = t if acc is None else acc + t
    return acc


def _dot_exact_lhs(m_bf16, x, parts=3):
    acc = None
    for p in _split_bf16(x, parts):
        t = jnp.dot(m_bf16, p, preferred_element_type=F32)
        acc = t if acc is None else acc + t
    return acc


def _mod_row(mod_ref, start, n_ctx, dec_seq):
    row = jnp.where(start < n_ctx, 0, 1 + (start - n_ctx) // dec_seq)
    return mod_ref[pl.ds(row, 1), :]


def _mod_kernel(cond_ref, w_ref, b_ref, out_ref):
    c = cond_ref[...]
    s = c * _sigmoid(c)
    out_ref[0] = jnp.dot(s, w_ref[0], preferred_element_type=F32) + b_ref[0]


def _modulation(cond, w_ada, b_ada):
    nl, d, n = w_ada.shape
    r = cond.shape[0]
    tn = MOD_TILE_N
    return pl.pallas_call(
        _mod_kernel,
        grid=(nl, n // tn),
        in_specs=[
            pl.BlockSpec((r, d), lambda l, j: (0, 0)),
            pl.BlockSpec((1, d, tn), lambda l, j: (l, 0, j)),
            pl.BlockSpec((1, 1, tn), lambda l, j: (l, 0, j)),
        ],
        out_specs=pl.BlockSpec((1, r, tn), lambda l, j: (l, 0, j)),
        out_shape=jax.ShapeDtypeStruct((nl, r, n), F32),
        name="adaln_modulation",
    )(cond, w_ada, b_ada.reshape(nl, 1, n))


def _token_specs(x_src, tile, n_ctx):
    d = x_src[0].shape[1]
    if len(x_src) == 1:
        return [pl.BlockSpec((tile, d), lambda i: (i, 0))]
    ct = n_ctx // tile
    pt = x_src[2].shape[0] // tile
    return [pl.BlockSpec((tile, d), lambda i: (jnp.minimum(i, max(ct - 1, 0)), 0)),
            pl.BlockSpec((tile, d), lambda i: (jnp.maximum(i - ct, 0), 0)),
            pl.BlockSpec((tile, d), lambda i: (jnp.maximum(i - ct, 0) % pt, 0))]


def _load_tokens(x_refs, rows, is_ctx_tile):
    if len(x_refs) == 1:
        return x_refs[0][rows, :]
    xp_ref, xs_ref, pos_ref = x_refs
    return jnp.where(is_ctx_tile, xp_ref[rows, :], xs_ref[rows, :] + pos_ref[rows, :])


def _inproj_kernel(*refs, tile, n_ctx, dec_seq):
    *x_refs, mod_ref, w_ref, out_ref = refs
    start = pl.program_id(0) * tile
    m = _mod_row(mod_ref, start, n_ctx, dec_seq)
    sh1 = m[:, 0:D_MODEL]
    sc1 = m[:, D_MODEL:2 * D_MODEL]
    h = _ln(_load_tokens(x_refs, slice(None), start < n_ctx)) * (1.0 + sc1) + sh1
    out_ref[...] = jnp.dot(h.astype(BF16), w_ref[...], preferred_element_type=F32)


def _in_projection(x_src, nt, mod, w_in_r, n_ctx, dec_seq):
    tile = TOK_TILE
    return pl.pallas_call(
        functools.partial(_inproj_kernel, tile=tile, n_ctx=n_ctx, dec_seq=dec_seq),
        grid=(nt // tile,),
        in_specs=_token_specs(x_src, tile, n_ctx) + [
            pl.BlockSpec(mod.shape, lambda i: (0, 0)),
            pl.BlockSpec(w_in_r.shape, lambda i: (0, 0)),
        ],
        out_specs=pl.BlockSpec((tile, PROJ_W), lambda i: (i, 0)),
        out_shape=jax.ShapeDtypeStruct((nt, PROJ_W), F32),
        compiler_params=pltpu.CompilerParams(vmem_limit_bytes=VMEM_LIMIT),
        name="in_projection",
    )(*x_src, mod, w_in_r)


def _group_norm(x, gavg):
    mean = _dot_exact_rhs(x, gavg, parts=2)
    xc = x - mean
    var = _dot_exact_rhs(xc * xc, gavg, parts=2)
    return xc * lax.rsqrt(var + EPS)


def _mixpre_kernel(qkv_ref, qkv_p_ref, qkv_n_ref, cv_ref, cv_p_ref, cv_n_ref, gm_ref, ba_ref,
                   convw_ref, convp_ref, dnw_ref, gvec_ref, gmp_ref, ws_ref, bsf_ref, gavg_ref,
                   ya_ref, yc_ref, q_ref, k_ref, v_ref, gates_ref, cbuf, cshift, qbuf,
                   *, tile, n_ctx_tiles, ctx_tps, den_tps):
    i = pl.program_id(0)
    pos = jnp.where(i < n_ctx_tiles, i % ctx_tps, (i - n_ctx_tiles) % den_tps)
    tps = jnp.where(i < n_ctx_tiles, ctx_tps, den_tps)
    first = pos == 0
    last = pos == tps - 1
    gavg = gavg_ref[...]

    def glu(p):
        return p[:, :CONV_W] * _sigmoid(p[:, CONV_W:])

    cbuf[0:CONV_HALO, :] = jnp.where(first, 0.0, glu(cv_p_ref[...]))
    cbuf[CONV_HALO:CONV_HALO + tile, :] = glu(cv_ref[...])
    cbuf[CONV_HALO + tile:2 * CONV_HALO + tile, :] = jnp.where(last, 0.0, glu(cv_n_ref[...]))
    conv_b = convp_ref[0:1, :]
    conv_g = convp_ref[1:2, :]
    conv_beta = convp_ref[2:3, :]
    rc = CONV_ROWS
    off = CONV_HALO - CONV_K // 2
    span = tile + 2 * CONV_HALO - SUBLANES
    for b in range(1, SUBLANES):
        cshift[b - 1] = cbuf[b:b + span, :]
    for c in range(tile // rc):
        acc = jnp.zeros((rc, CONV_W), F32)
        for k in range(CONV_K):
            a, b = divmod(off + k, SUBLANES)
            r0 = c * rc + a * SUBLANES
            win = cbuf[r0:r0 + rc, :] if b == 0 else cshift[b - 1, r0:r0 + rc, :]
            acc = acc + win * convw_ref[k:k + 1, :]
        y = _group_norm(acc + conv_b, gavg) * conv_g + conv_beta
        ya_ref[c * rc:(c + 1) * rc, :] = y * _sigmoid(y)

    qbuf[0:DN_HALO, :] = jnp.where(first, 0.0, qkv_p_ref[...])
    qbuf[DN_HALO:DN_HALO + tile, :] = qkv_ref[...]
    qbuf[DN_HALO + tile:2 * DN_HALO + tile, :] = jnp.where(last, 0.0, qkv_n_ref[...])
    rq = QKV_ROWS
    offq = DN_HALO - DN_SHORT_K // 2
    outs = (q_ref, k_ref, v_ref)
    for part in range(3):
        c0 = part * QK_W
        for c in range(tile // rq):
            acc = jnp.zeros((rq, QK_W), F32)
            for k in range(DN_SHORT_K):
                acc = acc + (qbuf[c * rq + offq + k:c * rq + offq + k + rq, c0:c0 + QK_W]
                             * dnw_ref[k:k + 1, c0:c0 + QK_W])
            a = acc * _sigmoid(acc)
            if part < 2:
                scale = DN_DK ** -0.5 if part == 0 else 1.0
                hs = []
                for h in range(DN_HEADS):
                    ah = a[:, h * DN_DK:(h + 1) * DN_DK]
                    nrm = lax.rsqrt(jnp.sum(ah * ah, axis=-1, keepdims=True) + EPS)
                    hs.append(ah * (nrm * scale))
                a = jnp.concatenate(hs, axis=-1)
            outs[part][c * rq:(c + 1) * rq, :] = a

    p = ba_ref[...]
    beta = _sigmoid(p)
    xg = p + gvec_ref[1:2, :]
    softplus = jnp.maximum(xg, 0.0) + jnp.log1p(jnp.exp(-jnp.abs(xg)))
    g = -jnp.exp(gvec_ref[0:1, :]) * softplus
    lane = lax.broadcasted_iota(jnp.int32, p.shape, 1)
    gates_ref[...] = jnp.where(lane < 2 * DN_HEADS, beta, g)

    pg = gm_ref[...]
    ge = pg * (0.5 * (1.0 + jnp.tanh(0.7978845608028654 * (pg + 0.044715 * (pg * pg * pg)))))
    u = ge[:, :GMLP_W]
    vn = _group_norm(ge[:, GMLP_W:], gavg) * gmp_ref[0:1, :] + gmp_ref[1:2, :]
    grp = lax.broadcasted_iota(jnp.int32, (GMLP_CHUNK, GMLP_W), 1) // GROUP_W
    for n in range(tile // GMLP_CHUNK):
        vchunk = vn[n * GMLP_CHUNK:(n + 1) * GMLP_CHUNK, :]
        stacked = jnp.concatenate([jnp.where(grp == gi, vchunk, 0.0).astype(BF16) for gi in range(GMLP_GROUPS)], axis=0)
        sg = bsf_ref[...] + jnp.dot(ws_ref[...], stacked, preferred_element_type=F32)
        yc_ref[n * GMLP_CHUNK:(n + 1) * GMLP_CHUNK, :] = u[n * GMLP_CHUNK:(n + 1) * GMLP_CHUNK, :] * sg


def _local_mixers(proj, lp, n_ctx_tiles, ctx_tps, den_tps):
    nt = proj.shape[0]
    tile = MIX_TILE
    n_tiles = nt // tile
    cpb = tile // CONV_HALO
    qpb = tile // DN_HALO
    n_cblk = nt // CONV_HALO
    n_qblk = nt // DN_HALO
    col = lambda c, w: c // w
    full = lambda a: pl.BlockSpec(a.shape, lambda i: (0,) * a.ndim)
    in_specs = [
        pl.BlockSpec((tile, 3 * QK_W), lambda i: (i, col(COL_QKV, 3 * QK_W))),
        pl.BlockSpec((DN_HALO, 3 * QK_W), lambda i: (jnp.maximum(i * qpb - 1, 0), 0)),
        pl.BlockSpec((DN_HALO, 3 * QK_W), lambda i: (jnp.minimum((i + 1) * qpb, n_qblk - 1), 0)),
        pl.BlockSpec((tile, 2 * CONV_W), lambda i: (i, col(COL_CONV, 2 * CONV_W))),
        pl.BlockSpec((CONV_HALO, 2 * CONV_W), lambda i: (jnp.maximum(i * cpb - 1, 0), col(COL_CONV, 2 * CONV_W))),
        pl.BlockSpec((CONV_HALO, 2 * CONV_W),
                     lambda i: (jnp.minimum((i + 1) * cpb, n_cblk - 1), col(COL_CONV, 2 * CONV_W))),
        pl.BlockSpec((tile, 2 * GMLP_W), lambda i: (i, col(COL_GM, 2 * GMLP_W))),
        pl.BlockSpec((tile, LANES), lambda i: (i, col(COL_BA, LANES))),
        full(lp["conv_dw"]), full(lp["conv_p"]), full(lp["dn_conv"]), full(lp["gvec"]),
        full(lp["gm_p"]), full(lp["gm_ws"]), full(lp["gm_bsf"]), full(lp["gavg"]),
    ]
    tok = lambda w: pl.BlockSpec((tile, w), lambda i: (i, 0))
    shp = lambda w: jax.ShapeDtypeStruct((nt, w), F32)
    return pl.pallas_call(
        functools.partial(_mixpre_kernel, tile=tile, n_ctx_tiles=n_ctx_tiles, ctx_tps=ctx_tps, den_tps=den_tps),
        grid=(n_tiles,),
        in_specs=in_specs,
        out_specs=[tok(CONV_W), tok(GMLP_W), tok(QK_W), tok(QK_W), tok(DN_W), tok(LANES)],
        out_shape=[shp(CONV_W), shp(GMLP_W), shp(QK_W), shp(QK_W), shp(DN_W), shp(LANES)],
        scratch_shapes=[pltpu.VMEM((tile + 2 * CONV_HALO, CONV_W), F32),
                        pltpu.VMEM((SUBLANES - 1, tile + 2 * CONV_HALO - SUBLANES, CONV_W), F32),
                        pltpu.VMEM((tile + 2 * DN_HALO, 3 * QK_W), F32)],
        compiler_params=pltpu.CompilerParams(vmem_limit_bytes=VMEM_LIMIT),
        name="local_mixers",
    )(proj, proj, proj, proj, proj, proj, proj, proj,
      lp["conv_dw"], lp["conv_p"], lp["dn_conv"], lp["gvec"], lp["gm_p"], lp["gm_ws"], lp["gm_bsf"], lp["gavg"])


DN_FIRST, DN_LAST_EMIT, DN_ZERO_INIT = 1, 2, 4


def _deltanet_kernel(tf_ref, tb_ref, flag_ref, s0i_ref, sfi_ref,
                     qf_ref, kf_ref, vf_ref, gf_ref, qb_ref, kb_ref, vb_ref, gb_ref, s0_ref,
                     of_ref, ob_ref, sfin_ref, s_ref, pq_scr, b_scr, o_scr, gt_scr, *, tile):
    flags = flag_ref[pl.program_id(0)]
    first = (flags & DN_FIRST) != 0
    zero_init = (flags & DN_ZERO_INIT) != 0
    c = DN_CHUNK
    n_chunks = tile // c
    upd = n_chunks * DN_HEADS

    @pl.when(first & zero_init)
    def _():
        s_ref[...] = jnp.zeros_like(s_ref)

    @pl.when(first & jnp.logical_not(zero_init))
    def _():
        for d in range(2):
            for h in range(DN_HEADS):
                s_ref[d * DN_HEADS + h] = s0_ref[0, 0, d, h]

    ti = lax.broadcasted_iota(jnp.int32, (tile, tile), 0)
    tj = lax.broadcasted_iota(jnp.int32, (tile, tile), 1)
    same_chunk = (ti // c) == (tj // c)
    nt_dims = (((1,), (1,)), ((), ()))
    tn_dims = (((0,), (0,)), ((), ()))
    dirs = (
        (qf_ref, kf_ref, vf_ref, gf_ref, of_ref, ti >= tj, c - 1),
        (qb_ref, kb_ref, vb_ref, gb_ref, ob_ref, ti <= tj, 0),
    )

    def gate_context(d):
        g_ref, tile_tri = dirs[d][3], dirs[d][5]
        gates = g_ref[...]
        blockcum = (same_chunk & tile_tri).astype(BF16)
        gc_t = _dot_exact_lhs(blockcum, gates)
        return gates, gc_t, gc_t.T

    gate_ctx = [gate_context(0), gate_context(1)]

    b16 = lambda x: x.astype(BF16)
    mm = lambda x, y: jnp.dot(x, y, preferred_element_type=F32)

    pi = lax.broadcasted_iota(jnp.int32, (c, 2 * c), 0)
    plane = lax.broadcasted_iota(jnp.int32, (c, 2 * c), 1)
    pj = plane & (c - 1)
    left = plane < c
    psame = lambda s: (pi >> s) == (pj >> s)
    pm8 = psame(3)
    pl16 = psame(4) & ~pm8
    pl32 = psame(5) & ~psame(4)
    pl64 = ~psame(5)
    ptri = ((pi >= pj, pi > pj), (pi <= pj, pi < pj))

    def blockdiag(y):
        return b16(jnp.concatenate([jnp.where(left, y, 0.0), jnp.where(left, 0.0, y)], axis=0))

    def blockdiag_wide(ya, yb):
        z = jnp.zeros_like(ya)
        return b16(jnp.concatenate([jnp.concatenate([ya, z], axis=1), jnp.concatenate([z, yb], axis=1)], axis=0))

    def prep():
        pairs = [(d, ci, p) for d in range(2) for ci in range(n_chunks) for p in range(DN_HEADS // 2)]
        chunk = {}
        for d in range(2):
            gc_t, last_row = gate_ctx[d][1], dirs[d][6]
            for ci in range(n_chunks):
                r0 = ci * c
                gc_c = gc_t[r0:r0 + c, :]
                glast = gc_t[r0 + last_row:r0 + last_row + 1, :]
                chunk[d, ci] = (gc_c, jnp.exp(gc_c), jnp.exp(glast - gc_c), jnp.exp(glast))
        lane_b = lambda d, h: d * DN_HEADS + h
        lane_g = lambda d, h: 2 * DN_HEADS + d * DN_HEADS + h
        col = lambda x, l: x[:, l:l + 1]
        n = range(len(pairs))
        heads = [(2 * p, 2 * p + 1) for _, _, p in pairs]

        def gj_pair(d, ci, ha, hb):
            gc_tt = gate_ctx[d][2]
            grow = lambda h: gc_tt[lane_g(d, h):lane_g(d, h) + 1, (ci // 2) * 2 * c:(ci // 2 + 1) * 2 * c]
            ra, rb = grow(ha), grow(hb)
            if ci % 2 == 0:
                rb = pltpu.roll(rb, c, axis=1)
            else:
                ra = pltpu.roll(ra, c, axis=1)
            return jnp.where(left[0:1, :], ra, rb)

        beta_h = [[col(gate_ctx[d][0][ci * c:(ci + 1) * c, :], lane_b(d, h)) for h in heads[i]]
                  for i, (d, ci, _) in enumerate(pairs)]
        eg_h = [[col(chunk[d, ci][1], lane_g(d, h)) for h in heads[i]] for i, (d, ci, _) in enumerate(pairs)]
        beta = [jnp.where(left, beta_h[i][0], beta_h[i][1]) for i in n]
        dmat = [jnp.where(ptri[d][0], jnp.exp(jnp.minimum(
            jnp.where(left, col(chunk[d, ci][0], lane_g(d, heads[i][0])), col(chunk[d, ci][0], lane_g(d, heads[i][1])))
            - gj_pair(d, ci, *heads[i]), 0.0)), 0.0) for i, (d, ci, _) in enumerate(pairs)]
        w2 = 2 * DN_DK
        q = [dirs[d][0][ci * c:(ci + 1) * c, p * w2:(p + 1) * w2] for d, ci, p in pairs]
        k = [dirs[d][1][ci * c:(ci + 1) * c, p * w2:(p + 1) * w2] for d, ci, p in pairs]
        v = [dirs[d][2][ci * c:(ci + 1) * c, p * w2:(p + 1) * w2] for d, ci, p in pairs]
        k16 = [b16(x) for x in k]
        kbd = [blockdiag_wide(x[:, :DN_DK], x[:, DN_DK:]) for x in k]
        kk = [lax.dot_general(k16[i], kbd[i], nt_dims, preferred_element_type=F32) for i in n]
        qk = [lax.dot_general(b16(q[i]), kbd[i], nt_dims, preferred_element_type=F32) for i in n]
        a = [jnp.where(ptri[pairs[i][0]][1], beta[i] * kk[i] * dmat[i], 0.0) for i in n]
        dd = [jnp.where(pm8, a[i], 0.0) for i in n]
        dd16 = [b16(x) for x in dd]
        d2 = [mm(dd16[i], blockdiag(dd[i])) for i in n]
        d2bd = [blockdiag(x) for x in d2]
        d3 = [mm(dd16[i], d2bd[i]) for i in n]
        d4 = [mm(b16(d2[i]), d2bd[i]) for i in n]
        e = [d2[i] - dd[i] - d3[i] for i in n]
        t = [mm(b16(e[i]), blockdiag(d4[i])) for i in n]
        e = [e[i] + d4[i] + t[i] for i in n]
        for sel in (pl16, pl32, pl64):
            l = [jnp.where(sel, a[i], 0.0) for i in n]
            ly = [l[i] + mm(b16(e[i]), blockdiag(l[i])) for i in n]
            z = [mm(b16(ly[i]), blockdiag(e[i])) for i in n]
            e = [e[i] - ly[i] - z[i] for i in n]
        half = lambda x, j: x[:, j * DN_DK:(j + 1) * DN_DK]
        r = [[jnp.concatenate([half(k[i], j) * (beta_h[i][j] * eg_h[i][j]), half(v[i], j) * beta_h[i][j]], axis=1)
              for j in range(2)] for i in n]
        wu = [jnp.concatenate(r[i], axis=1) + mm(b16(e[i]), blockdiag_wide(*r[i])) for i in n]
        wu_h = [[wu[i][:, j * w2:(j + 1) * w2] for j in range(2)] for i in n]
        qo = [mm(b16(qk[i] * dmat[i]), blockdiag_wide(*wu_h[i])) for i in n]
        units = [(i, j) for i in n for j in range(2)]
        kd = [b16(half(k[i], j) * col(chunk[pairs[i][0], pairs[i][1]][2], lane_g(pairs[i][0], heads[i][j])))
              for i, j in units]
        pb = [lax.dot_general(kd[m], b16(wu_h[i][j]), tn_dims, preferred_element_type=F32)
              for m, (i, j) in enumerate(units)]
        for m, (i, j) in enumerate(units):
            d, ci, h = pairs[i][0], pairs[i][1], heads[i][j]
            u = d * upd + ci * DN_HEADS + h
            qo_h = qo[i][:, j * w2:(j + 1) * w2]
            pq_scr[u, 0:DN_DK, :] = b16(pb[m][:, :DN_DK])
            pq_scr[u, DN_DK:DN_DK + c, :] = b16(half(q[i], j) * eg_h[i][j] - qo_h[:, :DN_DK])
            b_scr[u] = pb[m][:, DN_DK:]
            o_scr[u] = qo_h[:, DN_DK:]
            gt_scr[u] = jnp.broadcast_to(col(chunk[d, ci][3], lane_g(d, h)), (1, LANES))

    def scan(step):
        chains = [(0, step, h) for h in range(DN_HEADS)] + [(1, n_chunks - 1 - step, h) for h in range(DN_HEADS)]
        s = [s_ref[d * DN_HEADS + h] for d, ci, h in chains]
        ps = [mm(pq_scr[d * upd + ci * DN_HEADS + h], b16(s[i])) for i, (d, ci, h) in enumerate(chains)]
        for i, (d, ci, h) in enumerate(chains):
            u = d * upd + ci * DN_HEADS + h
            s_ref[d * DN_HEADS + h] = s[i] * gt_scr[u] - ps[i][0:DN_DK, :] + b_scr[u]
            dirs[d][4][ci * c:(ci + 1) * c, h * DN_DV:(h + 1) * DN_DV] = ps[i][DN_DK:DN_DK + c, :] + o_scr[u]

    prep()
    for step in range(n_chunks):
        scan(step)

    @pl.when((flags & DN_LAST_EMIT) != 0)
    def _():
        for d in range(2):
            for h in range(DN_HEADS):
                sfin_ref[0, d, h] = s_ref[d * DN_HEADS + h]


def _deltanet_schedule(batch, seq, dec_batch, dec_seq, tile):
    tf, tb, flags, s0i, sfi = [], [], [], [], []
    t0 = 0
    for n_seq, length, is_ctx in ((batch, seq, True), (dec_batch, dec_seq, False)):
        tps = length // tile
        for b in range(n_seq):
            for t in range(tps):
                tf.append(t0 + b * tps + t)
                tb.append(t0 + b * tps + tps - 1 - t)
                flags.append((DN_FIRST if t == 0 else 0)
                             | (DN_LAST_EMIT if (is_ctx and t == tps - 1) else 0)
                             | (DN_ZERO_INIT if is_ctx else 0))
                s0i.append(0 if is_ctx else b)
                sfi.append(b if is_ctx else batch - 1)
        t0 += n_seq * tps
    return [jnp.asarray(a, jnp.int32) for a in (tf, tb, flags, s0i, sfi)]


def _deltanet(q, k, v, gates, s0, layer, batch, seq, dec_batch, dec_seq):
    tile = MIX_TILE
    sched = _deltanet_schedule(batch, seq, dec_batch, dec_seq, tile)
    fwd = lambda w: pl.BlockSpec((tile, w), lambda s, tf, tb, fl, s0i, sfi: (tf[s], 0))
    bwd = lambda w: pl.BlockSpec((tile, w), lambda s, tf, tb, fl, s0i, sfi: (tb[s], 0))
    state_block = (2, DN_HEADS, DN_DK, DN_DV)
    units = 2 * (tile // DN_CHUNK) * DN_HEADS
    grid_spec = pltpu.PrefetchScalarGridSpec(
        num_scalar_prefetch=len(sched),
        grid=(sched[0].shape[0],),
        in_specs=[fwd(QK_W), fwd(QK_W), fwd(DN_W), fwd(LANES), bwd(QK_W), bwd(QK_W), bwd(DN_W), bwd(LANES),
                  pl.BlockSpec((1, 1) + state_block,
                               lambda s, tf, tb, fl, s0i, sfi: (s0i[s], layer, 0, 0, 0, 0))],
        out_specs=[fwd(DN_W), bwd(DN_W),
                   pl.BlockSpec((1,) + state_block, lambda s, tf, tb, fl, s0i, sfi: (sfi[s], 0, 0, 0, 0))],
        scratch_shapes=[pltpu.VMEM((2 * DN_HEADS, DN_DK, DN_DV), F32),
                        pltpu.VMEM((units, DN_DK + DN_CHUNK, DN_DV), BF16),
                        pltpu.VMEM((units, DN_DK, DN_DV), F32),
                        pltpu.VMEM((units, DN_CHUNK, DN_DV), F32),
                        pltpu.VMEM((units, 1, LANES), F32)],
    )
    return pl.pallas_call(
        functools.partial(_deltanet_kernel, tile=tile),
        grid_spec=grid_spec,
        out_shape=[jax.ShapeDtypeStruct((q.shape[0], DN_W), F32)] * 2
        + [jax.ShapeDtypeStruct((batch,) + state_block, F32)],
        compiler_params=pltpu.CompilerParams(dimension_semantics=("arbitrary",), vmem_limit_bytes=VMEM_LIMIT),
        name="deltanet_scan",
    )(*sched, q, k, v, gates, q, k, v, gates, s0)


def _outproj_kernel(*refs, tile, n_ctx, dec_seq):
    (*x_refs, ya_ref, of_ref, ob_ref, z_ref, yc_ref, mod_ref, wout_ref, vec_ref, dng_ref,
     wr_hi_ref, wr_lo_ref, br_ref, x1_ref, h_ref, route_ref, cnt_ref, run_ref) = refs
    start = pl.program_id(0) * tile
    m = _mod_row(mod_ref, start, n_ctx, dec_seq)
    gt1 = m[:, 2 * D_MODEL:3 * D_MODEL]
    sh2 = m[:, 3 * D_MODEL:4 * D_MODEL]
    sc2 = m[:, 4 * D_MODEL:5 * D_MODEL]

    @pl.when(pl.program_id(0) == 0)
    def _():
        run_ref[...] = jnp.zeros_like(run_ref)

    rows_per_part = tile // OUTPROJ_PARTS
    lane = lax.broadcasted_iota(jnp.int32, (rows_per_part, LANES), 1).astype(F32)
    ri = lax.broadcasted_iota(jnp.int32, (rows_per_part, rows_per_part), 0)
    rj = lax.broadcasted_iota(jnp.int32, (rows_per_part, rows_per_part), 1)
    earlier = (ri > rj).astype(BF16)
    parts = []
    for part in range(OUTPROJ_PARTS):
        rows = slice(part * rows_per_part, (part + 1) * rows_per_part)
        o = of_ref[rows, :] + ob_ref[rows, :]
        z = z_ref[rows, :]
        zz = z * _sigmoid(z)
        hs = []
        for h in range(DN_HEADS):
            oh = o[:, h * DN_DV:(h + 1) * DN_DV]
            on = oh * lax.rsqrt(jnp.mean(oh * oh, axis=-1, keepdims=True) + EPS) * dng_ref[...]
            hs.append(on * zz[:, h * DN_DV:(h + 1) * DN_DV])
        ycat = jnp.concatenate([ya_ref[rows, :]] + hs + [yc_ref[rows, :]], axis=-1).astype(BF16)
        y = jnp.dot(ycat, wout_ref[...], preferred_element_type=F32)
        x = _load_tokens(x_refs, rows, start < n_ctx)
        x1 = _ln(DEEPNORM_ALPHA * x + gt1 * y) * vec_ref[0:1, :] + vec_ref[1:2, :]
        x1_ref[rows, :] = x1
        hf = _ln(x1) * (1.0 + sc2) + sh2
        h_hi = hf.astype(BF16)
        h_ref[rows, :] = h_hi
        lg = jnp.dot(h_hi, jnp.concatenate([wr_hi_ref[...], wr_lo_ref[...]], axis=1), preferred_element_type=F32)
        logits = lg[:, :LANES] + lg[:, LANES:] + br_ref[...]
        vals, idxs = [], []
        for _ in range(TOP_K):
            top = jnp.max(logits, axis=-1, keepdims=True)
            idx = jnp.min(jnp.where(logits == top, lane, float(LANES)), axis=-1, keepdims=True)
            vals.append(top)
            idxs.append(idx)
            logits = jnp.where(lane == idx, -jnp.inf, logits)
        es = [jnp.exp(v - vals[0]) for v in vals]
        den = (es[0] + es[1]) + (es[2] + es[3])
        onehot = jnp.zeros((rows_per_part, LANES), F32)
        for idx in idxs:
            onehot = onehot + (lane == idx).astype(F32)
        before = jnp.dot(earlier, onehot.astype(BF16), preferred_element_type=F32)
        parts.append((rows, idxs, [e / den for e in es], before, jnp.sum(onehot, axis=0, keepdims=True)))

    run = run_ref[...]
    for rows, idxs, gates, before, count in parts:
        before = before + run
        run = run + count
        route = jnp.zeros((rows_per_part, LANES), F32)
        for j in range(TOP_K):
            rank = jnp.sum(jnp.where(lane == idxs[j], before, 0.0), axis=-1, keepdims=True)
            route = jnp.where(lane == float(j), idxs[j], route)
            route = jnp.where(lane == float(TOP_K + j), gates[j], route)
            route = jnp.where(lane == float(2 * TOP_K + j), rank, route)
        route_ref[rows, :] = route
    run_ref[...] = run
    cnt_ref[...] = jnp.broadcast_to(run, cnt_ref.shape)


def _out_projection(x_src, ya, o_f, o_b, proj, yc, mod, lp, n_ctx, dec_seq):
    nt, d = ya.shape[0], x_src[0].shape[1]
    tile = TOK_TILE
    tok = lambda w: pl.BlockSpec((tile, w), lambda i: (i, 0))
    full = lambda a: pl.BlockSpec(a.shape, lambda i: (0,) * a.ndim)
    return pl.pallas_call(
        functools.partial(_outproj_kernel, tile=tile, n_ctx=n_ctx, dec_seq=dec_seq),
        grid=(nt // tile,),
        in_specs=_token_specs(x_src, tile, n_ctx) + [
            tok(CONV_W), tok(DN_W), tok(DN_W),
            pl.BlockSpec((tile, DN_W), lambda i: (i, COL_Z // DN_W)),
            tok(GMLP_W), full(mod), full(lp["w_out"]), full(lp["ln1"]), full(lp["dn_norm_g"]),
            full(lp["wr_hi"]), full(lp["wr_lo"]), full(lp["b_router"])],
        out_specs=[tok(d), tok(d), tok(LANES), pl.BlockSpec((SUBLANES, LANES), lambda i: (0, 0))],
        out_shape=[jax.ShapeDtypeStruct((nt, d), F32), jax.ShapeDtypeStruct((nt, d), BF16),
                   jax.ShapeDtypeStruct((nt, LANES), F32), jax.ShapeDtypeStruct((SUBLANES, LANES), F32)],
        scratch_shapes=[pltpu.VMEM((1, LANES), F32)],
        compiler_params=pltpu.CompilerParams(dimension_semantics=("arbitrary",), vmem_limit_bytes=VMEM_LIMIT),
        name="out_projection_router",
    )(*x_src, ya, o_f, o_b, proj, yc, mod, lp["w_out"], lp["ln1"], lp["dn_norm_g"],
      lp["wr_hi"], lp["wr_lo"], lp["b_router"])


def _expert_kernel(be_ref, nu_ref, first_ref, next_ref, slot_ref, x_ref, wgu_hbm, bgu_ref, wd_hbm, bd_ref,
                   out_ref, wgu_buf, wd_buf, wgu16, wd16, sem, *, layer):
    i = pl.program_id(0)

    def weight_copies(expert, slot):
        return (pltpu.make_async_copy(wgu_hbm.at[layer, expert], wgu_buf.at[slot], sem.at[0, slot]),
                pltpu.make_async_copy(wd_hbm.at[layer, expert], wd_buf.at[slot], sem.at[1, slot]))

    @pl.when(first_ref[i] == 1)
    def _():
        slot = slot_ref[i]

        @pl.when(i == 0)
        def _():
            for cp in weight_copies(be_ref[0], slot):
                cp.start()

        for cp in weight_copies(be_ref[i], slot):
            cp.wait()

        @pl.when(next_ref[i] >= 0)
        def _():
            for cp in weight_copies(next_ref[i], 1 - slot):
                cp.start()

        wgu16[...] = wgu_buf[slot].astype(BF16)
        wd16[...] = wd_buf[slot].astype(BF16)

    @pl.when(i < nu_ref[0])
    def _():
        gu = jnp.dot(x_ref[...], wgu16[...], preferred_element_type=F32) + bgu_ref[0, 0]
        gate = jnp.minimum(gu[:, :D_FF], SWIGLU_LIMIT)
        up = jnp.clip(gu[:, D_FF:], -SWIGLU_LIMIT, SWIGLU_LIMIT)
        act = gate * _sigmoid(SWIGLU_ALPHA * gate)
        hmid = ((up + 1.0) * act).astype(BF16)
        y = jnp.dot(hmid, wd16[...], preferred_element_type=F32) + bd_ref[0, 0]
        out_ref[...] = y.astype(out_ref.dtype)

    @pl.when(i >= nu_ref[0])
    def _():
        out_ref[...] = jnp.zeros_like(out_ref)


def _expert_runs(block_e, n_used):
    nb = block_e.shape[0]
    idx = jnp.arange(nb, dtype=jnp.int32)
    prev_e = jnp.concatenate([jnp.full((1,), -1, jnp.int32), block_e[:-1]])
    first = ((idx < n_used[0]) & (block_e != prev_e)).astype(jnp.int32)
    slot = (jnp.cumsum(first) - 1) % 2
    first_pos = jnp.where(first == 1, idx, nb)
    later = jnp.concatenate([first_pos[1:], jnp.full((1,), nb, jnp.int32)])
    next_pos = lax.cummin(later, axis=0, reverse=True)
    next_e = jnp.where(next_pos < nb, block_e[jnp.minimum(next_pos, nb - 1)], -1)
    return first, next_e.astype(jnp.int32), slot.astype(jnp.int32)


def _experts(xg, block_e, n_used, w_gu, b_gu, w_down, b_down, layer):
    m_pad, d = xg.shape
    bm = MOE_BM
    first, next_e, slot = _expert_runs(block_e, n_used)
    pre = lambda f: (lambda i, be, nu, fi, ne, sl: f(i, be))
    grid_spec = pltpu.PrefetchScalarGridSpec(
        num_scalar_prefetch=5,
        grid=(m_pad // bm,),
        in_specs=[
            pl.BlockSpec((bm, d), pre(lambda i, be: (i, 0))),
            pl.BlockSpec(memory_space=pl.ANY),
            pl.BlockSpec((1, 1, 1, 2 * D_FF), pre(lambda i, be: (layer, be[i], 0, 0))),
            pl.BlockSpec(memory_space=pl.ANY),
            pl.BlockSpec((1, 1, 1, d), pre(lambda i, be: (layer, be[i], 0, 0))),
        ],
        out_specs=pl.BlockSpec((bm, d), pre(lambda i, be: (i, 0))),
        scratch_shapes=[pltpu.VMEM((2, d, 2 * D_FF), F32), pltpu.VMEM((2, D_FF, d), F32),
                        pltpu.VMEM((d, 2 * D_FF), BF16), pltpu.VMEM((D_FF, d), BF16),
                        pltpu.SemaphoreType.DMA((2, 2))],
    )
    return pl.pallas_call(
        functools.partial(_expert_kernel, layer=layer),
        grid_spec=grid_spec,
        out_shape=jax.ShapeDtypeStruct((m_pad, d), BF16),
        compiler_params=pltpu.CompilerParams(dimension_semantics=("arbitrary",), vmem_limit_bytes=VMEM_LIMIT),
        name="moe_experts",
    )(block_e, n_used, first, next_e, slot, xg, w_gu, b_gu, w_down, b_down)


def _combine_kernel(yg_ref, route_ref, x1_ref, mod_ref, vec_ref, *out_refs, tile, n_ctx, dec_seq):
    start = pl.program_id(0) * tile
    m = _mod_row(mod_ref, start, n_ctx, dec_seq)
    gt2 = m[:, 5 * D_MODEL:6 * D_MODEL]
    gate = lambda j: route_ref[:, TOP_K + j:TOP_K + j + 1]
    ye = lambda j: yg_ref[j].astype(F32) * gate(j)
    y = (ye(0) + ye(1)) + (ye(2) + ye(3))
    res = _ln(DEEPNORM_ALPHA * x1_ref[...] + gt2 * y) * vec_ref[0:1, :] + vec_ref[1:2, :]
    if len(out_refs) == 1:
        out_refs[0][...] = res
    else:
        @pl.when(start < n_ctx)
        def _():
            out_refs[0][...] = res

        @pl.when(start >= n_ctx)
        def _():
            out_refs[1][...] = res


def _combine(yg, route, x1, mod, ln2, n_ctx, dec_seq, split_streams):
    nt, d = x1.shape
    tile = TOK_TILE
    ct = n_ctx // tile
    if split_streams:
        out_specs = [pl.BlockSpec((tile, d), lambda i: (jnp.minimum(i, ct - 1), 0)),
                     pl.BlockSpec((tile, d), lambda i: (jnp.maximum(i - ct, 0), 0))]
        out_shape = [jax.ShapeDtypeStruct((n_ctx, d), F32), jax.ShapeDtypeStruct((nt - n_ctx, d), F32)]
    else:
        out_specs = pl.BlockSpec((tile, d), lambda i: (i, 0))
        out_shape = jax.ShapeDtypeStruct((nt, d), F32)
    return pl.pallas_call(
        functools.partial(_combine_kernel, tile=tile, n_ctx=n_ctx, dec_seq=dec_seq),
        grid=(nt // tile,),
        in_specs=[pl.BlockSpec((TOP_K, tile, d), lambda i: (0, i, 0)),
                  pl.BlockSpec((tile, LANES), lambda i: (i, 0)),
                  pl.BlockSpec((tile, d), lambda i: (i, 0)),
                  pl.BlockSpec(mod.shape, lambda i: (0, 0)),
                  pl.BlockSpec(ln2.shape, lambda i: (0, 0))],
        out_specs=out_specs,
        out_shape=out_shape,
        compiler_params=pltpu.CompilerParams(dimension_semantics=("arbitrary",), vmem_limit_bytes=VMEM_LIMIT),
        name="moe_combine_ln",
    )(yg, route, x1, mod, ln2)


def _route(route, counts):
    nt = route.shape[0]
    nk = nt * TOP_K
    bm = MOE_BM
    expert = route[:, 0:TOP_K].astype(jnp.int32)
    rank = route[:, 2 * TOP_K:3 * TOP_K].astype(jnp.int32)
    counts = counts[0, :N_EXPERTS].astype(jnp.int32)
    padded = (counts + bm - 1) // bm * bm
    pad_end = jnp.cumsum(padded)
    pad_start = pad_end - padded
    pair_slot = pad_start[expert] + rank
    n_blocks = nk // bm + N_EXPERTS
    m_pad = n_blocks * bm
    pad_tok = (1 << TOKEN_BITS) - 1
    assert nt <= pad_tok
    tok = jnp.arange(nt, dtype=jnp.int32)[:, None]
    pad_cum = jnp.cumsum(padded - counts)
    pad_id = jnp.arange(m_pad - nk, dtype=jnp.int32)
    pad_expert = jnp.sum((pad_cum[None, :] <= pad_id[:, None]).astype(jnp.int32), axis=1)
    keys = jnp.concatenate([(expert * (1 << TOKEN_BITS) + tok).reshape(-1), pad_expert * (1 << TOKEN_BITS) + pad_tok])
    slot_key = jnp.sort(keys) & pad_tok
    slot_tok = jnp.where(slot_key == pad_tok, jnp.arange(m_pad, dtype=jnp.int32) % nt, slot_key)
    blk_start = jnp.arange(n_blocks, dtype=jnp.int32) * bm
    block_e = jnp.minimum(jnp.sum((pad_end[None, :] <= blk_start[:, None]).astype(jnp.int32), axis=1),
                          N_EXPERTS - 1)
    n_used = (pad_end[-1] // bm).astype(jnp.int32).reshape(1)
    return slot_tok, pair_slot, block_e.astype(jnp.int32), n_used


def _grid_pos_embed(t, d):
    rows = t // GRID_W
    r, col = jnp.meshgrid(jnp.arange(rows), jnp.arange(GRID_W), indexing="ij")
    r = r.reshape(-1).astype(F32)[:, None]
    col = col.reshape(-1).astype(F32)[:, None]
    n_freq = d // 4
    omega = 1.0 / (POS_BASE ** (jnp.arange(n_freq, dtype=F32) / n_freq))
    return jnp.concatenate([jnp.sin(r * omega), jnp.cos(r * omega),
                            jnp.sin(col * omega), jnp.cos(col * omega)], axis=-1)


def _pad_lanes(a, offset):
    return jnp.zeros((1, LANES), F32).at[0, offset:offset + a.shape[0]].set(a.astype(F32))


def _layer_params(l, w_in, conv_dw, conv_b, conv_ln_g, conv_ln_b, dn_conv, dn_a_log, dn_dt_bias, dn_norm_g,
                  gm_ln_g, gm_ln_b, gm_ws, gm_bs, w_out, ln1_g, ln1_b, ln2_g, ln2_b, w_router, b_router):
    wi = w_in[l]
    c_conv, c_qkv, c_z, c_ba = 0, 2 * CONV_W, 2 * CONV_W + 2 * QK_W + DN_W, 2 * CONV_W + 2 * QK_W + 2 * DN_W
    c_gm = c_ba + 4 * DN_HEADS
    w_in_r = jnp.concatenate([
        wi[:, c_qkv:c_z], wi[:, c_conv:c_qkv], wi[:, c_z:c_ba], wi[:, c_gm:],
        wi[:, c_ba:c_gm], jnp.zeros((D_MODEL, LANES - 4 * DN_HEADS), F32)], axis=1).astype(BF16)
    grp = jnp.arange(CONV_W) // GROUP_W
    gavg = (grp[:, None] == grp[None, :]).astype(BF16) * (1.0 / GROUP_W)
    gvec = jnp.concatenate([_pad_lanes(dn_a_log[l].reshape(-1), 2 * DN_HEADS),
                            _pad_lanes(dn_dt_bias[l].reshape(-1), 2 * DN_HEADS)], axis=0)
    wr = jnp.zeros((D_MODEL, LANES), F32).at[:, :N_EXPERTS].set(w_router[l])
    wr_hi = wr.astype(BF16)
    wr_lo = (wr - wr_hi.astype(F32)).astype(BF16)
    return {
        "w_in_r": w_in_r,
        "conv_dw": conv_dw[l],
        "conv_p": jnp.stack([conv_b[l], conv_ln_g[l], conv_ln_b[l]]),
        "dn_conv": dn_conv[l],
        "gvec": gvec,
        "gm_p": jnp.stack([gm_ln_g[l], gm_ln_b[l]]),
        "gm_ws": jnp.transpose(gm_ws[l], (1, 0, 2)).reshape(GMLP_CHUNK, GMLP_GROUPS * GMLP_CHUNK).astype(BF16),
        "gm_bsf": jnp.repeat(jnp.transpose(gm_bs[l]), GROUP_W, axis=1),
        "gavg": gavg,
        "w_out": w_out[l].astype(BF16),
        "ln1": jnp.stack([ln1_g[l], ln1_b[l]]),
        "ln2": jnp.stack([ln2_g[l], ln2_b[l]]),
        "dn_norm_g": dn_norm_g[l].reshape(1, DN_DV),
        "wr_hi": wr_hi,
        "wr_lo": wr_lo,
        "b_router": jnp.full((1, LANES), -1e30, F32).at[0, :N_EXPERTS].set(b_router[l]),
    }


def kernel(x_prompt, x_sample, state_delta, c, c_ctx, w_ada, b_ada, w_in, conv_dw, conv_b, conv_ln_g, conv_ln_b, dn_conv, dn_a_log, dn_dt_bias, dn_norm_g, gm_ln_g, gm_ln_b, gm_ws, gm_bs, w_out, ln1_g, ln1_b, ln2_g, ln2_b, w_router, b_router, w_gu, b_gu, w_down, b_down):
    batch, seq, d = x_prompt.shape
    dec_batch, dec_seq, _ = x_sample.shape
    n_ctx = batch * seq
    n_den = dec_batch * dec_seq
    depth = w_in.shape[0]

    assert 1 + dec_batch <= MOD_ROWS and seq % MIX_TILE == 0 and dec_seq % TOK_TILE == 0 and n_ctx % TOK_TILE == 0
    cond = jnp.zeros((MOD_ROWS, d), F32).at[0].set(c_ctx).at[1:1 + dec_batch].set(c)
    mod_all = _modulation(cond, w_ada, b_ada)
    b_gu_r = b_gu.reshape(depth, N_EXPERTS, 1, 2 * D_FF)
    b_down_r = b_down.reshape(depth, N_EXPERTS, 1, d)

    nt = n_ctx + n_den
    x_src = (x_prompt.reshape(n_ctx, d), x_sample.reshape(n_den, d), _grid_pos_embed(dec_seq, d))
    ctx_states = []
    for l in range(depth):
        lp = _layer_params(l, w_in, conv_dw, conv_b, conv_ln_g, conv_ln_b, dn_conv, dn_a_log, dn_dt_bias,
                           dn_norm_g, gm_ln_g, gm_ln_b, gm_ws, gm_bs, w_out, ln1_g, ln1_b, ln2_g, ln2_b,
                           w_router, b_router)
        mod = mod_all[l]
        proj = _in_projection(x_src, nt, mod, lp["w_in_r"], n_ctx, dec_seq)
        ya, yc, q, k, v, gates = _local_mixers(proj, lp, n_ctx // MIX_TILE, seq // MIX_TILE, dec_seq // MIX_TILE)
        o_f, o_b, s_fin = _deltanet(q, k, v, gates, state_delta, l, batch, seq, dec_batch, dec_seq)
        ctx_states.append(s_fin)
        x1, hffn, route, counts = _out_projection(x_src, ya, o_f, o_b, proj, yc, mod, lp, n_ctx, dec_seq)
        slot_tok, pair_slot, block_e, n_used = _route(route, counts)
        xg = hffn.at[slot_tok].get(mode="promise_in_bounds")
        yb = _experts(xg, block_e, n_used, w_gu, b_gu_r, w_down, b_down_r, l)
        yg = yb.at[jnp.transpose(pair_slot)].get(mode="promise_in_bounds")
        out = _combine(yg, route, x1, mod, lp["ln2"], n_ctx, dec_seq, split_streams=(l == depth - 1))
        x_src = (out,)

    new_state = jnp.stack(ctx_states, axis=1).astype(x_prompt.dtype)
    return (out[0].reshape(batch, seq, d), out[1].reshape(dec_batch, dec_seq, d), new_state)
```
